```python
import math
import jax, jax.numpy as jnp
from jax import lax
import numpy as np

D_MODEL = 1024
BATCH = 4
SEQ = 4096
DEPTH = 2

HEAD_DIM = 64
N_MIXERS = 2
DSW_GROUPS = ((128, 1), (512, 4), (2048, 16))
DSW_N_GROUPS = len(DSW_GROUPS)
DSW_HEADS_PER_GROUP = D_MODEL // (2 * HEAD_DIM)
DSW_HEADS = DSW_N_GROUPS * DSW_HEADS_PER_GROUP
DSW_OUT_WIDTH = DSW_HEADS_PER_GROUP * HEAD_DIM
MOBA_HEADS = D_MODEL // HEAD_DIM
MOBA_BLOCK = 256
MOBA_TOPK = 3
MOBA_Q_CHUNK = 32
D_FF = 4 * D_MODEL
REL_BUCKETS = 32
REL_MAX_DISTANCE = 2048
BIAS_HEADS = max(DSW_HEADS, MOBA_HEADS)

N_A_LAYERS = (DEPTH + 1) // 2
N_B_LAYERS = DEPTH // 2
EPS = 1e-6
NEG = -1e30
SCALE = HEAD_DIM ** -0.5

kernel_name = "hybrid_dilated_moba_sqrelu"


def rmsnorm(x, g):
    xf = x.astype(jnp.float32)
    y = xf * lax.rsqrt(jnp.mean(xf * xf, axis=-1, keepdims=True) + EPS)
    return (y * g.astype(jnp.float32)).astype(x.dtype)


def t5_bucket(dist):
    n = jnp.maximum(dist, 0)
    max_exact = REL_BUCKETS // 2
    nf = jnp.maximum(n, 1).astype(jnp.float32)
    large = max_exact + (jnp.log(nf / max_exact) / math.log(REL_MAX_DISTANCE / max_exact)
                         * (REL_BUCKETS - max_exact)).astype(jnp.int32)
    large = jnp.minimum(large, REL_BUCKETS - 1)
    return jnp.where(n < max_exact, n, large)


def softmax_attend(logits, v, eq):
    m = jnp.max(logits, axis=-1, keepdims=True)
    p = jnp.exp(logits - m)
    den = jnp.sum(p, axis=-1, keepdims=True)
    o = jnp.einsum(eq, p / den, v.astype(jnp.float32))
    return o, (m + jnp.log(den))[..., 0]


def dsw_group(q, k, v, rel_bias, col0, window, dilation):
    B, H, S, hd = q.shape
    n = window // dilation
    blk = n
    L = S // dilation
    nb = -(-L // blk)
    Lp = nb * blk

    def to_sub(t):
        t = t.reshape(B, H, L, dilation, hd).transpose(0, 1, 3, 2, 4)
        return jnp.pad(t, ((0, 0), (0, 0), (0, 0), (0, Lp - L), (0, 0)))

    def band(t):
        t = jnp.pad(t, ((0, 0), (0, 0), (0, 0), (blk, 0), (0, 0))).reshape(B, H, dilation, nb + 1, blk, hd)
        return jnp.concatenate([t[:, :, :, :-1], t[:, :, :, 1:]], axis=4)

    qs, ks, vs = to_sub(q), to_sub(k), to_sub(v)
    qb = qs.reshape(B, H, dilation, nb, blk, hd)
    kb, vb = band(ks), band(vs)
    logits = jnp.einsum('bhrnid,bhrnjd->bhrnij', qb, kb, preferred_element_type=jnp.float32) * SCALE

    i = jnp.arange(blk)[:, None]
    j = jnp.arange(2 * blk)[None, :]
    dist = blk + i - j
    in_band = (dist >= 0) & (dist <= n)
    first = (jnp.arange(nb) == 0)[:, None, None] & (j < blk)[None]
    mask = in_band[None] & ~first
    bias = rel_bias[t5_bucket(dist * dilation)][..., col0:col0 + H].transpose(2, 0, 1)
    logits = jnp.where(mask[None, None, None], logits + bias[None, :, None, None], NEG)

    o, lse = softmax_attend(logits, vb, 'bhrnij,bhrnjd->bhrnid')
    o = o.reshape(B, H, dilation, Lp, hd)[:, :, :, :L].transpose(0, 1, 3, 2, 4).reshape(B, H, S, hd)
    lse = lse.reshape(B, H, dilation, Lp)[..., :L].transpose(0, 1, 3, 2).reshape(B, H, S)
    return o, lse


def dsw_mixer(h, w_qkv, q_gain, k_gain, w_o, rel_bias):
    B, S, _ = h.shape
    G, Hg = DSW_N_GROUPS, DSW_HEADS_PER_GROUP
    qkv = (h @ w_qkv).reshape(B, S, 3, G, Hg, HEAD_DIM)
    q = rmsnorm(qkv[:, :, 0], q_gain)
    k = rmsnorm(qkv[:, :, 1], k_gain)
    v = qkv[:, :, 2]
    outs, lses = [], []
    for g, (window, dilation) in enumerate(DSW_GROUPS):
        o, lse = dsw_group(q[:, :, g].transpose(0, 2, 1, 3), k[:, :, g].transpose(0, 2, 1, 3),
                           v[:, :, g].transpose(0, 2, 1, 3), rel_bias, g * Hg, window, dilation)
        outs.append(o)
        lses.append(lse)
    wts = jax.nn.softmax(jnp.stack(lses), axis=0)
    o = jnp.einsum('gbhs,gbhsd->bshd', wts, jnp.stack(outs)).reshape(B, S, DSW_OUT_WIDTH)
    return o.astype(h.dtype) @ w_o


def gather_blocks(blocks, idx):
    return jax.vmap(jax.vmap(lambda bl, ix: bl[ix]))(blocks, idx)


def moba_mixer(h, w_qkv, q_gain, k_gain, w_o, rel_bias):
    B, S, _ = h.shape
    H, hd, blk = MOBA_HEADS, HEAD_DIM, MOBA_BLOCK
    qkv = (h @ w_qkv).reshape(B, S, 3, H, hd)
    q = rmsnorm(qkv[:, :, 0], q_gain).transpose(0, 2, 1, 3)
    k = rmsnorm(qkv[:, :, 1], k_gain).transpose(0, 2, 1, 3)
    v = qkv[:, :, 2].transpose(0, 2, 1, 3)
    nblk = -(-S // blk)
    Sp = nblk * blk
    pad = ((0, 0), (0, 0), (0, Sp - S), (0, 0))
    q, k, v = jnp.pad(q, pad), jnp.pad(k, pad), jnp.pad(v, pad)
    qb = q.reshape(B, H, nblk, blk, hd)
    kb = k.reshape(B, H, nblk, blk, hd)
    vb = v.reshape(B, H, nblk, blk, hd)
    pos = jnp.arange(Sp)
    qblk = pos // blk
    table_h = rel_bias[:, :H].T

    ii = jnp.arange(blk)
    own_bias = table_h[:, t5_bucket(ii[:, None] - ii[None, :])]
    lo = jnp.einsum('bhnid,bhnjd->bhnij', qb, kb, preferred_element_type=jnp.float32) * SCALE
    lo = jnp.where(ii[:, None] >= ii[None, :], lo + own_bias[None, :, None], NEG)
    o_own, lse_own = softmax_attend(lo, vb, 'bhnij,bhnjd->bhnid')
    o_own = o_own.reshape(B, H, Sp, hd)
    lse_own = lse_own.reshape(B, H, Sp)

    kmean = jnp.mean(kb.astype(jnp.float32), axis=3)
    gate = jnp.einsum('bhsd,bhnd->bhsn', q.astype(jnp.float32), kmean)
    past = jnp.arange(nblk)[None, :] < qblk[:, None]
    gate = jnp.where(past, gate, -jnp.inf)
    topk = min(MOBA_TOPK, nblk)
    _, sel = lax.top_k(gate, topk)
    sel_valid = sel < qblk[:, None]

    C = MOBA_Q_CHUNK
    nC = Sp // C

    def chunk(t):
        return jnp.moveaxis(t.reshape(B, H, nC, C, *t.shape[3:]), 2, 0)

    jj = jnp.arange(blk)
    head_ix = jnp.arange(H)[None, :, None, None]

    def attend(args):
        qc, selc, validc, posc = args
        flat = selc.reshape(B, H, C * topk)
        kg = gather_blocks(kb, flat).reshape(B, H, C, topk * blk, hd)
        vg = gather_blocks(vb, flat).reshape(B, H, C, topk * blk, hd)
        logits = jnp.einsum('bhcd,bhckd->bhck', qc, kg, preferred_element_type=jnp.float32) * SCALE
        kpos = (selc[..., None] * blk + jj).reshape(B, H, C, topk * blk)
        bias = table_h[head_ix, t5_bucket(posc[None, None, :, None] - kpos)]
        valid = jnp.repeat(validc, blk, axis=-1)
        logits = jnp.where(valid, logits + bias, NEG)
        return softmax_attend(logits, vg, 'bhck,bhckd->bhcd')

    o_sel, lse_sel = lax.map(attend, (chunk(q), chunk(sel), chunk(sel_valid), pos.reshape(nC, C)))
    o_sel = jnp.moveaxis(o_sel, 0, 2).reshape(B, H, Sp, hd)
    lse_sel = jnp.moveaxis(lse_sel, 0, 2).reshape(B, H, Sp)

    lse = jnp.logaddexp(lse_own, lse_sel)
    o = jnp.exp(lse_own - lse)[..., None] * o_own + jnp.exp(lse_sel - lse)[..., None] * o_sel
    o = o[:, :, :S].transpose(0, 2, 1, 3).reshape(B, S, H * hd)
    return o.astype(h.dtype) @ w_o


def sq_relu_mlp(h, w1, w2):
    return jnp.square(jax.nn.relu(h @ w1)) @ w2


def setup_inputs(seed: int = 0) -> dict:
    key = jax.random.key(seed)
    ks = jax.random.split(key, 16)
    D = D_MODEL
    nrm = lambda k, shape, s: jax.random.normal(k, shape, jnp.float32) * s
    a_qkv_cols = 3 * DSW_HEADS * HEAD_DIM
    b_qkv_cols = 3 * MOBA_HEADS * HEAD_DIM
    return {
        "x": nrm(ks[0], (BATCH, SEQ, D), 1.0),
        "rel_bias": nrm(ks[1], (REL_BUCKETS, BIAS_HEADS), 0.3),
        "norm_mix": 1.0 + nrm(ks[2], (DEPTH, D), 0.02),
        "norm_ffn": 1.0 + nrm(ks[3], (DEPTH, D), 0.02),
        "a_w_qkv": nrm(ks[4], (N_A_LAYERS, D, a_qkv_cols), D ** -0.5),
        "a_q_gain": 1.0 + nrm(ks[5], (N_A_LAYERS, HEAD_DIM), 0.02),
        "a_k_gain": 1.0 + nrm(ks[6], (N_A_LAYERS, HEAD_DIM), 0.02),
        "a_w_o": nrm(ks[7], (N_A_LAYERS, DSW_OUT_WIDTH, D), DSW_OUT_WIDTH ** -0.5),
        "b_w_qkv": nrm(ks[8], (N_B_LAYERS, D, b_qkv_cols), D ** -0.5),
        "b_q_gain": 1.0 + nrm(ks[9], (N_B_LAYERS, HEAD_DIM), 0.02),
        "b_k_gain": 1.0 + nrm(ks[10], (N_B_LAYERS, HEAD_DIM), 0.02),
        "b_w_o": nrm(ks[11], (N_B_LAYERS, MOBA_HEADS * HEAD_DIM, D), (MOBA_HEADS * HEAD_DIM) ** -0.5),
        "ffn_w1": nrm(ks[12], (DEPTH, D, D_FF), D ** -0.5),
        "ffn_w2": nrm(ks[13], (DEPTH, D_FF, D), 0.5 * D_FF ** -0.5),
    }


def reference(x, rel_bias, norm_mix, norm_ffn, a_w_qkv, a_q_gain, a_k_gain, a_w_o,
              b_w_qkv, b_q_gain, b_k_gain, b_w_o, ffn_w1, ffn_w2):
    h = x
    for i in range(DEPTH):
        u = rmsnorm(h, norm_mix[i])
        li = i // N_MIXERS
        if i % N_MIXERS == 0:
            h = h + dsw_mixer(u, a_w_qkv[li], a_q_gain[li], a_k_gain[li], a_w_o[li], rel_bias)
        else:
            h = h + moba_mixer(u, b_w_qkv[li], b_q_gain[li], b_k_gain[li], b_w_o[li], rel_bias)
        h = h + sq_relu_mlp(rmsnorm(h, norm_ffn[i]), ffn_w1[i], ffn_w2[i])
    return h
```

```python
import functools
import math

import jax
import jax.numpy as jnp
from jax import lax
from jax.experimental import pallas as pl
from jax.experimental.pallas import tpu as pltpu

HEAD_DIM = 64
LANES = 128
DSW_GROUPS = ((128, 1), (512, 4), (2048, 16))
DSW_BLK = 128
DSW_HEADS_PER_GROUP = 8
MOBA_BLOCK = 256
MOBA_TOPK = 3
REL_BUCKETS = 32
REL_MAX_DISTANCE = 2048
EPS = 1e-6
NEG = -1e30
SCALE = HEAD_DIM ** -0.5
LOG2E = 1.4426950408889634
VMEM_LIMIT_BYTES = 56 * 1024 * 1024

F32 = jnp.float32
BF16 = jnp.bfloat16
NT_DIMS = (((1,), (1,)), ((), ()))


def _params(n_axes):
    return pltpu.CompilerParams(dimension_semantics=("arbitrary",) * n_axes,
                                vmem_limit_bytes=VMEM_LIMIT_BYTES)


def _t5_bucket(dist):
    n = jnp.maximum(dist, 0)
    max_exact = REL_BUCKETS // 2
    nf = jnp.maximum(n, 1).astype(F32)
    large = max_exact + (jnp.log(nf / max_exact) / math.log(REL_MAX_DISTANCE / max_exact)
                         * (REL_BUCKETS - max_exact)).astype(jnp.int32)
    large = jnp.minimum(large, REL_BUCKETS - 1)
    return jnp.where(n < max_exact, n, large)


def _lookup(bucket, table_t):
    acc = jnp.zeros(bucket.shape, F32)
    for b in range(REL_BUCKETS):
        acc = jnp.where(bucket == b, table_t[:, b:b + 1], acc)
    return acc


def _bias_tables_kernel(tab_ref, dsw_ref, moba_ref, *, n_moba_heads, seq):
    tab = tab_ref[...]
    n_heads = dsw_ref.shape[0]
    m = lax.broadcasted_iota(jnp.int32, (n_heads, 2 * DSW_BLK), 1)
    row = lax.broadcasted_iota(jnp.int32, (n_heads, 2 * DSW_BLK), 0)
    dil = jnp.where(row < DSW_HEADS_PER_GROUP, DSW_GROUPS[0][1],
                    jnp.where(row < 2 * DSW_HEADS_PER_GROUP, DSW_GROUPS[1][1], DSW_GROUPS[2][1]))
    sub = DSW_BLK - m
    vals = _lookup(_t5_bucket(sub * dil), tab)
    dsw_ref[...] = jnp.where(sub >= 0, vals, NEG)
    width = moba_ref.shape[1]
    t = lax.broadcasted_iota(jnp.int32, (n_moba_heads, width), 1)
    dist = t - MOBA_BLOCK
    vals = _lookup(_t5_bucket(dist), tab[:n_moba_heads])
    moba_ref[...] = jnp.where(dist >= 0, vals, NEG)


def _bias_tables(rel_bias, n_moba_heads, seq):
    n_heads = rel_bias.shape[1]
    width = seq + MOBA_BLOCK
    return pl.pallas_call(
        functools.partial(_bias_tables_kernel, n_moba_heads=n_moba_heads, seq=seq),
        out_shape=(jax.ShapeDtypeStruct((n_heads, 2 * DSW_BLK), F32),
                   jax.ShapeDtypeStruct((n_moba_heads, width), F32)),
        name="bias_tables",
    )(rel_bias.T)


def _toeplitz(u_row, rows):
    x = jnp.broadcast_to(u_row, (rows, u_row.shape[1]))
    return pltpu.roll(x, 0, 1, stride=1, stride_axis=0)


def _dsw_bias_kernel(vec_ref, out_ref):
    lane = lax.broadcasted_iota(jnp.int32, (DSW_BLK, 2 * DSW_BLK), 1)
    for h in range(DSW_HEADS_PER_GROUP):
        tile = _toeplitz(vec_ref[0, h:h + 1, :], DSW_BLK)
        out_ref[0, h, 0] = tile
        out_ref[0, h, 1] = jnp.where(lane < DSW_BLK, NEG, tile)


def _dsw_bias_tiles(dsw_vec):
    n_groups = len(DSW_GROUPS)
    hg = DSW_HEADS_PER_GROUP
    vec = dsw_vec.reshape(n_groups, hg, 2 * DSW_BLK)
    return pl.pallas_call(
        _dsw_bias_kernel,
        grid=(n_groups,),
        in_specs=[pl.BlockSpec((1, hg, 2 * DSW_BLK), lambda g: (g, 0, 0))],
        out_specs=pl.BlockSpec((1, hg, 2, DSW_BLK, 2 * DSW_BLK), lambda g: (g, 0, 0, 0, 0)),
        out_shape=jax.ShapeDtypeStruct((n_groups, hg, 2, DSW_BLK, 2 * DSW_BLK), F32),
        compiler_params=_params(1),
        name="dsw_bias_tiles",
    )(vec)


def _rmsnorm_bf16(x, gain):
    ms = jnp.mean(x * x, axis=-1, keepdims=True)
    return (x * lax.rsqrt(ms + EPS) * gain).astype(BF16)


def _qkv_kernel(x_ref, g_ref, w_ref, qg_ref, kg_ref, bd_ref, q_ref, k_ref, v_ref, *, width):
    x = x_ref[...].reshape(x_ref.shape[-2:])
    u = _rmsnorm_bf16(x, g_ref[...])
    chunk = bd_ref.shape[0]
    for part, (gain_ref, out_ref) in enumerate(((qg_ref, q_ref), (kg_ref, k_ref))):
        for c in range(width // chunk):
            col = part * width + c * chunk
            y = jnp.dot(u, w_ref[:, col:col + chunk], preferred_element_type=F32)
            ms = jnp.dot((y * y).astype(BF16), bd_ref[...], preferred_element_type=F32)
            yn = y * lax.rsqrt(ms + EPS) * gain_ref[...]
            out_ref[:, c * chunk:(c + 1) * chunk] = yn.astype(BF16)
    for c in range(width // chunk):
        col = 2 * width + c * chunk
        y = jnp.dot(u, w_ref[:, col:col + chunk], preferred_element_type=F32)
        v_ref[:, c * chunk:(c + 1) * chunk] = y.astype(BF16)


def _qkv_project(x_view, x_block, x_index, grid, out_index, n_rows, rows_per_step,
                 gain, w, q_gain_row, k_gain_row, bd, name):
    width = w.shape[1] // 3
    d_model = w.shape[0]
    const = lambda *_: (0, 0)
    out_spec = pl.BlockSpec((rows_per_step, width), out_index)
    out_sds = jax.ShapeDtypeStruct((n_rows, width), BF16)
    return pl.pallas_call(
        functools.partial(_qkv_kernel, width=width),
        grid=grid,
        in_specs=[pl.BlockSpec(x_block, x_index),
                  pl.BlockSpec((1, d_model), const),
                  pl.BlockSpec(w.shape, const),
                  pl.BlockSpec(q_gain_row.shape, const),
                  pl.BlockSpec(k_gain_row.shape, const),
                  pl.BlockSpec(bd.shape, const)],
        out_specs=(out_spec, out_spec, out_spec),
        out_shape=(out_sds, out_sds, out_sds),
        compiler_params=_params(len(grid)),
        name=name,
    )(x_view, gain, w, q_gain_row, k_gain_row, bd)


def _dsw_attn_kernel(q_ref, kc_ref, vc_ref, kp_ref, vp_ref, bias_ref, o_ref, lse_ref, *,
                     tq, blocks_per_seq):
    g = pl.program_id(0)
    t = pl.program_id(1)
    n_blk = tq // DSW_BLK
    seq_mask = jnp.where(g == 0, blocks_per_seq[0] - 1,
                         jnp.where(g == 1, blocks_per_seq[1] - 1, blocks_per_seq[2] - 1))
    lane = lax.broadcasted_iota(jnp.int32, (DSW_BLK, LANES), 1)
    low_half = lane < HEAD_DIM
    n_pairs = q_ref.shape[2] // LANES
    for qi in range(n_blk):
        rows = slice(qi * DSW_BLK, (qi + 1) * DSW_BLK)
        first = (jnp.bitwise_and(t * n_blk + qi, seq_mask) == 0).astype(jnp.int32)
        lse_tile = jnp.zeros((DSW_BLK, LANES), F32)
        for hp in range(n_pairs):
            cols = slice(hp * LANES, (hp + 1) * LANES)
            q2 = q_ref[0, rows, cols]
            if qi == 0:
                k_prev, v_prev = kp_ref[0, :, cols], vp_ref[0, :, cols]
            else:
                prev = slice((qi - 1) * DSW_BLK, qi * DSW_BLK)
                k_prev, v_prev = kc_ref[0, prev, cols], vc_ref[0, prev, cols]
            k_cat = jnp.concatenate([k_prev, kc_ref[0, rows, cols]], axis=0)
            v_cat = jnp.concatenate([v_prev, vc_ref[0, rows, cols]], axis=0)
            outs = []
            for hh in range(2):
                head_lanes = low_half if hh == 0 else jnp.logical_not(low_half)
                qm = jnp.where(head_lanes, q2, jnp.zeros_like(q2))
                s = lax.dot_general(qm, k_cat, NT_DIMS, preferred_element_type=F32)
                s = s + bias_ref[0, hp * 2 + hh, first]
                m = jnp.max(s, axis=-1, keepdims=True)
                p = jnp.exp(s - m)
                den = jnp.sum(p, axis=-1, keepdims=True)
                pv = jnp.dot(p.astype(BF16), v_cat, preferred_element_type=F32)
                outs.append(pv / den)
                lse = m + jnp.log(den)
                lse_tile = jnp.where(lane == hp * 2 + hh, lse, lse_tile)
            o_ref[0, rows, cols] = jnp.where(low_half, outs[0], outs[1])
        lse_ref[0, rows, :] = lse_tile


def _dsw_attention(q, k, v, bias_tiles, tq, blocks_per_seq):
    n_groups, n_rows, width = q.shape
    assert all(b & (b - 1) == 0 for b in blocks_per_seq)
    n_blk = tq // DSW_BLK
    cur = pl.BlockSpec((1, tq, width), lambda g, t: (g, t, 0))
    prev = pl.BlockSpec((1, DSW_BLK, width), lambda g, t: (g, jnp.maximum(t * n_blk - 1, 0), 0))
    hg = DSW_HEADS_PER_GROUP
    return pl.pallas_call(
        functools.partial(_dsw_attn_kernel, tq=tq, blocks_per_seq=tuple(blocks_per_seq)),
        grid=(n_groups, n_rows // tq),
        in_specs=[cur, cur, cur, prev, prev,
                  pl.BlockSpec((1, hg, 2, DSW_BLK, 2 * DSW_BLK), lambda g, t: (g, 0, 0, 0, 0))],
        out_specs=(pl.BlockSpec((1, tq, width), lambda g, t: (g, t, 0)),
                   pl.BlockSpec((1, tq, LANES), lambda g, t: (g, t, 0))),
        out_shape=(jax.ShapeDtypeStruct((n_groups, n_rows, width), F32),
                   jax.ShapeDtypeStruct((n_groups, n_rows, LANES), F32)),
        compiler_params=_params(2),
        name="dsw_attention",
    )(q, k, v, k, v, bias_tiles)


def _merge_wo_kernel(o0_ref, o1_ref, o2_ref, l0_ref, l1_ref, l2_ref, x_ref, wo_ref, e_ref, out_ref):
    rows, width = o0_ref.shape[-2:]
    lses = [r[...].reshape(rows, LANES) for r in (l0_ref, l1_ref, l2_ref)]
    outs = [r[...].reshape(rows, width) for r in (o0_ref, o1_ref, o2_ref)]
    mx = jnp.maximum(jnp.maximum(lses[0], lses[1]), lses[2])
    es = [jnp.exp(l - mx) for l in lses]
    den = es[0] + es[1] + es[2]
    merged = jnp.zeros((rows, width), F32)
    for e, o in zip(es, outs):
        wgt = e / den
        hi = wgt.astype(BF16)
        lo = (wgt - hi.astype(F32)).astype(BF16)
        spread = (jnp.dot(hi, e_ref[...], preferred_element_type=F32)
                  + jnp.dot(lo, e_ref[...], preferred_element_type=F32))
        merged = merged + spread * o
    x = x_ref[...].reshape(rows, x_ref.shape[-1])
    y = x + jnp.dot(merged.astype(BF16), wo_ref[...], preferred_element_type=F32)
    out_ref[...] = y.reshape(out_ref.shape)


def _merge_wo(o, lse, h, wo, batch, seq):
    width = o.shape[-1]
    d_model = h.shape[-1]
    d1, d2 = DSW_GROUPS[1][1], DSW_GROUPS[2][1]
    rows = seq // d2
    sub = d2 // d1
    o_a = o.reshape(3, batch, rows, d2 * width)
    o_b = o.reshape(3, batch, d1, rows, sub * width)
    o_c = o.reshape(3, batch, d2, rows, width)
    l_a = lse.reshape(3, batch, rows, d2 * LANES)
    l_b = lse.reshape(3, batch, d1, rows, sub * LANES)
    l_c = lse.reshape(3, batch, d2, rows, LANES)
    h_v = h.reshape(batch, rows, d2 * d_model)
    head_of_lane = jnp.arange(width) // HEAD_DIM
    expand = (jnp.arange(LANES)[:, None] == head_of_lane[None, :]).astype(BF16)
    spec_a = lambda w: pl.BlockSpec((1, 1, rows, w), lambda b, c: (0, b, 0, c))
    spec_b = lambda w: pl.BlockSpec((1, 1, 1, rows, w), lambda b, c: (1, b, c % d1, 0, c // d1))
    spec_c = lambda w: pl.BlockSpec((1, 1, 1, rows, w), lambda b, c: (2, b, c, 0, 0))
    const = lambda b, c: (0, 0)
    out = pl.pallas_call(
        _merge_wo_kernel,
        grid=(batch, d2),
        in_specs=[spec_a(width), spec_b(width), spec_c(width),
                  spec_a(LANES), spec_b(LANES), spec_c(LANES),
                  pl.BlockSpec((1, rows, d_model), lambda b, c: (b, 0, c)),
                  pl.BlockSpec(wo.shape, const),
                  pl.BlockSpec(expand.shape, const)],
        out_specs=pl.BlockSpec((1, rows, d_model), lambda b, c: (b, 0, c)),
        out_shape=jax.ShapeDtypeStruct(h_v.shape, F32),
        compiler_params=_params(2),
        name="dsw_merge_wo",
    )(o_a, o_b, o_c, l_a, l_b, l_c, h_v, wo, expand)
    return out.reshape(h.shape)


def _ffn_kernel(*refs, has_attn, ff_chunk):
    if has_attn:
        h_ref, a_ref, wo_ref, g_ref, w1_ref, w2_ref, out_ref = refs
        h = h_ref[...] + jnp.dot(a_ref[...], wo_ref[...], preferred_element_type=F32)
    else:
        h_ref, g_ref, w1_ref, w2_ref, out_ref = refs
        h = h_ref[...]
    u = _rmsnorm_bf16(h, g_ref[...])
    acc = h
    for c in range(w1_ref.shape[1] // ff_chunk):
        cols = slice(c * ff_chunk, (c + 1) * ff_chunk)
        a = jnp.dot(u, w1_ref[:, cols], preferred_element_type=F32)
        a = jnp.square(jnp.maximum(a, 0.0)).astype(BF16)
        acc = acc + jnp.dot(a, w2_ref[cols, :], preferred_element_type=F32)
    out_ref[...] = acc


def _ffn(h, gain, w1, w2, tm, attn=None, wo=None, name="ffn"):
    n_rows, d_model = h.shape
    const = lambda i: (0, 0)
    row_spec = pl.BlockSpec((tm, d_model), lambda i: (i, 0))
    in_specs = [row_spec]
    args = [h]
    if attn is not None:
        in_specs += [pl.BlockSpec((tm, attn.shape[1]), lambda i: (i, 0)), pl.BlockSpec(wo.shape, const)]
        args += [attn, wo]
    in_specs += [pl.BlockSpec((1, d_model), const), pl.BlockSpec(w1.shape, const), pl.BlockSpec(w2.shape, const)]
    args += [gain, w1, w2]
    return pl.pallas_call(
        functools.partial(_ffn_kernel, has_attn=attn is not None, ff_chunk=1024),
        grid=(n_rows // tm,),
        in_specs=in_specs,
        out_specs=row_spec,
        out_shape=jax.ShapeDtypeStruct(h.shape, F32),
        compiler_params=_params(1),
        name=name,
    )(*args)


def _moba_kernel(q_ref, k_ref, v_ref, wn_ref, o_ref,
                 bias_s, kp_s, vt_s, kmh_s, kml_s, m_s, acc_s, *, n_blocks):
    b = pl.program_id(1)
    n = pl.program_id(2)
    blk = MOBA_BLOCK
    lane = lax.broadcasted_iota(jnp.int32, (blk, LANES), 1)
    low_lanes = lane < HEAD_DIM
    row_t = lax.broadcasted_iota(jnp.int32, (LANES, blk), 0)

    @pl.when(jnp.logical_and(b == 0, n == 0))
    def _build_bias():
        for hh in range(2):
            for dlt in range(n_blocks):
                u = jnp.concatenate([wn_ref[0, hh:hh + 1, (dlt + 1) * blk:(dlt + 2) * blk],
                                     wn_ref[0, hh:hh + 1, dlt * blk:(dlt + 1) * blk]], axis=1)
                bias_s[hh, dlt] = _toeplitz(u, blk)[:, :blk] * LOG2E

    @pl.when(n == 0)
    def _build_kv():
        sub = lax.broadcasted_iota(jnp.int32, (n_blocks, LANES), 0)
        kmean = jnp.zeros((n_blocks, LANES), F32)
        for j in range(n_blocks):
            rows = slice(j * blk, (j + 1) * blk)
            kj = k_ref[0, rows, :].astype(F32)
            vj_t = v_ref[0, rows, :].astype(F32).T
            kmean = jnp.where(sub == j, jnp.mean(kj, axis=0, keepdims=True), kmean)
            kp_s[0, j] = jnp.where(low_lanes, kj, (lane == HEAD_DIM + j).astype(F32)).astype(BF16)
            kp_s[1, j] = jnp.where(low_lanes, (lane == j).astype(F32), kj).astype(BF16)
            vt_s[0, j] = jnp.where(row_t < HEAD_DIM, vj_t, (row_t == HEAD_DIM).astype(F32)).astype(BF16)
            vt_s[1, j] = jnp.where(row_t < HEAD_DIM, (row_t == 0).astype(F32), vj_t).astype(BF16)
        hi = kmean.astype(BF16)
        kmh_s[...] = hi
        kml_s[...] = (kmean - hi.astype(F32)).astype(BF16)

    q2 = q_ref[...]
    zero_q = jnp.zeros_like(q2)
    blk_row = lax.broadcasted_iota(jnp.int32, (n_blocks, blk), 0)
    blk_row_f = blk_row.astype(F32)
    q_aug = []
    for hh in range(2):
        own = low_lanes if hh == 0 else jnp.logical_not(low_lanes)
        qm = jnp.where(own, q2, zero_q)
        gate = (lax.dot_general(kmh_s[...], qm, NT_DIMS, preferred_element_type=F32)
                + lax.dot_general(kml_s[...], qm, NT_DIMS, preferred_element_type=F32))
        gate = jnp.where(blk_row < n, gate, -jnp.inf)
        chosen = blk_row == n
        for _ in range(MOBA_TOPK):
            best = jnp.max(gate, axis=0, keepdims=True)
            cand = jnp.where(gate == best, blk_row_f, float(n_blocks))
            first = jnp.min(cand, axis=0, keepdims=True)
            pick = jnp.logical_and(blk_row_f == first, best > -jnp.inf)
            chosen = jnp.logical_or(chosen, pick)
            gate = jnp.where(pick, -jnp.inf, gate)
        pen_t = jnp.where(chosen, 0.0, NEG)
        pad_lo = HEAD_DIM if hh == 0 else 0
        pieces = []
        if pad_lo:
            pieces.append(jnp.zeros((pad_lo, blk), F32))
        pieces.append(pen_t)
        pieces.append(jnp.zeros((LANES - pad_lo - n_blocks, blk), F32))
        pen = jnp.concatenate(pieces, axis=0).T
        q_aug.append(jnp.where(own, q2, pen.astype(BF16)))

    m_s[...] = jnp.full(m_s.shape, -3.0e38, F32)
    acc_s[...] = jnp.zeros(acc_s.shape, F32)

    def attend(j, dlt):
        for hh in range(2):
            s_t = lax.dot_general(kp_s[hh, j], q_aug[hh], NT_DIMS, preferred_element_type=F32)
            s_t = s_t + bias_s[hh, dlt]
            m_old = m_s[hh]
            m_new = jnp.maximum(m_old, jnp.max(s_t, axis=0, keepdims=True))
            p_t = jnp.exp2(s_t - m_new)
            alpha = jnp.exp2(m_old - m_new)
            acc_s[hh] = acc_s[hh] * alpha + jnp.dot(vt_s[hh, j], p_t.astype(BF16),
                                                    preferred_element_type=F32)
            m_s[hh] = m_new

    attend(n, 0)

    def body(j, carry):
        attend(j, n - j)
        return carry

    lax.fori_loop(0, n, body, 0)

    acc0 = acc_s[0]
    acc1 = acc_s[1]
    o_t = jnp.concatenate([acc0[:HEAD_DIM] / acc0[HEAD_DIM:HEAD_DIM + 1],
                           acc1[HEAD_DIM:] / acc1[0:1]], axis=0)
    o_ref[...] = o_t.T.astype(o_ref.dtype)


def _moba_attention(q, k, v, moba_vec, batch, seq):
    n_rows, width = q.shape
    n_pairs = width // LANES
    n_blocks = seq // MOBA_BLOCK
    wn = moba_vec.reshape(n_pairs, 2, moba_vec.shape[1])
    kv_spec = pl.BlockSpec((1, seq, LANES), lambda p, b, n: (b, 0, p))
    q_spec = pl.BlockSpec((MOBA_BLOCK, LANES), lambda p, b, n: (b * n_blocks + n, p))
    return pl.pallas_call(
        functools.partial(_moba_kernel, n_blocks=n_blocks),
        grid=(n_pairs, batch, n_blocks),
        in_specs=[q_spec, kv_spec, kv_spec,
                  pl.BlockSpec((1, 2, wn.shape[2]), lambda p, b, n: (p, 0, 0))],
        out_specs=q_spec,
        out_shape=jax.ShapeDtypeStruct((n_rows, width), BF16),
        scratch_shapes=[pltpu.VMEM((2, n_blocks, MOBA_BLOCK, MOBA_BLOCK), F32),
                        pltpu.VMEM((2, n_blocks, MOBA_BLOCK, LANES), BF16),
                        pltpu.VMEM((2, n_blocks, LANES, MOBA_BLOCK), BF16),
                        pltpu.VMEM((n_blocks, LANES), BF16),
                        pltpu.VMEM((n_blocks, LANES), BF16),
                        pltpu.VMEM((2, 1, MOBA_BLOCK), F32),
                        pltpu.VMEM((2, LANES, MOBA_BLOCK), F32)],
        compiler_params=_params(3),
        name="moba_attention",
    )(q, k.reshape(batch, seq, width), v.reshape(batch, seq, width), wn)


def _head_mean_matrix(size):
    idx = jnp.arange(size) // HEAD_DIM
    return (idx[:, None] == idx[None, :]).astype(BF16) * (1.0 / HEAD_DIM)


def _gain_row(gain, size, scale):
    return (jnp.tile(gain.astype(F32), size // HEAD_DIM) * scale).reshape(1, size)


def kernel(x, rel_bias, norm_mix, norm_ffn, a_w_qkv, a_q_gain, a_k_gain, a_w_o,
           b_w_qkv, b_q_gain, b_k_gain, b_w_o, ffn_w1, ffn_w2):
    batch, seq, d_model = x.shape
    n_rows = batch * seq
    n_groups = len(DSW_GROUPS)
    gw = DSW_HEADS_PER_GROUP * HEAD_DIM
    moba_heads = b_w_o.shape[1] // HEAD_DIM
    bd = _head_mean_matrix(2 * LANES)
    blocks_per_seq = [seq // d // DSW_BLK for _, d in DSW_GROUPS]

    dsw_vec, moba_vec = _bias_tables(rel_bias, moba_heads, seq)
    dsw_tiles = _dsw_bias_tiles(dsw_vec)

    h = x.reshape(n_rows, d_model)

    wa = a_w_qkv[0].astype(BF16).reshape(d_model, 3, n_groups, gw)
    qg = _gain_row(a_q_gain[0], 2 * LANES, SCALE)
    kg = _gain_row(a_k_gain[0], 2 * LANES, 1.0)
    gain = norm_mix[0].reshape(1, d_model)
    qs, ks, vs = [], [], []
    for g, (_, dil) in enumerate(DSW_GROUPS):
        sub_len = seq // dil
        tl = min(sub_len, 512)
        steps = sub_len // tl
        w_g = wa[:, :, g, :].reshape(d_model, 3 * gw)
        q_g, k_g, v_g = _qkv_project(
            h.reshape(batch, sub_len, dil * d_model), (1, tl, d_model),
            lambda b, c, t: (b, t, c), (batch, dil, steps),
            lambda b, c, t, dil=dil, steps=steps: ((b * dil + c) * steps + t, 0),
            n_rows, tl, gain, w_g, qg, kg, bd, name=f"dsw_qkv_g{g}")
        qs.append(q_g)
        ks.append(k_g)
        vs.append(v_g)
    o, lse = _dsw_attention(jnp.stack(qs), jnp.stack(ks), jnp.stack(vs), dsw_tiles, tq=256,
                            blocks_per_seq=blocks_per_seq)
    h = _merge_wo(o, lse, h, a_w_o[0].astype(BF16), batch, seq)
    h = _ffn(h, norm_ffn[0].reshape(1, d_model), ffn_w1[0].astype(BF16), ffn_w2[0].astype(BF16),
             tm=512, name="ffn0")

    width = moba_heads * HEAD_DIM
    tm = 512
    q, k, v = _qkv_project(
        h, (tm, d_model), lambda i: (i, 0), (n_rows // tm,), lambda i: (i, 0),
        n_rows, tm, norm_mix[1].reshape(1, d_model), b_w_qkv[0].astype(BF16),
        _gain_row(b_q_gain[0], 2 * LANES, SCALE * LOG2E), _gain_row(b_k_gain[0], 2 * LANES, 1.0),
        bd, name="moba_qkv")
    attn = _moba_attention(q, k, v, moba_vec, batch, seq)
    h = _ffn(h, norm_ffn[1].reshape(1, d_model), ffn_w1[1].astype(BF16), ffn_w2[1].astype(BF16),
             tm=512, attn=attn, wo=b_w_o[0].astype(BF16), name="wo_ffn1")
    return h.reshape(batch, seq, d_model)
```

```python
import functools
import math

import jax
import jax.numpy as jnp
from jax import lax
from jax.experimental import pallas as pl
from jax.experimental.pallas import tpu as pltpu

HEAD_DIM = 64
LANES = 128
DSW_GROUPS = ((128, 1), (512, 4), (2048, 16))
DSW_BLK = 128
DSW_HEADS_PER_GROUP = 8
MOBA_BLOCK = 256
MOBA_TOPK = 3
REL_BUCKETS = 32
REL_MAX_DISTANCE = 2048
EPS = 1e-6
NEG = -1e30
SCALE = HEAD_DIM ** -0.5
LOG2E = 1.4426950408889634
VMEM_LIMIT_BYTES = 56 * 1024 * 1024

F32 = jnp.float32
BF16 = jnp.bfloat16
NT_DIMS = (((1,), (1,)), ((), ()))


def _params(n_axes):
    return pltpu.CompilerParams(dimension_semantics=("arbitrary",) * n_axes,
                                vmem_limit_bytes=VMEM_LIMIT_BYTES)


def _t5_bucket(dist):
    n = jnp.maximum(dist, 0)
    max_exact = REL_BUCKETS // 2
    nf = jnp.maximum(n, 1).astype(F32)
    large = max_exact + (jnp.log(nf / max_exact) / math.log(REL_MAX_DISTANCE / max_exact)
                         * (REL_BUCKETS - max_exact)).astype(jnp.int32)
    large = jnp.minimum(large, REL_BUCKETS - 1)
    return jnp.where(n < max_exact, n, large)


def _lookup(bucket, table_t):
    acc = jnp.zeros(bucket.shape, F32)
    for b in range(REL_BUCKETS):
        acc = jnp.where(bucket == b, table_t[:, b:b + 1], acc)
    return acc


def _bias_tables_kernel(tab_ref, dsw_ref, moba_ref, *, n_moba_heads, seq):
    tab = tab_ref[...]
    n_heads = dsw_ref.shape[0]
    m = lax.broadcasted_iota(jnp.int32, (n_heads, 2 * DSW_BLK), 1)
    row = lax.broadcasted_iota(jnp.int32, (n_heads, 2 * DSW_BLK), 0)
    dil = jnp.where(row < DSW_HEADS_PER_GROUP, DSW_GROUPS[0][1],
                    jnp.where(row < 2 * DSW_HEADS_PER_GROUP, DSW_GROUPS[1][1], DSW_GROUPS[2][1]))
    sub = DSW_BLK - m
    vals = _lookup(_t5_bucket(sub * dil), tab)
    dsw_ref[...] = jnp.where(sub >= 0, vals, NEG)
    width = moba_ref.shape[1]
    t = lax.broadcasted_iota(jnp.int32, (n_moba_heads, width), 1)
    dist = t - MOBA_BLOCK
    vals = _lookup(_t5_bucket(dist), tab[:n_moba_heads])
    moba_ref[...] = jnp.where(dist >= 0, vals, NEG)


def _bias_tables(rel_bias, n_moba_heads, seq):
    n_heads = rel_bias.shape[1]
    width = seq + MOBA_BLOCK
    return pl.pallas_call(
        functools.partial(_bias_tables_kernel, n_moba_heads=n_moba_heads, seq=seq),
        out_shape=(jax.ShapeDtypeStruct((n_heads, 2 * DSW_BLK), F32),
                   jax.ShapeDtypeStruct((n_moba_heads, width), F32)),
        name="bias_tables",
    )(rel_bias.T)


def _toeplitz(u_row, rows):
    x = jnp.broadcast_to(u_row, (rows, u_row.shape[1]))
    return pltpu.roll(x, 0, 1, stride=1, stride_axis=0)


def _dsw_bias_kernel(vec_ref, out_ref):
    lane = lax.broadcasted_iota(jnp.int32, (DSW_BLK, 2 * DSW_BLK), 1)
    for h in range(DSW_HEADS_PER_GROUP):
        tile = _toeplitz(vec_ref[0, h:h + 1, :], DSW_BLK)
        out_ref[0, h, 0] = tile
        out_ref[0, h, 1] = jnp.where(lane < DSW_BLK, NEG, tile)


def _dsw_bias_tiles(dsw_vec):
    n_groups = len(DSW_GROUPS)
    hg = DSW_HEADS_PER_GROUP
    vec = dsw_vec.reshape(n_groups, hg, 2 * DSW_BLK)
    return pl.pallas_call(
        _dsw_bias_kernel,
        grid=(n_groups,),
        in_specs=[pl.BlockSpec((1, hg, 2 * DSW_BLK), lambda g: (g, 0, 0))],
        out_specs=pl.BlockSpec((1, hg, 2, DSW_BLK, 2 * DSW_BLK), lambda g: (g, 0, 0, 0, 0)),
        out_shape=jax.ShapeDtypeStruct((n_groups, hg, 2, DSW_BLK, 2 * DSW_BLK), F32),
        compiler_params=_params(1),
        name="dsw_bias_tiles",
    )(vec)


def _rmsnorm_bf16(x, gain):
    ms = jnp.mean(x * x, axis=-1, keepdims=True)
    return (x * lax.rsqrt(ms + EPS) * gain).astype(BF16)


def _qkv_kernel(x_ref, g_ref, w_ref, qg_ref, kg_ref, bd_ref, *rest, width, dilations):
    n_groups = len(dilations)
    out_refs = rest[:3 * n_groups]
    u_s = rest[3 * n_groups]
    x = x_ref[0]
    rows = x.shape[0]
    ms = jnp.mean(x * x, axis=-1, keepdims=True)
    u = x * lax.rsqrt(ms + EPS) * g_ref[...]
    n_tiles = x.shape[1] // LANES
    if any(d > 1 for d in dilations):
        for j in range(n_tiles):
            u_s[j] = u[:, j * LANES:(j + 1) * LANES]
    chunk = bd_ref.shape[0]
    for gi, d in enumerate(dilations):
        per = rows // d
        if d == 1:
            lhs = u.astype(BF16)
        else:
            lhs = jnp.concatenate(
                [jnp.concatenate([u_s[j, pl.ds(c, per, stride=d), :] for j in range(n_tiles)], axis=1)
                 for c in range(d)], axis=0).astype(BF16)
        q_ref, k_ref, v_ref = out_refs[3 * gi:3 * gi + 3]
        for part, (gain_ref, out_ref) in enumerate(((qg_ref, q_ref), (kg_ref, k_ref), (None, v_ref))):
            for cc in range(width // chunk):
                col = (gi * 3 + part) * width + cc * chunk
                y = jnp.dot(lhs, w_ref[:, col:col + chunk], preferred_element_type=F32)
                if gain_ref is not None:
                    msq = jnp.dot((y * y).astype(BF16), bd_ref[...], preferred_element_type=F32)
                    y = y * lax.rsqrt(msq + EPS) * gain_ref[...]
                y = y.astype(BF16)
                for c in range(d):
                    out_ref[0, c, :, cc * chunk:(cc + 1) * chunk] = y[c * per:(c + 1) * per]


def _qkv_project(x, gain, w, q_gain_row, k_gain_row, bd, dilations, tm, name):
    batch, seq, d_model = x.shape
    width = w.shape[1] // (3 * len(dilations))
    const = lambda b, t: (0, 0)
    out_specs, out_shapes = [], []
    for d in dilations:
        out_specs += [pl.BlockSpec((1, d, tm // d, width), lambda b, t: (b, 0, t, 0))] * 3
        out_shapes += [jax.ShapeDtypeStruct((batch, d, seq // d, width), BF16)] * 3
    outs = pl.pallas_call(
        functools.partial(_qkv_kernel, width=width, dilations=tuple(dilations)),
        grid=(batch, seq // tm),
        in_specs=[pl.BlockSpec((1, tm, d_model), lambda b, t: (b, t, 0)),
                  pl.BlockSpec((1, d_model), const),
                  pl.BlockSpec(w.shape, const),
                  pl.BlockSpec(q_gain_row.shape, const),
                  pl.BlockSpec(k_gain_row.shape, const),
                  pl.BlockSpec(bd.shape, const)],
        out_specs=tuple(out_specs),
        out_shape=tuple(out_shapes),
        scratch_shapes=[pltpu.VMEM((d_model // LANES, tm, LANES), F32)],
        compiler_params=_params(2),
        name=name,
    )(x, gain, w, q_gain_row, k_gain_row, bd)
    return [tuple(outs[3 * i:3 * i + 3]) for i in range(len(dilations))]


def _dsw_attn_kernel(q_ref, kc_ref, vc_ref, kp_ref, vp_ref, bias_ref, o_ref, lse_ref, *,
                     tq, blocks_per_seq):
    t = pl.program_id(0)
    n_blk = tq // DSW_BLK
    lane = lax.broadcasted_iota(jnp.int32, (DSW_BLK, LANES), 1)
    low_half = lane < HEAD_DIM
    n_pairs = q_ref.shape[1] // LANES
    for qi in range(n_blk):
        rows = slice(qi * DSW_BLK, (qi + 1) * DSW_BLK)
        first = (jnp.bitwise_and(t * n_blk + qi, blocks_per_seq - 1) == 0).astype(jnp.int32)
        lse_tile = jnp.zeros((DSW_BLK, LANES), F32)
        for hp in range(n_pairs):
            cols = slice(hp * LANES, (hp + 1) * LANES)
            q2 = q_ref[rows, cols]
            if qi == 0:
                k_prev, v_prev = kp_ref[:, cols], vp_ref[:, cols]
            else:
                prev = slice((qi - 1) * DSW_BLK, qi * DSW_BLK)
                k_prev, v_prev = kc_ref[prev, cols], vc_ref[prev, cols]
            k_cat = jnp.concatenate([k_prev, kc_ref[rows, cols]], axis=0)
            v_cat = jnp.concatenate([v_prev, vc_ref[rows, cols]], axis=0)
            outs = []
            for hh in range(2):
                head_lanes = low_half if hh == 0 else jnp.logical_not(low_half)
                qm = jnp.where(head_lanes, q2, jnp.zeros_like(q2))
                s = lax.dot_general(qm, k_cat, NT_DIMS, preferred_element_type=F32)
                s = s + bias_ref[0, hp * 2 + hh, first]
                m = jnp.max(s, axis=-1, keepdims=True)
                p = jnp.exp(s - m)
                den = jnp.sum(p, axis=-1, keepdims=True)
                pv = jnp.dot(p.astype(BF16), v_cat, preferred_element_type=F32)
                outs.append(pv / den)
                lse = m + jnp.log(den)
                lse_tile = jnp.where(lane == hp * 2 + hh, lse, lse_tile)
            o_ref[rows, cols] = jnp.where(low_half, outs[0], outs[1])
        lse_ref[rows, :] = lse_tile


def _dsw_attention(q, k, v, bias_tiles, group, tq, blocks_per_seq):
    n_rows, width = q.shape
    assert blocks_per_seq & (blocks_per_seq - 1) == 0
    n_blk = tq // DSW_BLK
    cur = pl.BlockSpec((tq, width), lambda t: (t, 0))
    prev = pl.BlockSpec((DSW_BLK, width), lambda t: (jnp.maximum(t * n_blk - 1, 0), 0))
    hg = DSW_HEADS_PER_GROUP
    return pl.pallas_call(
        functools.partial(_dsw_attn_kernel, tq=tq, blocks_per_seq=blocks_per_seq),
        grid=(n_rows // tq,),
        in_specs=[cur, cur, cur, prev, prev,
                  pl.BlockSpec((1, hg, 2, DSW_BLK, 2 * DSW_BLK), lambda t: (group, 0, 0, 0, 0))],
        out_specs=(pl.BlockSpec((tq, width), lambda t: (t, 0)),
                   pl.BlockSpec((tq, LANES), lambda t: (t, 0))),
        out_shape=(jax.ShapeDtypeStruct((n_rows, width), F32),
                   jax.ShapeDtypeStruct((n_rows, LANES), F32)),
        compiler_params=_params(1),
        name=f"dsw_attention_g{group}",
    )(q, k, v, k, v, bias_tiles)


def _merge_wo_kernel(*refs, dilations):
    n = len(dilations)
    o_refs, l_refs = refs[:n], refs[n:2 * n]
    x_ref, wo_ref, e_ref, out_ref = refs[2 * n:2 * n + 4]
    scratch = refs[2 * n + 4:]
    rows = x_ref.shape[1]
    outs, lses = [], []
    si = 0
    for d, o_ref, l_ref in zip(dilations, o_refs, l_refs):
        if d == 1:
            outs.append(o_ref[0, 0])
            lses.append(l_ref[0, 0])
            continue
        o_s, l_s = scratch[si], scratch[si + 1]
        si += 2
        n_tiles = o_s.shape[0]
        for c in range(d):
            piece = o_ref[0, c]
            for j in range(n_tiles):
                o_s[j, pl.ds(c, rows // d, stride=d), :] = piece[:, j * LANES:(j + 1) * LANES]
            l_s[pl.ds(c, rows // d, stride=d), :] = l_ref[0, c]
        outs.append(jnp.concatenate([o_s[j] for j in range(n_tiles)], axis=1))
        lses.append(l_s[...])
    mx = functools.reduce(jnp.maximum, lses)
    es = [jnp.exp(l - mx) for l in lses]
    den = functools.reduce(lambda a, b: a + b, es)
    merged = jnp.zeros(outs[0].shape, F32)
    for e, o in zip(es, outs):
        wgt = e / den
        hi = wgt.astype(BF16)
        lo = (wgt - hi.astype(F32)).astype(BF16)
        spread = (jnp.dot(hi, e_ref[...], preferred_element_type=F32)
                  + jnp.dot(lo, e_ref[...], preferred_element_type=F32))
        merged = merged + spread * o
    out_ref[0] = x_ref[0] + jnp.dot(merged.astype(BF16), wo_ref[...], preferred_element_type=F32)


def _merge_wo(os, lses, x, wo, dilations, tm):
    batch, seq, d_model = x.shape
    width = os[0].shape[-1]
    head_of_lane = jnp.arange(width) // HEAD_DIM
    expand = (jnp.arange(LANES)[:, None] == head_of_lane[None, :]).astype(BF16)
    const = lambda b, t: (0, 0)
    blk = lambda d, w: pl.BlockSpec((1, d, tm // d, w), lambda b, t: (b, 0, t, 0))
    x_spec = pl.BlockSpec((1, tm, d_model), lambda b, t: (b, t, 0))
    scratch = []
    for d in dilations:
        if d > 1:
            scratch += [pltpu.VMEM((width // LANES, tm, LANES), F32), pltpu.VMEM((tm, LANES), F32)]
    return pl.pallas_call(
        functools.partial(_merge_wo_kernel, dilations=tuple(dilations)),
        grid=(batch, seq // tm),
        in_specs=([blk(d, width) for d in dilations] + [blk(d, LANES) for d in dilations]
                  + [x_spec, pl.BlockSpec(wo.shape, const), pl.BlockSpec(expand.shape, const)]),
        out_specs=x_spec,
        out_shape=jax.ShapeDtypeStruct(x.shape, F32),
        scratch_shapes=scratch,
        compiler_params=_params(2),
        name="dsw_merge_wo",
    )(*os, *lses, x, wo, expand)


def _ffn_kernel(*refs, has_attn, ff_chunk):
    if has_attn:
        h_ref, a_ref, wo_ref, g_ref, w1_ref, w2_ref, out_ref = refs
        h = h_ref[...] + jnp.dot(a_ref[...], wo_ref[...], preferred_element_type=F32)
    else:
        h_ref, g_ref, w1_ref, w2_ref, out_ref = refs
        h = h_ref[...]
    u = _rmsnorm_bf16(h, g_ref[...])
    acc = h
    for c in range(w1_ref.shape[1] // ff_chunk):
        cols = slice(c * ff_chunk, (c + 1) * ff_chunk)
        a = jnp.dot(u, w1_ref[:, cols], preferred_element_type=F32)
        a = jnp.square(jnp.maximum(a, 0.0)).astype(BF16)
        acc = acc + jnp.dot(a, w2_ref[cols, :], preferred_element_type=F32)
    out_ref[...] = acc


def _ffn(h, gain, w1, w2, tm, attn=None, wo=None, name="ffn"):
    n_rows, d_model = h.shape
    const = lambda i: (0, 0)
    row_spec = pl.BlockSpec((tm, d_model), lambda i: (i, 0))
    in_specs = [row_spec]
    args = [h]
    if attn is not None:
        in_specs += [pl.BlockSpec((tm, attn.shape[1]), lambda i: (i, 0)), pl.BlockSpec(wo.shape, const)]
        args += [attn, wo]
    in_specs += [pl.BlockSpec((1, d_model), const), pl.BlockSpec(w1.shape, const), pl.BlockSpec(w2.shape, const)]
    args += [gain, w1, w2]
    return pl.pallas_call(
        functools.partial(_ffn_kernel, has_attn=attn is not None, ff_chunk=1024),
        grid=(n_rows // tm,),
        in_specs=in_specs,
        out_specs=row_spec,
        out_shape=jax.ShapeDtypeStruct(h.shape, F32),
        compiler_params=_params(1),
        name=name,
    )(*args)


def _moba_kernel(q_ref, k_ref, v_ref, wn_ref, o_ref,
                 bias_s, kp_s, vt_s, qa_s, s_s, *, n_blocks, chunk):
    b = pl.program_id(1)
    blk = MOBA_BLOCK
    seq = n_blocks * blk
    span = chunk * blk
    lane = lax.broadcasted_iota(jnp.int32, (blk, LANES), 1)
    low_lanes = lane < HEAD_DIM
    row_t = lax.broadcasted_iota(jnp.int32, (LANES, blk), 0)

    @pl.when(b == 0)
    def _build_bias():
        for hh in range(2):
            for dlt in range(n_blocks):
                u = jnp.concatenate([wn_ref[0, hh:hh + 1, (dlt + 1) * blk:(dlt + 2) * blk],
                                     wn_ref[0, hh:hh + 1, dlt * blk:(dlt + 1) * blk]], axis=1)
                r = n_blocks - 1 - dlt
                bias_s[hh, r * blk:(r + 1) * blk, :] = _toeplitz(u, blk)[:, :blk] * LOG2E
            bias_s[hh, seq:, :] = jnp.full((span - blk, blk), NEG, F32)

    sub = lax.broadcasted_iota(jnp.int32, (n_blocks, LANES), 0)
    kmean = jnp.zeros((n_blocks, LANES), F32)
    for j in range(n_blocks):
        rows = slice(j * blk, (j + 1) * blk)
        kj = k_ref[0, rows, :].astype(F32)
        vj_t = v_ref[0, rows, :].astype(F32).T
        kmean = jnp.where(sub == j, jnp.mean(kj, axis=0, keepdims=True), kmean)
        kp_s[0, rows, :] = jnp.where(low_lanes, kj, (lane == HEAD_DIM + j).astype(F32)).astype(BF16)
        kp_s[1, rows, :] = jnp.where(low_lanes, (lane == j).astype(F32), kj).astype(BF16)
        vt_s[0, :, rows] = jnp.where(row_t < HEAD_DIM, vj_t, (row_t == HEAD_DIM).astype(F32)).astype(BF16)
        vt_s[1, :, rows] = jnp.where(row_t < HEAD_DIM, (row_t == 0).astype(F32), vj_t).astype(BF16)
    km_hi = kmean.astype(BF16)
    km_lo = (kmean - km_hi.astype(F32)).astype(BF16)

    blk_row = lax.broadcasted_iota(jnp.int32, (n_blocks, blk), 0)
    blk_row_f = blk_row.astype(F32)
    for n in range(n_blocks):
        rows = slice(n * blk, (n + 1) * blk)
        q2 = q_ref[0, rows, :]
        for hh in range(2):
            own = low_lanes if hh == 0 else jnp.logical_not(low_lanes)
            qm = jnp.where(own, q2, jnp.zeros_like(q2))
            gate = (lax.dot_general(km_hi, qm, NT_DIMS, preferred_element_type=F32)
                    + lax.dot_general(km_lo, qm, NT_DIMS, preferred_element_type=F32))
            gate = jnp.where(blk_row < n, gate, -jnp.inf)
            chosen = blk_row == n
            for _ in range(min(MOBA_TOPK, n)):
                best = jnp.max(gate, axis=0, keepdims=True)
                cand = jnp.where(gate == best, blk_row_f, float(n_blocks))
                first = jnp.min(cand, axis=0, keepdims=True)
                pick = blk_row_f == first
                chosen = jnp.logical_or(chosen, pick)
                gate = jnp.where(pick, -jnp.inf, gate)
            pen_t = jnp.where(chosen, 0.0, NEG)
            pad_lo = HEAD_DIM if hh == 0 else 0
            pieces = [jnp.zeros((pad_lo, blk), F32)] if pad_lo else []
            pieces += [pen_t, jnp.zeros((LANES - pad_lo - n_blocks, blk), F32)]
            pen = jnp.concatenate(pieces, axis=0).T
            qa_s[hh, rows, :] = jnp.where(own, q2, pen.astype(BF16))

    def attend_group(g):
        n_keys = (g + 1) * span

        def q_block(t, carry):
            n = g * chunk + t
            q_rows = pl.ds(pl.multiple_of(n * blk, blk), blk)
            bias_row0 = (n_blocks - 1 - n) * blk
            col_max = []
            for hh in range(2):
                q_aug = qa_s[hh, q_rows, :]
                m = None
                for c in range(g + 1):
                    k_rows = slice(c * span, (c + 1) * span)
                    b_rows = pl.ds(pl.multiple_of(bias_row0 + c * span, blk), span)
                    s_t = lax.dot_general(kp_s[hh, k_rows, :], q_aug, NT_DIMS,
                                          preferred_element_type=F32)
                    s_t = s_t + bias_s[hh, b_rows, :]
                    s_s[hh, k_rows, :] = s_t
                    cm = jnp.max(s_t, axis=0, keepdims=True)
                    m = cm if m is None else jnp.maximum(m, cm)
                col_max.append(m)
            accs = []
            for hh in range(2):
                p_t = jnp.exp2(s_s[hh, :n_keys, :] - col_max[hh]).astype(BF16)
                accs.append(jnp.dot(vt_s[hh, :, :n_keys], p_t, preferred_element_type=F32))
            o_t = jnp.concatenate([accs[0][:HEAD_DIM] / accs[0][HEAD_DIM:HEAD_DIM + 1],
                                   accs[1][HEAD_DIM:] / accs[1][0:1]], axis=0)
            o_ref[0, q_rows, :] = o_t.T.astype(o_ref.dtype)
            return carry

        lax.fori_loop(0, chunk, q_block, 0)

    for g in range(n_blocks // chunk):
        attend_group(g)


def _moba_attention(q, k, v, moba_vec, batch, seq, chunk):
    n_rows, width = q.shape
    n_pairs = width // LANES
    n_blocks = seq // MOBA_BLOCK
    assert n_blocks % chunk == 0
    wn = moba_vec.reshape(n_pairs, 2, moba_vec.shape[1])
    spec = pl.BlockSpec((1, seq, LANES), lambda p, b: (b, 0, p))
    shape3 = (batch, seq, width)
    out = pl.pallas_call(
        functools.partial(_moba_kernel, n_blocks=n_blocks, chunk=chunk),
        grid=(n_pairs, batch),
        in_specs=[spec, spec, spec, pl.BlockSpec((1, 2, wn.shape[2]), lambda p, b: (p, 0, 0))],
        out_specs=spec,
        out_shape=jax.ShapeDtypeStruct(shape3, BF16),
        scratch_shapes=[pltpu.VMEM((2, seq + (chunk - 1) * MOBA_BLOCK, MOBA_BLOCK), F32),
                        pltpu.VMEM((2, seq, LANES), BF16),
                        pltpu.VMEM((2, LANES, seq), BF16),
                        pltpu.VMEM((2, seq, LANES), BF16),
                        pltpu.VMEM((2, seq, MOBA_BLOCK), F32)],
        compiler_params=_params(2),
        name="moba_attention",
    )(q.reshape(shape3), k.reshape(shape3), v.reshape(shape3), wn)
    return out.reshape(n_rows, width)


def _head_mean_matrix(size):
    idx = jnp.arange(size) // HEAD_DIM
    return (idx[:, None] == idx[None, :]).astype(BF16) * (1.0 / HEAD_DIM)


def _gain_row(gain, size, scale):
    return (jnp.tile(gain.astype(F32), size // HEAD_DIM) * scale).reshape(1, size)


def kernel(x, rel_bias, norm_mix, norm_ffn, a_w_qkv, a_q_gain, a_k_gain, a_w_o,
           b_w_qkv, b_q_gain, b_k_gain, b_w_o, ffn_w1, ffn_w2):
    batch, seq, d_model = x.shape
    n_rows = batch * seq
    n_groups = len(DSW_GROUPS)
    dilations = [d for _, d in DSW_GROUPS]
    gw = DSW_HEADS_PER_GROUP * HEAD_DIM
    moba_heads = b_w_o.shape[1] // HEAD_DIM
    bd = _head_mean_matrix(2 * LANES)

    dsw_vec, moba_vec = _bias_tables(rel_bias, moba_heads, seq)
    dsw_tiles = _dsw_bias_tiles(dsw_vec)

    wa = a_w_qkv[0].astype(BF16).reshape(d_model, 3, n_groups, gw)
    wa = wa.transpose(0, 2, 1, 3).reshape(d_model, n_groups * 3 * gw)
    qkv = _qkv_project(x, norm_mix[0].reshape(1, d_model), wa,
                       _gain_row(a_q_gain[0], 2 * LANES, SCALE), _gain_row(a_k_gain[0], 2 * LANES, 1.0),
                       bd, dilations, tm=512, name="dsw_qkv")
    os, lses = [], []
    for g, d in enumerate(dilations):
        q_g, k_g, v_g = (a.reshape(n_rows, gw) for a in qkv[g])
        o_g, lse_g = _dsw_attention(q_g, k_g, v_g, dsw_tiles, g, tq=256,
                                    blocks_per_seq=seq // d // DSW_BLK)
        os.append(o_g.reshape(batch, d, seq // d, gw))
        lses.append(lse_g.reshape(batch, d, seq // d, LANES))
    h = _merge_wo(os, lses, x, a_w_o[0].astype(BF16), dilations, tm=512)
    h = h.reshape(n_rows, d_model)
    h = _ffn(h, norm_ffn[0].reshape(1, d_model), ffn_w1[0].astype(BF16), ffn_w2[0].astype(BF16),
             tm=512, name="ffn0")

    width = moba_heads * HEAD_DIM
    (q, k, v), = _qkv_project(
        h.reshape(batch, seq, d_model), norm_mix[1].reshape(1, d_model), b_w_qkv[0].astype(BF16),
        _gain_row(b_q_gain[0], 2 * LANES, SCALE * LOG2E), _gain_row(b_k_gain[0], 2 * LANES, 1.0),
        bd, [1], tm=512, name="moba_qkv")
    q, k, v = (a.reshape(n_rows, width) for a in (q, k, v))
    attn = _moba_attention(q, k, v, moba_vec, batch, seq, chunk=4)
    h = _ffn(h, norm_ffn[1].reshape(1, d_model), ffn_w1[1].astype(BF16), ffn_w2[1].astype(BF16),
             tm=512, attn=attn, wo=b_w_o[0].astype(BF16), name="wo_ffn1")
    return h.reshape(batch, seq, d_model)
```

```python
import functools
import math

import jax
import jax.numpy as jnp
from jax import lax
from jax.experimental import pallas as pl
from jax.experimental.pallas import tpu as pltpu

HEAD_DIM = 64
LANES = 128
DSW_GROUPS = ((128, 1), (512, 4), (2048, 16))
DSW_BLK = 128
DSW_HEADS_PER_GROUP = 8
MOBA_BLOCK = 256
MOBA_TOPK = 3
REL_BUCKETS = 32
REL_MAX_DISTANCE = 2048
EPS = 1e-6
NEG = -1e30
SCALE = HEAD_DIM ** -0.5
LOG2E = 1.4426950408889634
VMEM_LIMIT_BYTES = 56 * 1024 * 1024

F32 = jnp.float32
BF16 = jnp.bfloat16
NT_DIMS = (((1,), (1,)), ((), ()))


def _params(n_axes):
    return pltpu.CompilerParams(dimension_semantics=("arbitrary",) * n_axes,
                                vmem_limit_bytes=VMEM_LIMIT_BYTES)


def _aligned(start, multiple):
    return start if isinstance(start, int) else pl.multiple_of(start, multiple)


def _t5_bucket(dist):
    n = jnp.maximum(dist, 0)
    max_exact = REL_BUCKETS // 2
    nf = jnp.maximum(n, 1).astype(F32)
    large = max_exact + (jnp.log(nf / max_exact) / math.log(REL_MAX_DISTANCE / max_exact)
                         * (REL_BUCKETS - max_exact)).astype(jnp.int32)
    large = jnp.minimum(large, REL_BUCKETS - 1)
    return jnp.where(n < max_exact, n, large)


def _lookup(bucket, table_t):
    acc = jnp.zeros(bucket.shape, F32)
    for b in range(REL_BUCKETS):
        acc = jnp.where(bucket == b, table_t[:, b:b + 1], acc)
    return acc


def _bias_tables_kernel(tab_ref, dsw_ref, moba_ref, *, n_moba_heads, seq):
    tab = tab_ref[...]
    n_heads = dsw_ref.shape[0]
    m = lax.broadcasted_iota(jnp.int32, (n_heads, 2 * DSW_BLK), 1)
    row = lax.broadcasted_iota(jnp.int32, (n_heads, 2 * DSW_BLK), 0)
    dil = jnp.where(row < DSW_HEADS_PER_GROUP, DSW_GROUPS[0][1],
                    jnp.where(row < 2 * DSW_HEADS_PER_GROUP, DSW_GROUPS[1][1], DSW_GROUPS[2][1]))
    sub = DSW_BLK - m
    vals = _lookup(_t5_bucket(sub * dil), tab)
    dsw_ref[...] = jnp.where(sub >= 0, vals, NEG)
    width = moba_ref.shape[1]
    t = lax.broadcasted_iota(jnp.int32, (n_moba_heads, width), 1)
    dist = t - MOBA_BLOCK
    vals = _lookup(_t5_bucket(dist), tab[:n_moba_heads])
    moba_ref[...] = jnp.where(dist >= 0, vals, NEG)


def _bias_tables(rel_bias, n_moba_heads, seq):
    n_heads = rel_bias.shape[1]
    width = seq + MOBA_BLOCK
    return pl.pallas_call(
        functools.partial(_bias_tables_kernel, n_moba_heads=n_moba_heads, seq=seq),
        out_shape=(jax.ShapeDtypeStruct((n_heads, 2 * DSW_BLK), F32),
                   jax.ShapeDtypeStruct((n_moba_heads, width), F32)),
        name="bias_tables",
    )(rel_bias.T)


def _toeplitz(u_row, rows):
    x = jnp.broadcast_to(u_row, (rows, u_row.shape[1]))
    return pltpu.roll(x, 0, 1, stride=1, stride_axis=0)


def _dsw_bias_kernel(vec_ref, out_ref):
    lane = lax.broadcasted_iota(jnp.int32, (DSW_BLK, 2 * DSW_BLK), 1)
    for h in range(DSW_HEADS_PER_GROUP):
        tile = _toeplitz(vec_ref[0, h:h + 1, :], DSW_BLK)
        out_ref[0, h, 0] = tile
        out_ref[0, h, 1] = jnp.where(lane < DSW_BLK, NEG, tile)


def _dsw_bias_tiles(dsw_vec):
    n_groups = len(DSW_GROUPS)
    hg = DSW_HEADS_PER_GROUP
    vec = dsw_vec.reshape(n_groups, hg, 2 * DSW_BLK)
    return pl.pallas_call(
        _dsw_bias_kernel,
        grid=(n_groups,),
        in_specs=[pl.BlockSpec((1, hg, 2 * DSW_BLK), lambda g: (g, 0, 0))],
        out_specs=pl.BlockSpec((1, hg, 2, DSW_BLK, 2 * DSW_BLK), lambda g: (g, 0, 0, 0, 0)),
        out_shape=jax.ShapeDtypeStruct((n_groups, hg, 2, DSW_BLK, 2 * DSW_BLK), F32),
        compiler_params=_params(1),
        name="dsw_bias_tiles",
    )(vec)


def _rmsnorm_bf16(x, gain):
    ms = jnp.mean(x * x, axis=-1, keepdims=True)
    return (x * lax.rsqrt(ms + EPS) * gain).astype(BF16)


def _qkv_kernel(x_ref, g_ref, w_ref, qg_ref, kg_ref, bd_ref, *rest, width, dilations):
    n_groups = len(dilations)
    out_refs = rest[:3 * n_groups]
    u_s = rest[3 * n_groups]
    x = x_ref[0]
    rows = x.shape[0]
    ms = jnp.mean(x * x, axis=-1, keepdims=True)
    u = x * lax.rsqrt(ms + EPS) * g_ref[...]
    n_tiles = x.shape[1] // LANES
    if any(d > 1 for d in dilations):
        for j in range(n_tiles):
            u_s[j] = u[:, j * LANES:(j + 1) * LANES]
    chunk = bd_ref.shape[0]
    for gi, d in enumerate(dilations):
        per = rows // d
        if d == 1:
            lhs = u.astype(BF16)
        else:
            lhs = jnp.concatenate(
                [jnp.concatenate([u_s[j, pl.ds(c, per, stride=d), :] for j in range(n_tiles)], axis=1)
                 for c in range(d)], axis=0).astype(BF16)
        q_ref, k_ref, v_ref = out_refs[3 * gi:3 * gi + 3]
        for part, (gain_ref, out_ref) in enumerate(((qg_ref, q_ref), (kg_ref, k_ref), (None, v_ref))):
            for cc in range(width // chunk):
                col = (gi * 3 + part) * width + cc * chunk
                y = jnp.dot(lhs, w_ref[:, col:col + chunk], preferred_element_type=F32)
                if gain_ref is not None:
                    msq = jnp.dot((y * y).astype(BF16), bd_ref[...], preferred_element_type=F32)
                    y = y * lax.rsqrt(msq + EPS) * gain_ref[...]
                y = y.astype(BF16)
                for c in range(d):
                    out_ref[0, c, :, cc * chunk:(cc + 1) * chunk] = y[c * per:(c + 1) * per]


def _qkv_project(x, gain, w, q_gain_row, k_gain_row, bd, dilations, tm, name):
    batch, seq, d_model = x.shape
    width = w.shape[1] // (3 * len(dilations))
    const = lambda b, t: (0, 0)
    out_specs, out_shapes = [], []
    for d in dilations:
        out_specs += [pl.BlockSpec((1, d, tm // d, width), lambda b, t: (b, 0, t, 0))] * 3
        out_shapes += [jax.ShapeDtypeStruct((batch, d, seq // d, width), BF16)] * 3
    outs = pl.pallas_call(
        functools.partial(_qkv_kernel, width=width, dilations=tuple(dilations)),
        grid=(batch, seq // tm),
        in_specs=[pl.BlockSpec((1, tm, d_model), lambda b, t: (b, t, 0)),
                  pl.BlockSpec((1, d_model), const),
                  pl.BlockSpec(w.shape, const),
                  pl.BlockSpec(q_gain_row.shape, const),
                  pl.BlockSpec(k_gain_row.shape, const),
                  pl.BlockSpec(bd.shape, const)],
        out_specs=tuple(out_specs),
        out_shape=tuple(out_shapes),
        scratch_shapes=[pltpu.VMEM((d_model // LANES, tm, LANES), F32)],
        compiler_params=_params(2),
        name=name,
    )(x, gain, w, q_gain_row, k_gain_row, bd)
    return [tuple(outs[3 * i:3 * i + 3]) for i in range(len(dilations))]


def _dsw_attn_kernel(q_ref, kc_ref, vc_ref, kp_ref, vp_ref, bias_ref, o_ref, lse_ref, *,
                     tq, blocks_per_seq):
    t = pl.program_id(0)
    n_blk = tq // DSW_BLK
    lane = lax.broadcasted_iota(jnp.int32, (DSW_BLK, LANES), 1)
    low_half = lane < HEAD_DIM
    n_pairs = q_ref.shape[1] // LANES
    for qi in range(n_blk):
        rows = slice(qi * DSW_BLK, (qi + 1) * DSW_BLK)
        first = (jnp.bitwise_and(t * n_blk + qi, blocks_per_seq - 1) == 0).astype(jnp.int32)
        lse_tile = jnp.zeros((DSW_BLK, LANES), F32)
        for hp in range(n_pairs):
            cols = slice(hp * LANES, (hp + 1) * LANES)
            q2 = q_ref[rows, cols]
            if qi == 0:
                k_prev, v_prev = kp_ref[:, cols], vp_ref[:, cols]
            else:
                prev = slice((qi - 1) * DSW_BLK, qi * DSW_BLK)
                k_prev, v_prev = kc_ref[prev, cols], vc_ref[prev, cols]
            k_cat = jnp.concatenate([k_prev, kc_ref[rows, cols]], axis=0)
            v_cat = jnp.concatenate([v_prev, vc_ref[rows, cols]], axis=0)
            outs = []
            for hh in range(2):
                head_lanes = low_half if hh == 0 else jnp.logical_not(low_half)
                qm = jnp.where(head_lanes, q2, jnp.zeros_like(q2))
                s = lax.dot_general(qm, k_cat, NT_DIMS, preferred_element_type=F32)
                s = s + bias_ref[0, hp * 2 + hh, first]
                m = jnp.max(s, axis=-1, keepdims=True)
                p = jnp.exp(s - m)
                den = jnp.sum(p, axis=-1, keepdims=True)
                pv = jnp.dot(p.astype(BF16), v_cat, preferred_element_type=F32)
                outs.append(pv / den)
                lse = m + jnp.log(den)
                lse_tile = jnp.where(lane == hp * 2 + hh, lse, lse_tile)
            o_ref[rows, cols] = jnp.where(low_half, outs[0], outs[1])
        lse_ref[rows, :] = lse_tile


def _dsw_attention(q, k, v, bias_tiles, group, tq, blocks_per_seq):
    n_rows, width = q.shape
    assert blocks_per_seq & (blocks_per_seq - 1) == 0
    n_blk = tq // DSW_BLK
    cur = pl.BlockSpec((tq, width), lambda t: (t, 0))
    prev = pl.BlockSpec((DSW_BLK, width), lambda t: (jnp.maximum(t * n_blk - 1, 0), 0))
    hg = DSW_HEADS_PER_GROUP
    return pl.pallas_call(
        functools.partial(_dsw_attn_kernel, tq=tq, blocks_per_seq=blocks_per_seq),
        grid=(n_rows // tq,),
        in_specs=[cur, cur, cur, prev, prev,
                  pl.BlockSpec((1, hg, 2, DSW_BLK, 2 * DSW_BLK), lambda t: (group, 0, 0, 0, 0))],
        out_specs=(pl.BlockSpec((tq, width), lambda t: (t, 0)),
                   pl.BlockSpec((tq, LANES), lambda t: (t, 0))),
        out_shape=(jax.ShapeDtypeStruct((n_rows, width), F32),
                   jax.ShapeDtypeStruct((n_rows, LANES), F32)),
        compiler_params=_params(1),
        name=f"dsw_attention_g{group}",
    )(q, k, v, k, v, bias_tiles)


def _merge_wo_kernel(*refs, dilations):
    n = len(dilations)
    o_refs, l_refs = refs[:n], refs[n:2 * n]
    x_ref, wo_ref, e_ref, out_ref = refs[2 * n:2 * n + 4]
    scratch = refs[2 * n + 4:]
    rows = x_ref.shape[1]
    outs, lses = [], []
    si = 0
    for d, o_ref, l_ref in zip(dilations, o_refs, l_refs):
        if d == 1:
            outs.append(o_ref[0, 0])
            lses.append(l_ref[0, 0])
            continue
        o_s, l_s = scratch[si], scratch[si + 1]
        si += 2
        n_tiles = o_s.shape[0]
        for c in range(d):
            piece = o_ref[0, c]
            for j in range(n_tiles):
                o_s[j, pl.ds(c, rows // d, stride=d), :] = piece[:, j * LANES:(j + 1) * LANES]
            l_s[pl.ds(c, rows // d, stride=d), :] = l_ref[0, c]
        outs.append(jnp.concatenate([o_s[j] for j in range(n_tiles)], axis=1))
        lses.append(l_s[...])
    mx = functools.reduce(jnp.maximum, lses)
    es = [jnp.exp(l - mx) for l in lses]
    den = functools.reduce(lambda a, b: a + b, es)
    merged = jnp.zeros(outs[0].shape, F32)
    for e, o in zip(es, outs):
        wgt = e / den
        hi = wgt.astype(BF16)
        lo = (wgt - hi.astype(F32)).astype(BF16)
        spread = (jnp.dot(hi, e_ref[...], preferred_element_type=F32)
                  + jnp.dot(lo, e_ref[...], preferred_element_type=F32))
        merged = merged + spread * o
    out_ref[0] = x_ref[0] + jnp.dot(merged.astype(BF16), wo_ref[...], preferred_element_type=F32)


def _merge_wo(os, lses, x, wo, dilations, tm):
    batch, seq, d_model = x.shape
    width = os[0].shape[-1]
    head_of_lane = jnp.arange(width) // HEAD_DIM
    expand = (jnp.arange(LANES)[:, None] == head_of_lane[None, :]).astype(BF16)
    const = lambda b, t: (0, 0)
    blk = lambda d, w: pl.BlockSpec((1, d, tm // d, w), lambda b, t: (b, 0, t, 0))
    x_spec = pl.BlockSpec((1, tm, d_model), lambda b, t: (b, t, 0))
    scratch = []
    for d in dilations:
        if d > 1:
            scratch += [pltpu.VMEM((width // LANES, tm, LANES), F32), pltpu.VMEM((tm, LANES), F32)]
    return pl.pallas_call(
        functools.partial(_merge_wo_kernel, dilations=tuple(dilations)),
        grid=(batch, seq // tm),
        in_specs=([blk(d, width) for d in dilations] + [blk(d, LANES) for d in dilations]
                  + [x_spec, pl.BlockSpec(wo.shape, const), pl.BlockSpec(expand.shape, const)]),
        out_specs=x_spec,
        out_shape=jax.ShapeDtypeStruct(x.shape, F32),
        scratch_shapes=scratch,
        compiler_params=_params(2),
        name="dsw_merge_wo",
    )(*os, *lses, x, wo, expand)


def _ffn_kernel(*refs, has_attn, ff_chunk):
    if has_attn:
        h_ref, a_ref, wo_ref, g_ref, w1_ref, w2_ref, out_ref = refs
        h = h_ref[...] + jnp.dot(a_ref[...], wo_ref[...], preferred_element_type=F32)
    else:
        h_ref, g_ref, w1_ref, w2_ref, out_ref = refs
        h = h_ref[...]
    u = _rmsnorm_bf16(h, g_ref[...])
    acc = h
    for c in range(w1_ref.shape[1] // ff_chunk):
        cols = slice(c * ff_chunk, (c + 1) * ff_chunk)
        a = jnp.dot(u, w1_ref[:, cols], preferred_element_type=F32)
        a = jnp.square(jnp.maximum(a, 0.0)).astype(BF16)
        acc = acc + jnp.dot(a, w2_ref[cols, :], preferred_element_type=F32)
    out_ref[...] = acc


def _ffn(h, gain, w1, w2, tm, attn=None, wo=None, name="ffn"):
    n_rows, d_model = h.shape
    const = lambda i: (0, 0)
    row_spec = pl.BlockSpec((tm, d_model), lambda i: (i, 0))
    in_specs = [row_spec]
    args = [h]
    if attn is not None:
        in_specs += [pl.BlockSpec((tm, attn.shape[1]), lambda i: (i, 0)), pl.BlockSpec(wo.shape, const)]
        args += [attn, wo]
    in_specs += [pl.BlockSpec((1, d_model), const), pl.BlockSpec(w1.shape, const), pl.BlockSpec(w2.shape, const)]
    args += [gain, w1, w2]
    return pl.pallas_call(
        functools.partial(_ffn_kernel, has_attn=attn is not None, ff_chunk=1024),
        grid=(n_rows // tm,),
        in_specs=in_specs,
        out_specs=row_spec,
        out_shape=jax.ShapeDtypeStruct(h.shape, F32),
        compiler_params=_params(1),
        name=name,
    )(*args)


def _moba_kernel(q_ref, k_ref, v_ref, wn_ref, o_ref,
                 bias_s, kp_s, vt_s, qa_s, s_s, *, n_blocks, chunk):
    b = pl.program_id(1)
    blk = MOBA_BLOCK
    seq = n_blocks * blk
    span = chunk * blk
    lane = lax.broadcasted_iota(jnp.int32, (blk, LANES), 1)
    low_lanes = lane < HEAD_DIM
    v_rows = vt_s.shape[1]
    ones_row = (lax.broadcasted_iota(jnp.int32, (v_rows - HEAD_DIM, blk), 0) == 0).astype(F32)

    @pl.when(b == 0)
    def _build_bias():
        for hh in range(2):
            for dlt in range(n_blocks):
                u = jnp.concatenate([wn_ref[0, hh:hh + 1, (dlt + 1) * blk:(dlt + 2) * blk],
                                     wn_ref[0, hh:hh + 1, dlt * blk:(dlt + 1) * blk]], axis=1)
                r = n_blocks - 1 - dlt
                bias_s[hh, r * blk:(r + 1) * blk, :] = _toeplitz(u, blk)[:, :blk] * LOG2E
            bias_s[hh, seq:, :] = jnp.full((span - blk, blk), NEG, F32)

    sub = lax.broadcasted_iota(jnp.int32, (n_blocks, LANES), 0)
    kmean = jnp.zeros((n_blocks, LANES), F32)
    for j in range(n_blocks):
        rows = slice(j * blk, (j + 1) * blk)
        kj = k_ref[0, rows, :].astype(F32)
        vj_t = v_ref[0, rows, :].astype(F32).T
        kmean = jnp.where(sub == j, jnp.mean(kj, axis=0, keepdims=True), kmean)
        kp_s[0, rows, :] = jnp.where(low_lanes, kj, (lane == HEAD_DIM + j).astype(F32)).astype(BF16)
        kp_s[1, rows, :] = jnp.where(low_lanes, (lane == j).astype(F32), kj).astype(BF16)
        for hh in range(2):
            vt_s[hh, :, rows] = jnp.concatenate(
                [vj_t[hh * HEAD_DIM:(hh + 1) * HEAD_DIM], ones_row], axis=0).astype(BF16)
    km_hi = kmean.astype(BF16)
    km_lo = (kmean - km_hi.astype(F32)).astype(BF16)

    blk_row = lax.broadcasted_iota(jnp.int32, (n_blocks, blk), 0)
    blk_row_f = blk_row.astype(F32)
    for n in range(n_blocks):
        rows = slice(n * blk, (n + 1) * blk)
        q2 = q_ref[0, rows, :]
        for hh in range(2):
            own = low_lanes if hh == 0 else jnp.logical_not(low_lanes)
            qm = jnp.where(own, q2, jnp.zeros_like(q2))
            gate = (lax.dot_general(km_hi, qm, NT_DIMS, preferred_element_type=F32)
                    + lax.dot_general(km_lo, qm, NT_DIMS, preferred_element_type=F32))
            gate = jnp.where(blk_row < n, gate, -jnp.inf)
            chosen = blk_row == n
            for _ in range(min(MOBA_TOPK, n)):
                best = jnp.max(gate, axis=0, keepdims=True)
                cand = jnp.where(gate == best, blk_row_f, float(n_blocks))
                first = jnp.min(cand, axis=0, keepdims=True)
                pick = blk_row_f == first
                chosen = jnp.logical_or(chosen, pick)
                gate = jnp.where(pick, -jnp.inf, gate)
            pen_t = jnp.where(chosen, 0.0, NEG)
            pad_lo = HEAD_DIM if hh == 0 else 0
            pieces = [jnp.zeros((pad_lo, blk), F32)] if pad_lo else []
            pieces += [pen_t, jnp.zeros((LANES - pad_lo - n_blocks, blk), F32)]
            pen = jnp.concatenate(pieces, axis=0).T
            qa_s[hh, rows, :] = jnp.where(own, q2, pen.astype(BF16))

    def attend_group(g):
        n_keys = (g + 1) * span

        def scores(hh, n):
            q_aug = qa_s[hh, pl.ds(_aligned(n * blk, blk), blk), :]
            bias_row0 = (n_blocks - 1 - n) * blk
            m = None
            for c in range(g + 1):
                k_rows = slice(c * span, (c + 1) * span)
                b_rows = pl.ds(_aligned(bias_row0 + c * span, blk), span)
                s_t = lax.dot_general(kp_s[hh, k_rows, :], q_aug, NT_DIMS,
                                      preferred_element_type=F32)
                s_t = s_t + bias_s[hh, b_rows, :]
                s_s[hh, k_rows, :] = s_t
                cm = jnp.max(s_t, axis=0, keepdims=True)
                m = cm if m is None else jnp.maximum(m, cm)
            return m

        def weighted_values(hh, m):
            p_t = jnp.exp2(s_s[hh, :n_keys, :] - m).astype(BF16)
            acc = jnp.dot(vt_s[hh, :, :n_keys], p_t, preferred_element_type=F32)
            return acc[:HEAD_DIM] / acc[HEAD_DIM:HEAD_DIM + 1]

        def store(n, o0_t, o1_t):
            o_t = jnp.concatenate([o0_t, o1_t], axis=0)
            o_ref[0, pl.ds(_aligned(n * blk, blk), blk), :] = o_t.T.astype(o_ref.dtype)

        def q_block(t, m0):
            n = g * chunk + t
            o0_t = weighted_values(0, m0)
            m1 = scores(1, n)
            m0_next = scores(0, n + 1)
            store(n, o0_t, weighted_values(1, m1))
            return m0_next

        m0 = lax.fori_loop(0, chunk - 1, q_block, scores(0, g * chunk))
        n_last = g * chunk + chunk - 1
        o0_t = weighted_values(0, m0)
        store(n_last, o0_t, weighted_values(1, scores(1, n_last)))

    for g in range(n_blocks // chunk):
        attend_group(g)


def _moba_attention(q, k, v, moba_vec, batch, seq, chunk):
    n_rows, width = q.shape
    n_pairs = width // LANES
    n_blocks = seq // MOBA_BLOCK
    assert n_blocks % chunk == 0
    wn = moba_vec.reshape(n_pairs, 2, moba_vec.shape[1])
    spec = pl.BlockSpec((1, seq, LANES), lambda p, b: (b, 0, p))
    shape3 = (batch, seq, width)
    out = pl.pallas_call(
        functools.partial(_moba_kernel, n_blocks=n_blocks, chunk=chunk),
        grid=(n_pairs, batch),
        in_specs=[spec, spec, spec, pl.BlockSpec((1, 2, wn.shape[2]), lambda p, b: (p, 0, 0))],
        out_specs=spec,
        out_shape=jax.ShapeDtypeStruct(shape3, BF16),
        scratch_shapes=[pltpu.VMEM((2, seq + (chunk - 1) * MOBA_BLOCK, MOBA_BLOCK), F32),
                        pltpu.VMEM((2, seq, LANES), BF16),
                        pltpu.VMEM((2, HEAD_DIM + 16, seq), BF16),
                        pltpu.VMEM((2, seq, LANES), BF16),
                        pltpu.VMEM((2, seq, MOBA_BLOCK), F32)],
        compiler_params=_params(2),
        name="moba_attention",
    )(q.reshape(shape3), k.reshape(shape3), v.reshape(shape3), wn)
    return out.reshape(n_rows, width)


def _head_mean_matrix(size):
    idx = jnp.arange(size) // HEAD_DIM
    return (idx[:, None] == idx[None, :]).astype(BF16) * (1.0 / HEAD_DIM)


def _gain_row(gain, size, scale):
    return (jnp.tile(gain.astype(F32), size // HEAD_DIM) * scale).reshape(1, size)


def kernel(x, rel_bias, norm_mix, norm_ffn, a_w_qkv, a_q_gain, a_k_gain, a_w_o,
           b_w_qkv, b_q_gain, b_k_gain, b_w_o, ffn_w1, ffn_w2):
    batch, seq, d_model = x.shape
    n_rows = batch * seq
    n_groups = len(DSW_GROUPS)
    dilations = [d for _, d in DSW_GROUPS]
    gw = DSW_HEADS_PER_GROUP * HEAD_DIM
    moba_heads = b_w_o.shape[1] // HEAD_DIM
    bd = _head_mean_matrix(2 * LANES)

    dsw_vec, moba_vec = _bias_tables(rel_bias, moba_heads, seq)
    dsw_tiles = _dsw_bias_tiles(dsw_vec)

    wa = a_w_qkv[0].astype(BF16).reshape(d_model, 3, n_groups, gw)
    wa = wa.transpose(0, 2, 1, 3).reshape(d_model, n_groups * 3 * gw)
    qkv = _qkv_project(x, norm_mix[0].reshape(1, d_model), wa,
                       _gain_row(a_q_gain[0], 2 * LANES, SCALE), _gain_row(a_k_gain[0], 2 * LANES, 1.0),
                       bd, dilations, tm=512, name="dsw_qkv")
    os, lses = [], []
    for g, d in enumerate(dilations):
        q_g, k_g, v_g = (a.reshape(n_rows, gw) for a in qkv[g])
        o_g, lse_g = _dsw_attention(q_g, k_g, v_g, dsw_tiles, g, tq=256,
                                    blocks_per_seq=seq // d // DSW_BLK)
        os.append(o_g.reshape(batch, d, seq // d, gw))
        lses.append(lse_g.reshape(batch, d, seq // d, LANES))
    h = _merge_wo(os, lses, x, a_w_o[0].astype(BF16), dilations, tm=512)
    h = h.reshape(n_rows, d_model)
    h = _ffn(h, norm_ffn[0].reshape(1, d_model), ffn_w1[0].astype(BF16), ffn_w2[0].astype(BF16),
             tm=512, name="ffn0")

    width = moba_heads * HEAD_DIM
    (q, k, v), = _qkv_project(
        h.reshape(batch, seq, d_model), norm_mix[1].reshape(1, d_model), b_w_qkv[0].astype(BF16),
        _gain_row(b_q_gain[0], 2 * LANES, SCALE * LOG2E), _gain_row(b_k_gain[0], 2 * LANES, 1.0),
        bd, [1], tm=512, name="moba_qkv")
    q, k, v = (a.reshape(n_rows, width) for a in (q, k, v))
    attn = _moba_attention(q, k, v, moba_vec, batch, seq, chunk=4)
    h = _ffn(h, norm_ffn[1].reshape(1, d_model), ffn_w1[1].astype(BF16), ffn_w2[1].astype(BF16),
             tm=512, attn=attn, wo=b_w_o[0].astype(BF16), name="wo_ffn1")
    return h.reshape(batch, seq, d_model)
```

```python
import functools
import math

import jax
import jax.numpy as jnp
from jax import lax
from jax.experimental import pallas as pl
from jax.experimental.pallas import tpu as pltpu

HEAD_DIM = 64
LANES = 128
DSW_GROUPS = ((128, 1), (512, 4), (2048, 16))
DSW_BLK = 128
DSW_HEADS_PER_GROUP = 8
MOBA_BLOCK = 256
MOBA_TOPK = 3
REL_BUCKETS = 32
REL_MAX_DISTANCE = 2048
EPS = 1e-6
NEG = -1e30
SCALE = HEAD_DIM ** -0.5
LOG2E = 1.4426950408889634
VMEM_LIMIT_BYTES = 56 * 1024 * 1024

F32 = jnp.float32
BF16 = jnp.bfloat16
NT_DIMS = (((1,), (1,)), ((), ()))


def _params(n_axes):
    return pltpu.CompilerParams(dimension_semantics=("arbitrary",) * n_axes,
                                vmem_limit_bytes=VMEM_LIMIT_BYTES)


def _aligned(start, multiple):
    return start if isinstance(start, int) else pl.multiple_of(start, multiple)


def _t5_bucket(dist):
    n = jnp.maximum(dist, 0)
    max_exact = REL_BUCKETS // 2
    nf = jnp.maximum(n, 1).astype(F32)
    large = max_exact + (jnp.log(nf / max_exact) / math.log(REL_MAX_DISTANCE / max_exact)
                         * (REL_BUCKETS - max_exact)).astype(jnp.int32)
    large = jnp.minimum(large, REL_BUCKETS - 1)
    return jnp.where(n < max_exact, n, large)


def _lookup(bucket, table_t):
    acc = jnp.zeros(bucket.shape, F32)
    for b in range(REL_BUCKETS):
        acc = jnp.where(bucket == b, table_t[:, b:b + 1], acc)
    return acc


def _bias_tables_kernel(tab_ref, dsw_ref, moba_ref, *, n_moba_heads, seq):
    tab = tab_ref[...]
    n_heads = dsw_ref.shape[0]
    m = lax.broadcasted_iota(jnp.int32, (n_heads, 2 * DSW_BLK), 1)
    row = lax.broadcasted_iota(jnp.int32, (n_heads, 2 * DSW_BLK), 0)
    dil = jnp.where(row < DSW_HEADS_PER_GROUP, DSW_GROUPS[0][1],
                    jnp.where(row < 2 * DSW_HEADS_PER_GROUP, DSW_GROUPS[1][1], DSW_GROUPS[2][1]))
    sub = DSW_BLK - m
    vals = _lookup(_t5_bucket(sub * dil), tab)
    dsw_ref[...] = jnp.where(sub >= 0, vals * LOG2E, NEG)
    width = moba_ref.shape[1]
    t = lax.broadcasted_iota(jnp.int32, (n_moba_heads, width), 1)
    dist = t - MOBA_BLOCK
    vals = _lookup(_t5_bucket(dist), tab[:n_moba_heads])
    moba_ref[...] = jnp.where(dist >= 0, vals, NEG)


def _bias_tables(rel_bias, n_moba_heads, seq):
    n_heads = rel_bias.shape[1]
    width = seq + MOBA_BLOCK
    return pl.pallas_call(
        functools.partial(_bias_tables_kernel, n_moba_heads=n_moba_heads, seq=seq),
        out_shape=(jax.ShapeDtypeStruct((n_heads, 2 * DSW_BLK), F32),
                   jax.ShapeDtypeStruct((n_moba_heads, width), F32)),
        name="bias_tables",
    )(rel_bias.T)


def _toeplitz(u_row, rows):
    x = jnp.broadcast_to(u_row, (rows, u_row.shape[1]))
    return pltpu.roll(x, 0, 1, stride=1, stride_axis=0)


def _dsw_bias_kernel(vec_ref, out_ref):
    lane = lax.broadcasted_iota(jnp.int32, (DSW_BLK, 2 * DSW_BLK), 1)
    for h in range(DSW_HEADS_PER_GROUP):
        tile = _toeplitz(vec_ref[0, h:h + 1, :], DSW_BLK)
        out_ref[0, h, 0] = tile
        out_ref[0, h, 1] = jnp.where(lane < DSW_BLK, NEG, tile)


def _dsw_bias_tiles(dsw_vec):
    n_groups = len(DSW_GROUPS)
    hg = DSW_HEADS_PER_GROUP
    vec = dsw_vec.reshape(n_groups, hg, 2 * DSW_BLK)
    return pl.pallas_call(
        _dsw_bias_kernel,
        grid=(n_groups,),
        in_specs=[pl.BlockSpec((1, hg, 2 * DSW_BLK), lambda g: (g, 0, 0))],
        out_specs=pl.BlockSpec((1, hg, 2, DSW_BLK, 2 * DSW_BLK), lambda g: (g, 0, 0, 0, 0)),
        out_shape=jax.ShapeDtypeStruct((n_groups, hg, 2, DSW_BLK, 2 * DSW_BLK), F32),
        compiler_params=_params(1),
        name="dsw_bias_tiles",
    )(vec)


def _rmsnorm_bf16(x, gain):
    ms = jnp.mean(x * x, axis=-1, keepdims=True)
    return (x * lax.rsqrt(ms + EPS) * gain).astype(BF16)


def _qkv_kernel(x_ref, g_ref, w_ref, qg_ref, kg_ref, bd_ref, *rest, width, dilations):
    n_groups = len(dilations)
    out_refs = rest[:3 * n_groups]
    u_s = rest[3 * n_groups]
    x = x_ref[0]
    rows = x.shape[0]
    ms = jnp.mean(x * x, axis=-1, keepdims=True)
    u = x * lax.rsqrt(ms + EPS) * g_ref[...]
    n_tiles = x.shape[1] // LANES
    if any(d > 1 for d in dilations):
        for j in range(n_tiles):
            u_s[j] = u[:, j * LANES:(j + 1) * LANES]
    chunk = bd_ref.shape[0]
    for gi, d in enumerate(dilations):
        per = rows // d
        if d == 1:
            lhs = u.astype(BF16)
        else:
            lhs = jnp.concatenate(
                [jnp.concatenate([u_s[j, pl.ds(c, per, stride=d), :] for j in range(n_tiles)], axis=1)
                 for c in range(d)], axis=0).astype(BF16)
        q_ref, k_ref, v_ref = out_refs[3 * gi:3 * gi + 3]
        for part, (gain_ref, out_ref) in enumerate(((qg_ref, q_ref), (kg_ref, k_ref), (None, v_ref))):
            col = (gi * 3 + part) * width
            y_part = jnp.dot(lhs, w_ref[:, col:col + width], preferred_element_type=F32)
            for cc in range(width // chunk):
                y = y_part[:, cc * chunk:(cc + 1) * chunk]
                if gain_ref is not None:
                    msq = jnp.dot((y * y).astype(BF16), bd_ref[...], preferred_element_type=F32)
                    y = y * lax.rsqrt(msq + EPS) * gain_ref[...]
                y = y.astype(BF16)
                for c in range(d):
                    out_ref[0, c, :, cc * chunk:(cc + 1) * chunk] = y[c * per:(c + 1) * per]


def _qkv_project(x, gain, w, q_gain_row, k_gain_row, bd, dilations, tm, name):
    batch, seq, d_model = x.shape
    width = w.shape[1] // (3 * len(dilations))
    const = lambda b, t: (0, 0)
    out_specs, out_shapes = [], []
    for d in dilations:
        out_specs += [pl.BlockSpec((1, d, tm // d, width), lambda b, t: (b, 0, t, 0))] * 3
        out_shapes += [jax.ShapeDtypeStruct((batch, d, seq // d, width), BF16)] * 3
    outs = pl.pallas_call(
        functools.partial(_qkv_kernel, width=width, dilations=tuple(dilations)),
        grid=(batch, seq // tm),
        in_specs=[pl.BlockSpec((1, tm, d_model), lambda b, t: (b, t, 0)),
                  pl.BlockSpec((1, d_model), const),
                  pl.BlockSpec(w.shape, const),
                  pl.BlockSpec(q_gain_row.shape, const),
                  pl.BlockSpec(k_gain_row.shape, const),
                  pl.BlockSpec(bd.shape, const)],
        out_specs=tuple(out_specs),
        out_shape=tuple(out_shapes),
        scratch_shapes=[pltpu.VMEM((d_model // LANES, tm, LANES), F32)],
        compiler_params=_params(2),
        name=name,
    )(x, gain, w, q_gain_row, k_gain_row, bd)
    return [tuple(outs[3 * i:3 * i + 3]) for i in range(len(dilations))]


def _dsw_attn_kernel(q_ref, kc_ref, vc_ref, kp_ref, vp_ref, bias_ref, o_ref, stat_ref, *,
                     tq, blocks_per_seq):
    t = pl.program_id(0)
    n_blk = tq // DSW_BLK
    lane = lax.broadcasted_iota(jnp.int32, (DSW_BLK, LANES), 1)
    low_half = lane < HEAD_DIM
    n_pairs = q_ref.shape[1] // LANES
    for qi in range(n_blk):
        rows = slice(qi * DSW_BLK, (qi + 1) * DSW_BLK)
        first = (jnp.bitwise_and(t * n_blk + qi, blocks_per_seq - 1) == 0).astype(jnp.int32)
        stat_tile = jnp.zeros((DSW_BLK, LANES), F32)
        for hp in range(n_pairs):
            cols = slice(hp * LANES, (hp + 1) * LANES)
            q2 = q_ref[rows, cols]
            if qi == 0:
                k_prev, v_prev = kp_ref[:, cols], vp_ref[:, cols]
            else:
                prev = slice((qi - 1) * DSW_BLK, qi * DSW_BLK)
                k_prev, v_prev = kc_ref[prev, cols], vc_ref[prev, cols]
            k_cat = jnp.concatenate([k_prev, kc_ref[rows, cols]], axis=0)
            v_cat = jnp.concatenate([v_prev, vc_ref[rows, cols]], axis=0)
            outs = []
            for hh in range(2):
                head_lanes = low_half if hh == 0 else jnp.logical_not(low_half)
                qm = jnp.where(head_lanes, q2, jnp.zeros_like(q2))
                s = lax.dot_general(qm, k_cat, NT_DIMS, preferred_element_type=F32)
                s = s + bias_ref[0, hp * 2 + hh, first]
                m = jnp.max(s, axis=-1, keepdims=True)
                p = jnp.exp2(s - m)
                den = jnp.sum(p, axis=-1, keepdims=True)
                outs.append(jnp.dot(p.astype(BF16), v_cat, preferred_element_type=F32))
                head = hp * 2 + hh
                stat_tile = jnp.where(lane == head, m,
                                      jnp.where(lane == DSW_HEADS_PER_GROUP + head, den, stat_tile))
            o_ref[rows, cols] = jnp.where(low_half, outs[0], outs[1])
        stat_ref[rows, :] = stat_tile


def _dsw_attention(q, k, v, bias_tiles, group, tq, blocks_per_seq):
    n_rows, width = q.shape
    assert blocks_per_seq & (blocks_per_seq - 1) == 0
    n_blk = tq // DSW_BLK
    cur = pl.BlockSpec((tq, width), lambda t: (t, 0))
    prev = pl.BlockSpec((DSW_BLK, width), lambda t: (jnp.maximum(t * n_blk - 1, 0), 0))
    hg = DSW_HEADS_PER_GROUP
    return pl.pallas_call(
        functools.partial(_dsw_attn_kernel, tq=tq, blocks_per_seq=blocks_per_seq),
        grid=(n_rows // tq,),
        in_specs=[cur, cur, cur, prev, prev,
                  pl.BlockSpec((1, hg, 2, DSW_BLK, 2 * DSW_BLK), lambda t: (group, 0, 0, 0, 0))],
        out_specs=(pl.BlockSpec((tq, width), lambda t: (t, 0)),
                   pl.BlockSpec((tq, LANES), lambda t: (t, 0))),
        out_shape=(jax.ShapeDtypeStruct((n_rows, width), F32),
                   jax.ShapeDtypeStruct((n_rows, LANES), F32)),
        compiler_params=_params(1),
        name=f"dsw_attention_g{group}",
    )(q, k, v, k, v, bias_tiles)


def _merge_wo_kernel(*refs, dilations):
    n = len(dilations)
    o_refs, l_refs = refs[:n], refs[n:2 * n]
    x_ref, wo_ref, e_ref, out_ref = refs[2 * n:2 * n + 4]
    scratch = refs[2 * n + 4:]
    rows = x_ref.shape[1]
    outs, lses = [], []
    si = 0
    for d, o_ref, l_ref in zip(dilations, o_refs, l_refs):
        if d == 1:
            outs.append(o_ref[0, 0])
            lses.append(l_ref[0, 0])
            continue
        o_s, l_s = scratch[si], scratch[si + 1]
        si += 2
        n_tiles = o_s.shape[0]
        for c in range(d):
            piece = o_ref[0, c]
            for j in range(n_tiles):
                o_s[j, pl.ds(c, rows // d, stride=d), :] = piece[:, j * LANES:(j + 1) * LANES]
            l_s[pl.ds(c, rows // d, stride=d), :] = l_ref[0, c]
        outs.append(jnp.concatenate([o_s[j] for j in range(n_tiles)], axis=1))
        lses.append(l_s[...])
    lane = lax.broadcasted_iota(jnp.int32, lses[0].shape, 1)
    mx = functools.reduce(jnp.maximum, lses)
    es = [jnp.exp2(l - mx) for l in lses]
    sums = [pltpu.roll(l, LANES - DSW_HEADS_PER_GROUP, 1) for l in lses]
    total = functools.reduce(lambda a, b: a + b, [e * s for e, s in zip(es, sums)])
    merged = jnp.zeros(outs[0].shape, F32)
    for e, o in zip(es, outs):
        wgt = jnp.where(lane < DSW_HEADS_PER_GROUP, e / total, 0.0)
        hi = wgt.astype(BF16)
        lo = (wgt - hi.astype(F32)).astype(BF16)
        spread = (jnp.dot(hi, e_ref[...], preferred_element_type=F32)
                  + jnp.dot(lo, e_ref[...], preferred_element_type=F32))
        merged = merged + spread * o
    out_ref[0] = x_ref[0] + jnp.dot(merged.astype(BF16), wo_ref[...], preferred_element_type=F32)


def _merge_wo(os, lses, x, wo, dilations, tm):
    batch, seq, d_model = x.shape
    width = os[0].shape[-1]
    head_of_lane = jnp.arange(width) // HEAD_DIM
    expand = (jnp.arange(LANES)[:, None] == head_of_lane[None, :]).astype(BF16)
    const = lambda b, t: (0, 0)
    blk = lambda d, w: pl.BlockSpec((1, d, tm // d, w), lambda b, t: (b, 0, t, 0))
    x_spec = pl.BlockSpec((1, tm, d_model), lambda b, t: (b, t, 0))
    scratch = []
    for d in dilations:
        if d > 1:
            scratch += [pltpu.VMEM((width // LANES, tm, LANES), F32), pltpu.VMEM((tm, LANES), F32)]
    return pl.pallas_call(
        functools.partial(_merge_wo_kernel, dilations=tuple(dilations)),
        grid=(batch, seq // tm),
        in_specs=([blk(d, width) for d in dilations] + [blk(d, LANES) for d in dilations]
                  + [x_spec, pl.BlockSpec(wo.shape, const), pl.BlockSpec(expand.shape, const)]),
        out_specs=x_spec,
        out_shape=jax.ShapeDtypeStruct(x.shape, F32),
        scratch_shapes=scratch,
        compiler_params=_params(2),
        name="dsw_merge_wo",
    )(*os, *lses, x, wo, expand)


def _ffn_kernel(*refs, has_attn, ff_chunk):
    if has_attn:
        h_ref, a_ref, wo_ref, g_ref, w1_ref, w2_ref, out_ref = refs
        h = h_ref[...] + jnp.dot(a_ref[...], wo_ref[...], preferred_element_type=F32)
    else:
        h_ref, g_ref, w1_ref, w2_ref, out_ref = refs
        h = h_ref[...]
    u = _rmsnorm_bf16(h, g_ref[...])
    acc = h
    for c in range(w1_ref.shape[1] // ff_chunk):
        cols = slice(c * ff_chunk, (c + 1) * ff_chunk)
        a = jnp.dot(u, w1_ref[:, cols], preferred_element_type=F32)
        a = jnp.square(jnp.maximum(a, 0.0)).astype(BF16)
        acc = acc + jnp.dot(a, w2_ref[cols, :], preferred_element_type=F32)
    out_ref[...] = acc


def _ffn(h, gain, w1, w2, tm, attn=None, wo=None, name="ffn"):
    n_rows, d_model = h.shape
    const = lambda i: (0, 0)
    row_spec = pl.BlockSpec((tm, d_model), lambda i: (i, 0))
    in_specs = [row_spec]
    args = [h]
    if attn is not None:
        in_specs += [pl.BlockSpec((tm, attn.shape[1]), lambda i: (i, 0)), pl.BlockSpec(wo.shape, const)]
        args += [attn, wo]
    in_specs += [pl.BlockSpec((1, d_model), const), pl.BlockSpec(w1.shape, const), pl.BlockSpec(w2.shape, const)]
    args += [gain, w1, w2]
    return pl.pallas_call(
        functools.partial(_ffn_kernel, has_attn=attn is not None, ff_chunk=1024),
        grid=(n_rows // tm,),
        in_specs=in_specs,
        out_specs=row_spec,
        out_shape=jax.ShapeDtypeStruct(h.shape, F32),
        compiler_params=_params(1),
        name=name,
    )(*args)


def _moba_kernel(q_ref, k_ref, v_ref, wn_ref, o_ref,
                 bias_s, kp_s, vt_s, qa_s, s_s, *, n_blocks, chunk):
    b = pl.program_id(1)
    blk = MOBA_BLOCK
    seq = n_blocks * blk
    span = chunk * blk
    lane = lax.broadcasted_iota(jnp.int32, (blk, LANES), 1)
    low_lanes = lane < HEAD_DIM
    v_rows = vt_s.shape[1]
    ones_row = (lax.broadcasted_iota(jnp.int32, (v_rows - HEAD_DIM, blk), 0) == 0).astype(F32)

    @pl.when(b == 0)
    def _build_bias():
        for hh in range(2):
            for dlt in range(n_blocks):
                u = jnp.concatenate([wn_ref[0, hh:hh + 1, (dlt + 1) * blk:(dlt + 2) * blk],
                                     wn_ref[0, hh:hh + 1, dlt * blk:(dlt + 1) * blk]], axis=1)
                r = n_blocks - 1 - dlt
                bias_s[hh, r * blk:(r + 1) * blk, :] = _toeplitz(u, blk)[:, :blk] * LOG2E
            bias_s[hh, seq:, :] = jnp.full((span - blk, blk), NEG, F32)

    sub = lax.broadcasted_iota(jnp.int32, (n_blocks, LANES), 0)
    kmean = jnp.zeros((n_blocks, LANES), F32)
    for j in range(n_blocks):
        rows = slice(j * blk, (j + 1) * blk)
        kj = k_ref[0, rows, :].astype(F32)
        vj_t = v_ref[0, rows, :].astype(F32).T
        kmean = jnp.where(sub == j, jnp.mean(kj, axis=0, keepdims=True), kmean)
        kp_s[0, rows, :] = jnp.where(low_lanes, kj, (lane == HEAD_DIM + j).astype(F32)).astype(BF16)
        kp_s[1, rows, :] = jnp.where(low_lanes, (lane == j).astype(F32), kj).astype(BF16)
        for hh in range(2):
            vt_s[hh, :, rows] = jnp.concatenate(
                [vj_t[hh * HEAD_DIM:(hh + 1) * HEAD_DIM], ones_row], axis=0).astype(BF16)
    km_hi = kmean.astype(BF16)
    km_lo = (kmean - km_hi.astype(F32)).astype(BF16)

    blk_row = lax.broadcasted_iota(jnp.int32, (n_blocks, blk), 0)
    blk_row_f = blk_row.astype(F32)
    for n in range(n_blocks):
        rows = slice(n * blk, (n + 1) * blk)
        q2 = q_ref[0, rows, :]
        for hh in range(2):
            own = low_lanes if hh == 0 else jnp.logical_not(low_lanes)
            qm = jnp.where(own, q2, jnp.zeros_like(q2))
            gate = (lax.dot_general(km_hi, qm, NT_DIMS, preferred_element_type=F32)
                    + lax.dot_general(km_lo, qm, NT_DIMS, preferred_element_type=F32))
            gate = jnp.where(blk_row < n, gate, -jnp.inf)
            chosen = blk_row == n
            for _ in range(min(MOBA_TOPK, n)):
                best = jnp.max(gate, axis=0, keepdims=True)
                cand = jnp.where(gate == best, blk_row_f, float(n_blocks))
                first = jnp.min(cand, axis=0, keepdims=True)
                pick = blk_row_f == first
                chosen = jnp.logical_or(chosen, pick)
                gate = jnp.where(pick, -jnp.inf, gate)
            pen_t = jnp.where(chosen, 0.0, NEG)
            pad_lo = HEAD_DIM if hh == 0 else 0
            pieces = [jnp.zeros((pad_lo, blk), F32)] if pad_lo else []
            pieces += [pen_t, jnp.zeros((LANES - pad_lo - n_blocks, blk), F32)]
            pen = jnp.concatenate(pieces, axis=0).T
            qa_s[hh, rows, :] = jnp.where(own, q2, pen.astype(BF16))

    def attend_group(g):
        n_keys = (g + 1) * span

        def scores(hh, n):
            q_aug = qa_s[hh, pl.ds(_aligned(n * blk, blk), blk), :]
            bias_row0 = (n_blocks - 1 - n) * blk
            m = None
            for c in range(g + 1):
                k_rows = slice(c * span, (c + 1) * span)
                b_rows = pl.ds(_aligned(bias_row0 + c * span, blk), span)
                s_t = lax.dot_general(kp_s[hh, k_rows, :], q_aug, NT_DIMS,
                                      preferred_element_type=F32)
                s_t = s_t + bias_s[hh, b_rows, :]
                s_s[hh, k_rows, :] = s_t
                cm = jnp.max(s_t, axis=0, keepdims=True)
                m = cm if m is None else jnp.maximum(m, cm)
            return m

        def weighted_values(hh, m):
            p_t = jnp.exp2(s_s[hh, :n_keys, :] - m).astype(BF16)
            acc = jnp.dot(vt_s[hh, :, :n_keys], p_t, preferred_element_type=F32)
            return acc[:HEAD_DIM] / acc[HEAD_DIM:HEAD_DIM + 1]

        def store(n, o0_t, o1_t):
            o_t = jnp.concatenate([o0_t, o1_t], axis=0)
            o_ref[0, pl.ds(_aligned(n * blk, blk), blk), :] = o_t.T.astype(o_ref.dtype)

        def q_block(t, m0):
            n = g * chunk + t
            o0_t = weighted_values(0, m0)
            m1 = scores(1, n)
            m0_next = scores(0, n + 1)
            store(n, o0_t, weighted_values(1, m1))
            return m0_next

        m0 = lax.fori_loop(0, chunk - 1, q_block, scores(0, g * chunk))
        n_last = g * chunk + chunk - 1
        o0_t = weighted_values(0, m0)
        store(n_last, o0_t, weighted_values(1, scores(1, n_last)))

    for g in range(n_blocks // chunk):
        attend_group(g)


def _moba_attention(q, k, v, moba_vec, batch, seq, chunk):
    n_rows, width = q.shape
    n_pairs = width // LANES
    n_blocks = seq // MOBA_BLOCK
    assert n_blocks % chunk == 0
    wn = moba_vec.reshape(n_pairs, 2, moba_vec.shape[1])
    spec = pl.BlockSpec((1, seq, LANES), lambda p, b: (b, 0, p))
    shape3 = (batch, seq, width)
    out = pl.pallas_call(
        functools.partial(_moba_kernel, n_blocks=n_blocks, chunk=chunk),
        grid=(n_pairs, batch),
        in_specs=[spec, spec, spec, pl.BlockSpec((1, 2, wn.shape[2]), lambda p, b: (p, 0, 0))],
        out_specs=spec,
        out_shape=jax.ShapeDtypeStruct(shape3, BF16),
        scratch_shapes=[pltpu.VMEM((2, seq + (chunk - 1) * MOBA_BLOCK, MOBA_BLOCK), F32),
                        pltpu.VMEM((2, seq, LANES), BF16),
                        pltpu.VMEM((2, HEAD_DIM + 16, seq), BF16),
                        pltpu.VMEM((2, seq, LANES), BF16),
                        pltpu.VMEM((2, seq, MOBA_BLOCK), F32)],
        compiler_params=_params(2),
        name="moba_attention",
    )(q.reshape(shape3), k.reshape(shape3), v.reshape(shape3), wn)
    return out.reshape(n_rows, width)


def _head_mean_matrix(size):
    idx = jnp.arange(size) // HEAD_DIM
    return (idx[:, None] == idx[None, :]).astype(BF16) * (1.0 / HEAD_DIM)


def _gain_row(gain, size, scale):
    return (jnp.tile(gain.astype(F32), size // HEAD_DIM) * scale).reshape(1, size)


def kernel(x, rel_bias, norm_mix, norm_ffn, a_w_qkv, a_q_gain, a_k_gain, a_w_o,
           b_w_qkv, b_q_gain, b_k_gain, b_w_o, ffn_w1, ffn_w2):
    batch, seq, d_model = x.shape
    n_rows = batch * seq
    n_groups = len(DSW_GROUPS)
    dilations = [d for _, d in DSW_GROUPS]
    gw = DSW_HEADS_PER_GROUP * HEAD_DIM
    moba_heads = b_w_o.shape[1] // HEAD_DIM
    bd = _head_mean_matrix(2 * LANES)

    dsw_vec, moba_vec = _bias_tables(rel_bias, moba_heads, seq)
    dsw_tiles = _dsw_bias_tiles(dsw_vec)

    wa = a_w_qkv[0].astype(BF16).reshape(d_model, 3, n_groups, gw)
    wa = wa.transpose(0, 2, 1, 3).reshape(d_model, n_groups * 3 * gw)
    qkv = _qkv_project(x, norm_mix[0].reshape(1, d_model), wa,
                       _gain_row(a_q_gain[0], 2 * LANES, SCALE * LOG2E), _gain_row(a_k_gain[0], 2 * LANES, 1.0),
                       bd, dilations, tm=512, name="dsw_qkv")
    os, lses = [], []
    for g, d in enumerate(dilations):
        q_g, k_g, v_g = (a.reshape(n_rows, gw) for a in qkv[g])
        o_g, lse_g = _dsw_attention(q_g, k_g, v_g, dsw_tiles, g, tq=512,
                                    blocks_per_seq=seq // d // DSW_BLK)
        os.append(o_g.reshape(batch, d, seq // d, gw))
        lses.append(lse_g.reshape(batch, d, seq // d, LANES))
    h = _merge_wo(os, lses, x, a_w_o[0].astype(BF16), dilations, tm=512)
    h = h.reshape(n_rows, d_model)
    h = _ffn(h, norm_ffn[0].reshape(1, d_model), ffn_w1[0].astype(BF16), ffn_w2[0].astype(BF16),
             tm=512, name="ffn0")

    width = moba_heads * HEAD_DIM
    (q, k, v), = _qkv_project(
        h.reshape(batch, seq, d_model), norm_mix[1].reshape(1, d_model), b_w_qkv[0].astype(BF16),
        _gain_row(b_q_gain[0], 2 * LANES, SCALE * LOG2E), _gain_row(b_k_gain[0], 2 * LANES, 1.0),
        bd, [1], tm=512, name="moba_qkv")
    q, k, v = (a.reshape(n_rows, width) for a in (q, k, v))
    attn = _moba_attention(q, k, v, moba_vec, batch, seq, chunk=4)
    h = _ffn(h, norm_ffn[1].reshape(1, d_model), ffn_w1[1].astype(BF16), ffn_w2[1].astype(BF16),
             tm=512, attn=attn, wo=b_w_o[0].astype(BF16), name="wo_ffn1")
    return h.reshape(batch, seq, d_model)
```

```python
import functools
import math

import jax
import jax.numpy as jnp
from jax import lax
from jax.experimental import pallas as pl
from jax.experimental.pallas import tpu as pltpu

HEAD_DIM = 64
LANES = 128
DSW_GROUPS = ((128, 1), (512, 4), (2048, 16))
DSW_BLK = 128
DSW_HEADS_PER_GROUP = 8
MOBA_BLOCK = 256
MOBA_TOPK = 3
REL_BUCKETS = 32
REL_MAX_DISTANCE = 2048
EPS = 1e-6
NEG = -1e30
SCALE = HEAD_DIM ** -0.5
LOG2E = 1.4426950408889634
VMEM_LIMIT_BYTES = 56 * 1024 * 1024

F32 = jnp.float32
BF16 = jnp.bfloat16
NT_DIMS = (((1,), (1,)), ((), ()))


def _params(n_axes):
    return pltpu.CompilerParams(dimension_semantics=("arbitrary",) * n_axes,
                                vmem_limit_bytes=VMEM_LIMIT_BYTES)


def _aligned(start, multiple):
    return start if isinstance(start, int) else pl.multiple_of(start, multiple)


def _t5_bucket(dist):
    n = jnp.maximum(dist, 0)
    max_exact = REL_BUCKETS // 2
    nf = jnp.maximum(n, 1).astype(F32)
    large = max_exact + (jnp.log(nf / max_exact) / math.log(REL_MAX_DISTANCE / max_exact)
                         * (REL_BUCKETS - max_exact)).astype(jnp.int32)
    large = jnp.minimum(large, REL_BUCKETS - 1)
    return jnp.where(n < max_exact, n, large)


def _lookup(bucket, table_t):
    acc = jnp.zeros(bucket.shape, F32)
    for b in range(REL_BUCKETS):
        acc = jnp.where(bucket == b, table_t[:, b:b + 1], acc)
    return acc


def _bias_tables_kernel(tab_ref, dsw_ref, moba_ref, *, n_moba_heads, seq):
    tab = tab_ref[...]
    n_heads = dsw_ref.shape[0]
    m = lax.broadcasted_iota(jnp.int32, (n_heads, 2 * DSW_BLK), 1)
    row = lax.broadcasted_iota(jnp.int32, (n_heads, 2 * DSW_BLK), 0)
    dil = jnp.where(row < DSW_HEADS_PER_GROUP, DSW_GROUPS[0][1],
                    jnp.where(row < 2 * DSW_HEADS_PER_GROUP, DSW_GROUPS[1][1], DSW_GROUPS[2][1]))
    sub = DSW_BLK - m
    vals = _lookup(_t5_bucket(sub * dil), tab)
    dsw_ref[...] = jnp.where(sub >= 0, vals * LOG2E, NEG)
    width = moba_ref.shape[1]
    t = lax.broadcasted_iota(jnp.int32, (n_moba_heads, width), 1)
    dist = t - MOBA_BLOCK
    vals = _lookup(_t5_bucket(dist), tab[:n_moba_heads])
    moba_ref[...] = jnp.where(dist >= 0, vals, NEG)


def _bias_tables(rel_bias, n_moba_heads, seq):
    n_heads = rel_bias.shape[1]
    width = seq + MOBA_BLOCK
    return pl.pallas_call(
        functools.partial(_bias_tables_kernel, n_moba_heads=n_moba_heads, seq=seq),
        out_shape=(jax.ShapeDtypeStruct((n_heads, 2 * DSW_BLK), F32),
                   jax.ShapeDtypeStruct((n_moba_heads, width), F32)),
        name="bias_tables",
    )(rel_bias.T)


def _toeplitz(u_row, rows):
    x = jnp.broadcast_to(u_row, (rows, u_row.shape[1]))
    return pltpu.roll(x, 0, 1, stride=1, stride_axis=0)


def _dsw_bias_kernel(vec_ref, out_ref):
    lane = lax.broadcasted_iota(jnp.int32, (DSW_BLK, 2 * DSW_BLK), 1)
    for h in range(DSW_HEADS_PER_GROUP):
        tile = _toeplitz(vec_ref[0, h:h + 1, :], DSW_BLK)
        out_ref[0, h, 0] = tile
        out_ref[0, h, 1] = jnp.where(lane < DSW_BLK, NEG, tile)


def _dsw_bias_tiles(dsw_vec):
    n_groups = len(DSW_GROUPS)
    hg = DSW_HEADS_PER_GROUP
    vec = dsw_vec.reshape(n_groups, hg, 2 * DSW_BLK)
    return pl.pallas_call(
        _dsw_bias_kernel,
        grid=(n_groups,),
        in_specs=[pl.BlockSpec((1, hg, 2 * DSW_BLK), lambda g: (g, 0, 0))],
        out_specs=pl.BlockSpec((1, hg, 2, DSW_BLK, 2 * DSW_BLK), lambda g: (g, 0, 0, 0, 0)),
        out_shape=jax.ShapeDtypeStruct((n_groups, hg, 2, DSW_BLK, 2 * DSW_BLK), F32),
        compiler_params=_params(1),
        name="dsw_bias_tiles",
    )(vec)


def _rmsnorm_bf16(x, gain):
    ms = jnp.mean(x * x, axis=-1, keepdims=True)
    return (x * lax.rsqrt(ms + EPS) * gain).astype(BF16)


def _qkv_kernel(x_ref, g_ref, w_ref, qg_ref, kg_ref, bd_ref, *rest, width, dilations):
    n_groups = len(dilations)
    out_refs = rest[:3 * n_groups]
    u_s = rest[3 * n_groups]
    x = x_ref[0]
    rows = x.shape[0]
    ms = jnp.mean(x * x, axis=-1, keepdims=True)
    u = x * lax.rsqrt(ms + EPS) * g_ref[...]
    n_tiles = x.shape[1] // LANES
    if any(d > 1 for d in dilations):
        for j in range(n_tiles):
            u_s[j] = u[:, j * LANES:(j + 1) * LANES]
    chunk = bd_ref.shape[0]
    for gi, d in enumerate(dilations):
        per = rows // d
        if d == 1:
            lhs = u.astype(BF16)
        else:
            lhs = jnp.concatenate(
                [jnp.concatenate([u_s[j, pl.ds(c, per, stride=d), :] for j in range(n_tiles)], axis=1)
                 for c in range(d)], axis=0).astype(BF16)
        q_ref, k_ref, v_ref = out_refs[3 * gi:3 * gi + 3]
        for part, (gain_ref, out_ref) in enumerate(((qg_ref, q_ref), (kg_ref, k_ref), (None, v_ref))):
            col = (part * n_groups + gi) * width
            y_part = jnp.dot(lhs, w_ref[:, col:col + width], preferred_element_type=F32)
            for cc in range(width // chunk):
                y = y_part[:, cc * chunk:(cc + 1) * chunk]
                if gain_ref is not None:
                    msq = jnp.dot((y * y).astype(BF16), bd_ref[...], preferred_element_type=F32)
                    y = y * lax.rsqrt(msq + EPS) * gain_ref[...]
                y = y.astype(BF16)
                for c in range(d):
                    out_ref[0, c, :, cc * chunk:(cc + 1) * chunk] = y[c * per:(c + 1) * per]


def _qkv_project(x, gain, w, q_gain_row, k_gain_row, bd, dilations, tm, name):
    batch, seq, d_model = x.shape
    width = w.shape[1] // (3 * len(dilations))
    const = lambda b, t: (0, 0)
    out_specs, out_shapes = [], []
    for d in dilations:
        out_specs += [pl.BlockSpec((1, d, tm // d, width), lambda b, t: (b, 0, t, 0))] * 3
        out_shapes += [jax.ShapeDtypeStruct((batch, d, seq // d, width), BF16)] * 3
    outs = pl.pallas_call(
        functools.partial(_qkv_kernel, width=width, dilations=tuple(dilations)),
        grid=(batch, seq // tm),
        in_specs=[pl.BlockSpec((1, tm, d_model), lambda b, t: (b, t, 0)),
                  pl.BlockSpec((1, d_model), const),
                  pl.BlockSpec(w.shape, const),
                  pl.BlockSpec(q_gain_row.shape, const),
                  pl.BlockSpec(k_gain_row.shape, const),
                  pl.BlockSpec(bd.shape, const)],
        out_specs=tuple(out_specs),
        out_shape=tuple(out_shapes),
        scratch_shapes=[pltpu.VMEM((d_model // LANES, tm, LANES), F32)],
        compiler_params=_params(2),
        name=name,
    )(x, gain, w, q_gain_row, k_gain_row, bd)
    return [tuple(outs[3 * i:3 * i + 3]) for i in range(len(dilations))]


def _dsw_attn_kernel(q_ref, kc_ref, vc_ref, kp_ref, vp_ref, bias_ref, o_ref, stat_ref, *,
                     tq, blocks_per_seq):
    t = pl.program_id(0)
    n_blk = tq // DSW_BLK
    lane = lax.broadcasted_iota(jnp.int32, (DSW_BLK, LANES), 1)
    low_half = lane < HEAD_DIM
    n_pairs = q_ref.shape[1] // LANES
    for qi in range(n_blk):
        rows = slice(qi * DSW_BLK, (qi + 1) * DSW_BLK)
        first = (jnp.bitwise_and(t * n_blk + qi, blocks_per_seq - 1) == 0).astype(jnp.int32)
        stat_tile = jnp.zeros((DSW_BLK, LANES), F32)
        for hp in range(n_pairs):
            cols = slice(hp * LANES, (hp + 1) * LANES)
            q2 = q_ref[rows, cols]
            if qi == 0:
                k_prev, v_prev = kp_ref[:, cols], vp_ref[:, cols]
            else:
                prev = slice((qi - 1) * DSW_BLK, qi * DSW_BLK)
                k_prev, v_prev = kc_ref[prev, cols], vc_ref[prev, cols]
            k_cat = jnp.concatenate([k_prev, kc_ref[rows, cols]], axis=0)
            v_cat = jnp.concatenate([v_prev, vc_ref[rows, cols]], axis=0)
            outs = []
            for hh in range(2):
                head_lanes = low_half if hh == 0 else jnp.logical_not(low_half)
                qm = jnp.where(head_lanes, q2, jnp.zeros_like(q2))
                s = lax.dot_general(qm, k_cat, NT_DIMS, preferred_element_type=F32)
                s = s + bias_ref[0, hp * 2 + hh, first]
                m = jnp.max(s, axis=-1, keepdims=True)
                p = jnp.exp2(s - m)
                den = jnp.sum(p, axis=-1, keepdims=True)
                outs.append(jnp.dot(p.astype(BF16), v_cat, preferred_element_type=F32))
                head = hp * 2 + hh
                stat_tile = jnp.where(lane == head, m,
                                      jnp.where(lane == DSW_HEADS_PER_GROUP + head, den, stat_tile))
            o_ref[rows, cols] = jnp.where(low_half, outs[0], outs[1])
        stat_ref[rows, :] = stat_tile


def _dsw_attention(q, k, v, bias_tiles, group, tq, blocks_per_seq):
    n_rows, width = q.shape
    assert blocks_per_seq & (blocks_per_seq - 1) == 0
    n_blk = tq // DSW_BLK
    cur = pl.BlockSpec((tq, width), lambda t: (t, 0))
    prev = pl.BlockSpec((DSW_BLK, width), lambda t: (jnp.maximum(t * n_blk - 1, 0), 0))
    hg = DSW_HEADS_PER_GROUP
    return pl.pallas_call(
        functools.partial(_dsw_attn_kernel, tq=tq, blocks_per_seq=blocks_per_seq),
        grid=(n_rows // tq,),
        in_specs=[cur, cur, cur, prev, prev,
                  pl.BlockSpec((1, hg, 2, DSW_BLK, 2 * DSW_BLK), lambda t: (group, 0, 0, 0, 0))],
        out_specs=(pl.BlockSpec((tq, width), lambda t: (t, 0)),
                   pl.BlockSpec((tq, LANES), lambda t: (t, 0))),
        out_shape=(jax.ShapeDtypeStruct((n_rows, width), F32),
                   jax.ShapeDtypeStruct((n_rows, LANES), F32)),
        compiler_params=_params(1),
        name=f"dsw_attention_g{group}",
    )(q, k, v, k, v, bias_tiles)


def _merge_wo_kernel(*refs, dilations):
    n = len(dilations)
    o_refs, l_refs = refs[:n], refs[n:2 * n]
    x_ref, wo_ref, e_ref, out_ref = refs[2 * n:2 * n + 4]
    scratch = refs[2 * n + 4:]
    rows = x_ref.shape[1]
    outs, lses = [], []
    si = 0
    for d, o_ref, l_ref in zip(dilations, o_refs, l_refs):
        if d == 1:
            outs.append(o_ref[0, 0])
            lses.append(l_ref[0, 0])
            continue
        o_s, l_s = scratch[si], scratch[si + 1]
        si += 2
        n_tiles = o_s.shape[0]
        for c in range(d):
            piece = o_ref[0, c]
            for j in range(n_tiles):
                o_s[j, pl.ds(c, rows // d, stride=d), :] = piece[:, j * LANES:(j + 1) * LANES]
            l_s[pl.ds(c, rows // d, stride=d), :] = l_ref[0, c]
        outs.append(jnp.concatenate([o_s[j] for j in range(n_tiles)], axis=1))
        lses.append(l_s[...])
    lane = lax.broadcasted_iota(jnp.int32, lses[0].shape, 1)
    mx = functools.reduce(jnp.maximum, lses)
    es = [jnp.exp2(l - mx) for l in lses]
    sums = [pltpu.roll(l, LANES - DSW_HEADS_PER_GROUP, 1) for l in lses]
    total = functools.reduce(lambda a, b: a + b, [e * s for e, s in zip(es, sums)])
    merged = jnp.zeros(outs[0].shape, F32)
    for e, o in zip(es, outs):
        wgt = jnp.where(lane < DSW_HEADS_PER_GROUP, e / total, 0.0)
        hi = wgt.astype(BF16)
        lo = (wgt - hi.astype(F32)).astype(BF16)
        spread = (jnp.dot(hi, e_ref[...], preferred_element_type=F32)
                  + jnp.dot(lo, e_ref[...], preferred_element_type=F32))
        merged = merged + spread * o
    out_ref[0] = x_ref[0] + jnp.dot(merged.astype(BF16), wo_ref[...], preferred_element_type=F32)


def _merge_wo(os, lses, x, wo, dilations, tm):
    batch, seq, d_model = x.shape
    width = os[0].shape[-1]
    head_of_lane = jnp.arange(width) // HEAD_DIM
    expand = (jnp.arange(LANES)[:, None] == head_of_lane[None, :]).astype(BF16)
    const = lambda b, t: (0, 0)
    blk = lambda d, w: pl.BlockSpec((1, d, tm // d, w), lambda b, t: (b, 0, t, 0))
    x_spec = pl.BlockSpec((1, tm, d_model), lambda b, t: (b, t, 0))
    scratch = []
    for d in dilations:
        if d > 1:
            scratch += [pltpu.VMEM((width // LANES, tm, LANES), F32), pltpu.VMEM((tm, LANES), F32)]
    return pl.pallas_call(
        functools.partial(_merge_wo_kernel, dilations=tuple(dilations)),
        grid=(batch, seq // tm),
        in_specs=([blk(d, width) for d in dilations] + [blk(d, LANES) for d in dilations]
                  + [x_spec, pl.BlockSpec(wo.shape, const), pl.BlockSpec(expand.shape, const)]),
        out_specs=x_spec,
        out_shape=jax.ShapeDtypeStruct(x.shape, F32),
        scratch_shapes=scratch,
        compiler_params=_params(2),
        name="dsw_merge_wo",
    )(*os, *lses, x, wo, expand)


def _ffn_kernel(*refs, has_attn, ff_chunk):
    if has_attn:
        h_ref, a_ref, wo_ref, g_ref, w1_ref, w2_ref, out_ref = refs
        h = h_ref[...] + jnp.dot(a_ref[...], wo_ref[...], preferred_element_type=F32)
    else:
        h_ref, g_ref, w1_ref, w2_ref, out_ref = refs
        h = h_ref[...]
    u = _rmsnorm_bf16(h, g_ref[...])
    acc = h
    for c in range(w1_ref.shape[2] // ff_chunk):
        cols = slice(c * ff_chunk, (c + 1) * ff_chunk)
        a = jnp.dot(u, w1_ref[0, :, cols], preferred_element_type=F32)
        a = jnp.square(jnp.maximum(a, 0.0)).astype(BF16)
        acc = acc + jnp.dot(a, w2_ref[0, cols, :], preferred_element_type=F32)
    out_ref[...] = acc


def _ffn(h, gain, w1, w2, layer, tm, attn=None, wo=None, name="ffn"):
    n_rows, d_model = h.shape
    const = lambda i: (0, 0)
    row_spec = pl.BlockSpec((tm, d_model), lambda i: (i, 0))
    in_specs = [row_spec]
    args = [h]
    if attn is not None:
        in_specs += [pl.BlockSpec((tm, attn.shape[1]), lambda i: (i, 0)), pl.BlockSpec(wo.shape, const)]
        args += [attn, wo]
    in_specs += [pl.BlockSpec((1, d_model), const),
                 pl.BlockSpec((1,) + w1.shape[1:], lambda i: (layer, 0, 0)),
                 pl.BlockSpec((1,) + w2.shape[1:], lambda i: (layer, 0, 0))]
    args += [gain, w1, w2]
    return pl.pallas_call(
        functools.partial(_ffn_kernel, has_attn=attn is not None, ff_chunk=1024),
        grid=(n_rows // tm,),
        in_specs=in_specs,
        out_specs=row_spec,
        out_shape=jax.ShapeDtypeStruct(h.shape, F32),
        compiler_params=_params(1),
        name=name,
    )(*args)


def _moba_kernel(q_ref, k_ref, v_ref, wn_ref, o_ref,
                 bias_s, kp_s, vt_s, qa_s, s_s, acc_s, *, n_blocks, chunk):
    b = pl.program_id(1)
    blk = MOBA_BLOCK
    seq = n_blocks * blk
    span = chunk * blk
    lane = lax.broadcasted_iota(jnp.int32, (blk, LANES), 1)
    low_lanes = lane < HEAD_DIM
    v_rows = vt_s.shape[1]
    ones_row = (lax.broadcasted_iota(jnp.int32, (v_rows - HEAD_DIM, blk), 0) == 0).astype(F32)

    @pl.when(b == 0)
    def _build_bias():
        for hh in range(2):
            for dlt in range(n_blocks):
                u = jnp.concatenate([wn_ref[0, hh:hh + 1, (dlt + 1) * blk:(dlt + 2) * blk],
                                     wn_ref[0, hh:hh + 1, dlt * blk:(dlt + 1) * blk]], axis=1)
                r = n_blocks - 1 - dlt
                bias_s[hh, r * blk:(r + 1) * blk, :] = _toeplitz(u, blk)[:, :blk] * LOG2E
            bias_s[hh, seq:, :] = jnp.full((span - blk, blk), NEG, F32)

    sub = lax.broadcasted_iota(jnp.int32, (n_blocks, LANES), 0)
    kmean = jnp.zeros((n_blocks, LANES), F32)
    for j in range(n_blocks):
        rows = slice(j * blk, (j + 1) * blk)
        kj = k_ref[0, rows, :].astype(F32)
        vj_t = v_ref[0, rows, :].astype(F32).T
        kmean = jnp.where(sub == j, jnp.mean(kj, axis=0, keepdims=True), kmean)
        kp_s[0, rows, :] = jnp.where(low_lanes, kj, (lane == HEAD_DIM + j).astype(F32)).astype(BF16)
        kp_s[1, rows, :] = jnp.where(low_lanes, (lane == j).astype(F32), kj).astype(BF16)
        for hh in range(2):
            vt_s[hh, :, rows] = jnp.concatenate(
                [vj_t[hh * HEAD_DIM:(hh + 1) * HEAD_DIM], ones_row], axis=0).astype(BF16)
    km_hi = kmean.astype(BF16)
    km_lo = (kmean - km_hi.astype(F32)).astype(BF16)

    blk_row = lax.broadcasted_iota(jnp.int32, (n_blocks, blk), 0)
    blk_row_f = blk_row.astype(F32)
    for n in range(n_blocks):
        rows = slice(n * blk, (n + 1) * blk)
        q2 = q_ref[0, rows, :]
        for hh in range(2):
            own = low_lanes if hh == 0 else jnp.logical_not(low_lanes)
            qm = jnp.where(own, q2, jnp.zeros_like(q2))
            gate = (lax.dot_general(km_hi, qm, NT_DIMS, preferred_element_type=F32)
                    + lax.dot_general(km_lo, qm, NT_DIMS, preferred_element_type=F32))
            gate = jnp.where(blk_row < n, gate, -jnp.inf)
            chosen = blk_row == n
            for _ in range(min(MOBA_TOPK, n)):
                best = jnp.max(gate, axis=0, keepdims=True)
                cand = jnp.where(gate == best, blk_row_f, float(n_blocks))
                first = jnp.min(cand, axis=0, keepdims=True)
                pick = blk_row_f == first
                chosen = jnp.logical_or(chosen, pick)
                gate = jnp.where(pick, -jnp.inf, gate)
            pen_t = jnp.where(chosen, 0.0, NEG)
            pad_lo = HEAD_DIM if hh == 0 else 0
            pieces = [jnp.zeros((pad_lo, blk), F32)] if pad_lo else []
            pieces += [pen_t, jnp.zeros((LANES - pad_lo - n_blocks, blk), F32)]
            pen = jnp.concatenate(pieces, axis=0).T
            qa_s[hh, rows, :] = jnp.where(own, q2, pen.astype(BF16))

    def scores(hh, n, n_chunks):
        q_aug = qa_s[hh, pl.ds(_aligned(n * blk, blk), blk), :]
        bias_row0 = (n_blocks - 1 - n) * blk
        m = None
        for c in range(n_chunks):
            k_rows = slice(c * span, (c + 1) * span)
            b_rows = pl.ds(_aligned(bias_row0 + c * span, blk), span)
            s_t = lax.dot_general(kp_s[hh, k_rows, :], q_aug, NT_DIMS,
                                  preferred_element_type=F32)
            s_t = s_t + bias_s[hh, b_rows, :]
            s_s[hh, k_rows, :] = s_t
            cm = jnp.max(s_t, axis=0, keepdims=True)
            m = cm if m is None else jnp.maximum(m, cm)
        return m

    def weighted_values(hh, m, n_chunks):
        n_keys = n_chunks * span
        p_t = jnp.exp2(s_s[hh, :n_keys, :] - m).astype(BF16)
        return jnp.dot(vt_s[hh, :, :n_keys], p_t, preferred_element_type=F32)

    def emit(n):
        o_t = jnp.concatenate([acc_s[hh, :HEAD_DIM] / acc_s[hh, HEAD_DIM:HEAD_DIM + 1]
                               for hh in range(2)], axis=0)
        o_ref[0, pl.ds(_aligned(n * blk, blk), blk), :] = o_t.T.astype(o_ref.dtype)

    def step(n, m0, n_chunks, next_chunks):
        emit(jnp.maximum(n - 1, 0))
        acc0 = weighted_values(0, m0, n_chunks)
        m1 = scores(1, n, n_chunks)
        m0_next = scores(0, n + 1, next_chunks) if next_chunks else m0
        acc1 = weighted_values(1, m1, n_chunks)
        acc_s[0] = acc0
        acc_s[1] = acc1
        return m0_next

    acc_s[...] = jnp.ones(acc_s.shape, F32)
    n_groups = n_blocks // chunk
    m0 = scores(0, 0, 1)
    for g in range(n_groups):
        m0 = lax.fori_loop(0, chunk - 1,
                           lambda t, m, g=g: step(g * chunk + t, m, g + 1, g + 1), m0)
        m0 = step(g * chunk + chunk - 1, m0, g + 1, g + 2 if g + 1 < n_groups else 0)
    emit(n_blocks - 1)


def _moba_attention(q, k, v, moba_vec, batch, seq, chunk):
    n_rows, width = q.shape
    n_pairs = width // LANES
    n_blocks = seq // MOBA_BLOCK
    assert n_blocks % chunk == 0
    wn = moba_vec.reshape(n_pairs, 2, moba_vec.shape[1])
    spec = pl.BlockSpec((1, seq, LANES), lambda p, b: (b, 0, p))
    shape3 = (batch, seq, width)
    out = pl.pallas_call(
        functools.partial(_moba_kernel, n_blocks=n_blocks, chunk=chunk),
        grid=(n_pairs, batch),
        in_specs=[spec, spec, spec, pl.BlockSpec((1, 2, wn.shape[2]), lambda p, b: (p, 0, 0))],
        out_specs=spec,
        out_shape=jax.ShapeDtypeStruct(shape3, BF16),
        scratch_shapes=[pltpu.VMEM((2, seq + (chunk - 1) * MOBA_BLOCK, MOBA_BLOCK), F32),
                        pltpu.VMEM((2, seq, LANES), BF16),
                        pltpu.VMEM((2, HEAD_DIM + 16, seq), BF16),
                        pltpu.VMEM((2, seq, LANES), BF16),
                        pltpu.VMEM((2, seq, MOBA_BLOCK), F32),
                        pltpu.VMEM((2, HEAD_DIM + 16, MOBA_BLOCK), F32)],
        compiler_params=_params(2),
        name="moba_attention",
    )(q.reshape(shape3), k.reshape(shape3), v.reshape(shape3), wn)
    return out.reshape(n_rows, width)


def _head_mean_matrix(size):
    idx = jnp.arange(size) // HEAD_DIM
    return (idx[:, None] == idx[None, :]).astype(BF16) * (1.0 / HEAD_DIM)


def _gain_row(gain, size, scale):
    return (jnp.tile(gain.astype(F32), size // HEAD_DIM) * scale).reshape(1, size)


def kernel(x, rel_bias, norm_mix, norm_ffn, a_w_qkv, a_q_gain, a_k_gain, a_w_o,
           b_w_qkv, b_q_gain, b_k_gain, b_w_o, ffn_w1, ffn_w2):
    batch, seq, d_model = x.shape
    n_rows = batch * seq
    n_groups = len(DSW_GROUPS)
    dilations = [d for _, d in DSW_GROUPS]
    gw = DSW_HEADS_PER_GROUP * HEAD_DIM
    moba_heads = b_w_o.shape[1] // HEAD_DIM
    bd = _head_mean_matrix(2 * LANES)

    dsw_vec, moba_vec = _bias_tables(rel_bias, moba_heads, seq)
    dsw_tiles = _dsw_bias_tiles(dsw_vec)

    qkv = _qkv_project(x, norm_mix[0].reshape(1, d_model), a_w_qkv[0].astype(BF16),
                       _gain_row(a_q_gain[0], 2 * LANES, SCALE * LOG2E), _gain_row(a_k_gain[0], 2 * LANES, 1.0),
                       bd, dilations, tm=512, name="dsw_qkv")
    os, lses = [], []
    for g, d in enumerate(dilations):
        q_g, k_g, v_g = (a.reshape(n_rows, gw) for a in qkv[g])
        o_g, lse_g = _dsw_attention(q_g, k_g, v_g, dsw_tiles, g, tq=512,
                                    blocks_per_seq=seq // d // DSW_BLK)
        os.append(o_g.reshape(batch, d, seq // d, gw))
        lses.append(lse_g.reshape(batch, d, seq // d, LANES))
    h = _merge_wo(os, lses, x, a_w_o[0].astype(BF16), dilations, tm=512)
    h = h.reshape(n_rows, d_model)
    w1, w2 = ffn_w1.astype(BF16), ffn_w2.astype(BF16)
    h = _ffn(h, norm_ffn[0].reshape(1, d_model), w1, w2, 0, tm=512, name="ffn0")

    width = moba_heads * HEAD_DIM
    (q, k, v), = _qkv_project(
        h.reshape(batch, seq, d_model), norm_mix[1].reshape(1, d_model), b_w_qkv[0].astype(BF16),
        _gain_row(b_q_gain[0], 2 * LANES, SCALE * LOG2E), _gain_row(b_k_gain[0], 2 * LANES, 1.0),
        bd, [1], tm=512, name="moba_qkv")
    q, k, v = (a.reshape(n_rows, width) for a in (q, k, v))
    attn = _moba_attention(q, k, v, moba_vec, batch, seq, chunk=4)
    h = _ffn(h, norm_ffn[1].reshape(1, d_model), w1, w2, 1,
             tm=512, attn=attn, wo=b_w_o[0].astype(BF16), name="wo_ffn1")
    return h.reshape(batch, seq, d_model)
```

```python
import functools
import math

import jax
import jax.numpy as jnp
from jax import lax
from jax.experimental import pallas as pl
from jax.experimental.pallas import tpu as pltpu

HEAD_DIM = 64
LANES = 128
DSW_GROUPS = ((128, 1), (512, 4), (2048, 16))
DSW_BLK = 128
DSW_HEADS_PER_GROUP = 8
MOBA_BLOCK = 256
MOBA_TOPK = 3
REL_BUCKETS = 32
REL_MAX_DISTANCE = 2048
EPS = 1e-6
NEG = -1e30
SCALE = HEAD_DIM ** -0.5
LOG2E = 1.4426950408889634
VMEM_LIMIT_BYTES = 56 * 1024 * 1024

F32 = jnp.float32
BF16 = jnp.bfloat16
NT_DIMS = (((1,), (1,)), ((), ()))


def _params(n_axes):
    return pltpu.CompilerParams(dimension_semantics=("arbitrary",) * n_axes,
                                vmem_limit_bytes=VMEM_LIMIT_BYTES)


def _aligned(start, multiple):
    return start if isinstance(start, int) else pl.multiple_of(start, multiple)


def _t5_bucket(dist):
    n = jnp.maximum(dist, 0)
    max_exact = REL_BUCKETS // 2
    nf = jnp.maximum(n, 1).astype(F32)
    large = max_exact + (jnp.log(nf / max_exact) / math.log(REL_MAX_DISTANCE / max_exact)
                         * (REL_BUCKETS - max_exact)).astype(jnp.int32)
    large = jnp.minimum(large, REL_BUCKETS - 1)
    return jnp.where(n < max_exact, n, large)


def _lookup(bucket, table_t):
    acc = jnp.zeros(bucket.shape, F32)
    for b in range(REL_BUCKETS):
        acc = jnp.where(bucket == b, table_t[:, b:b + 1], acc)
    return acc


def _bias_tables_kernel(tab_ref, dsw_ref, moba_ref, *, n_moba_heads, seq):
    tab = tab_ref[...]
    n_heads = dsw_ref.shape[0]
    m = lax.broadcasted_iota(jnp.int32, (n_heads, 2 * DSW_BLK), 1)
    row = lax.broadcasted_iota(jnp.int32, (n_heads, 2 * DSW_BLK), 0)
    dil = jnp.where(row < DSW_HEADS_PER_GROUP, DSW_GROUPS[0][1],
                    jnp.where(row < 2 * DSW_HEADS_PER_GROUP, DSW_GROUPS[1][1], DSW_GROUPS[2][1]))
    sub = DSW_BLK - m
    vals = _lookup(_t5_bucket(sub * dil), tab)
    dsw_ref[...] = jnp.where(sub >= 0, vals * LOG2E, NEG)
    width = moba_ref.shape[1]
    t = lax.broadcasted_iota(jnp.int32, (n_moba_heads, width), 1)
    dist = t - MOBA_BLOCK
    vals = _lookup(_t5_bucket(dist), tab[:n_moba_heads])
    moba_ref[...] = jnp.where(dist >= 0, vals, NEG)


def _bias_tables(rel_bias, n_moba_heads, seq):
    n_heads = rel_bias.shape[1]
    width = seq + MOBA_BLOCK
    return pl.pallas_call(
        functools.partial(_bias_tables_kernel, n_moba_heads=n_moba_heads, seq=seq),
        out_shape=(jax.ShapeDtypeStruct((n_heads, 2 * DSW_BLK), F32),
                   jax.ShapeDtypeStruct((n_moba_heads, width), F32)),
        name="bias_tables",
    )(rel_bias.T)


def _toeplitz(u_row, rows):
    x = jnp.broadcast_to(u_row, (rows, u_row.shape[1]))
    return pltpu.roll(x, 0, 1, stride=1, stride_axis=0)


def _dsw_bias_kernel(vec_ref, out_ref):
    lane = lax.broadcasted_iota(jnp.int32, (DSW_BLK, 2 * DSW_BLK), 1)
    for h in range(DSW_HEADS_PER_GROUP):
        tile = _toeplitz(vec_ref[0, h:h + 1, :], DSW_BLK)
        out_ref[0, h, 0] = tile
        out_ref[0, h, 1] = jnp.where(lane < DSW_BLK, NEG, tile)


def _dsw_bias_tiles(dsw_vec):
    n_groups = len(DSW_GROUPS)
    hg = DSW_HEADS_PER_GROUP
    vec = dsw_vec.reshape(n_groups, hg, 2 * DSW_BLK)
    return pl.pallas_call(
        _dsw_bias_kernel,
        grid=(n_groups,),
        in_specs=[pl.BlockSpec((1, hg, 2 * DSW_BLK), lambda g: (g, 0, 0))],
        out_specs=pl.BlockSpec((1, hg, 2, DSW_BLK, 2 * DSW_BLK), lambda g: (g, 0, 0, 0, 0)),
        out_shape=jax.ShapeDtypeStruct((n_groups, hg, 2, DSW_BLK, 2 * DSW_BLK), F32),
        compiler_params=_params(1),
        name="dsw_bias_tiles",
    )(vec)


def _rmsnorm_bf16(x, gain):
    ms = jnp.mean(x * x, axis=-1, keepdims=True)
    return (x * lax.rsqrt(ms + EPS) * gain).astype(BF16)


def _qkv_kernel(x_ref, g_ref, w_ref, qg_ref, kg_ref, bd_ref, *rest, width, dilations):
    n_groups = len(dilations)
    out_refs = rest[:3 * n_groups]
    u_s = rest[3 * n_groups]
    x = x_ref[0]
    rows = x.shape[0]
    ms = jnp.mean(x * x, axis=-1, keepdims=True)
    u = x * lax.rsqrt(ms + EPS) * g_ref[...]
    n_tiles = x.shape[1] // LANES
    if any(d > 1 for d in dilations):
        for j in range(n_tiles):
            u_s[j] = u[:, j * LANES:(j + 1) * LANES]
    chunk = bd_ref.shape[0]
    for gi, d in enumerate(dilations):
        per = rows // d
        if d == 1:
            lhs = u.astype(BF16)
        else:
            lhs = jnp.concatenate(
                [jnp.concatenate([u_s[j, pl.ds(c, per, stride=d), :] for j in range(n_tiles)], axis=1)
                 for c in range(d)], axis=0).astype(BF16)
        q_ref, k_ref, v_ref = out_refs[3 * gi:3 * gi + 3]
        for part, (gain_ref, out_ref) in enumerate(((qg_ref, q_ref), (kg_ref, k_ref), (None, v_ref))):
            col = (part * n_groups + gi) * width
            y_part = jnp.dot(lhs, w_ref[:, col:col + width].astype(BF16), preferred_element_type=F32)
            for cc in range(width // chunk):
                y = y_part[:, cc * chunk:(cc + 1) * chunk]
                if gain_ref is not None:
                    msq = jnp.dot((y * y).astype(BF16), bd_ref[...], preferred_element_type=F32)
                    y = y * lax.rsqrt(msq + EPS) * gain_ref[...]
                y = y.astype(BF16)
                for c in range(d):
                    out_ref[0, c, :, cc * chunk:(cc + 1) * chunk] = y[c * per:(c + 1) * per]


def _qkv_project(x, gain, w, q_gain_row, k_gain_row, bd, dilations, tm, name):
    batch, seq, d_model = x.shape
    width = w.shape[1] // (3 * len(dilations))
    const = lambda b, t: (0, 0)
    out_specs, out_shapes = [], []
    for d in dilations:
        out_specs += [pl.BlockSpec((1, d, tm // d, width), lambda b, t: (b, 0, t, 0))] * 3
        out_shapes += [jax.ShapeDtypeStruct((batch, d, seq // d, width), BF16)] * 3
    outs = pl.pallas_call(
        functools.partial(_qkv_kernel, width=width, dilations=tuple(dilations)),
        grid=(batch, seq // tm),
        in_specs=[pl.BlockSpec((1, tm, d_model), lambda b, t: (b, t, 0)),
                  pl.BlockSpec((1, d_model), const),
                  pl.BlockSpec(w.shape, const, pipeline_mode=pl.Buffered(1)),
                  pl.BlockSpec(q_gain_row.shape, const),
                  pl.BlockSpec(k_gain_row.shape, const),
                  pl.BlockSpec(bd.shape, const)],
        out_specs=tuple(out_specs),
        out_shape=tuple(out_shapes),
        scratch_shapes=[pltpu.VMEM((d_model // LANES, tm, LANES), F32)],
        compiler_params=_params(2),
        name=name,
    )(x, gain, w, q_gain_row, k_gain_row, bd)
    return [tuple(outs[3 * i:3 * i + 3]) for i in range(len(dilations))]


def _dsw_attn_kernel(q_ref, kc_ref, vc_ref, kp_ref, vp_ref, bias_ref, o_ref, stat_ref, *,
                     tq, blocks_per_seq):
    t = pl.program_id(0)
    n_blk = tq // DSW_BLK
    lane = lax.broadcasted_iota(jnp.int32, (DSW_BLK, LANES), 1)
    low_half = lane < HEAD_DIM
    n_pairs = q_ref.shape[1] // LANES
    for qi in range(n_blk):
        rows = slice(qi * DSW_BLK, (qi + 1) * DSW_BLK)
        first = (jnp.bitwise_and(t * n_blk + qi, blocks_per_seq - 1) == 0).astype(jnp.int32)
        stat_tile = jnp.zeros((DSW_BLK, LANES), F32)
        for hp in range(n_pairs):
            cols = slice(hp * LANES, (hp + 1) * LANES)
            q2 = q_ref[rows, cols]
            if qi == 0:
                k_prev, v_prev = kp_ref[:, cols], vp_ref[:, cols]
            else:
                prev = slice((qi - 1) * DSW_BLK, qi * DSW_BLK)
                k_prev, v_prev = kc_ref[prev, cols], vc_ref[prev, cols]
            k_cat = jnp.concatenate([k_prev, kc_ref[rows, cols]], axis=0)
            v_cat = jnp.concatenate([v_prev, vc_ref[rows, cols]], axis=0)
            outs = []
            for hh in range(2):
                head_lanes = low_half if hh == 0 else jnp.logical_not(low_half)
                qm = jnp.where(head_lanes, q2, jnp.zeros_like(q2))
                s = lax.dot_general(qm, k_cat, NT_DIMS, preferred_element_type=F32)
                s = s + bias_ref[0, hp * 2 + hh, first]
                m = jnp.max(s, axis=-1, keepdims=True)
                p = jnp.exp2(s - m)
                den = jnp.sum(p, axis=-1, keepdims=True)
                outs.append(jnp.dot(p.astype(BF16), v_cat, preferred_element_type=F32))
                head = hp * 2 + hh
                stat_tile = jnp.where(lane == head, m,
                                      jnp.where(lane == DSW_HEADS_PER_GROUP + head, den, stat_tile))
            o_ref[rows, cols] = jnp.where(low_half, outs[0], outs[1])
        stat_ref[rows, :] = stat_tile


def _dsw_attention(q, k, v, bias_tiles, group, tq, blocks_per_seq):
    n_rows, width = q.shape
    assert blocks_per_seq & (blocks_per_seq - 1) == 0
    n_blk = tq // DSW_BLK
    cur = pl.BlockSpec((tq, width), lambda t: (t, 0))
    prev = pl.BlockSpec((DSW_BLK, width), lambda t: (jnp.maximum(t * n_blk - 1, 0), 0))
    hg = DSW_HEADS_PER_GROUP
    return pl.pallas_call(
        functools.partial(_dsw_attn_kernel, tq=tq, blocks_per_seq=blocks_per_seq),
        grid=(n_rows // tq,),
        in_specs=[cur, cur, cur, prev, prev,
                  pl.BlockSpec((1, hg, 2, DSW_BLK, 2 * DSW_BLK), lambda t: (group, 0, 0, 0, 0))],
        out_specs=(pl.BlockSpec((tq, width), lambda t: (t, 0)),
                   pl.BlockSpec((tq, LANES), lambda t: (t, 0))),
        out_shape=(jax.ShapeDtypeStruct((n_rows, width), F32),
                   jax.ShapeDtypeStruct((n_rows, LANES), F32)),
        compiler_params=_params(1),
        name=f"dsw_attention_g{group}",
    )(q, k, v, k, v, bias_tiles)


def _merge_wo_kernel(*refs, dilations):
    n = len(dilations)
    o_refs, l_refs = refs[:n], refs[n:2 * n]
    x_ref, wo_ref, e_ref, out_ref = refs[2 * n:2 * n + 4]
    scratch = refs[2 * n + 4:]
    rows = x_ref.shape[1]
    outs, lses = [], []
    si = 0
    for d, o_ref, l_ref in zip(dilations, o_refs, l_refs):
        if d == 1:
            outs.append(o_ref[0, 0])
            lses.append(l_ref[0, 0])
            continue
        o_s, l_s = scratch[si], scratch[si + 1]
        si += 2
        n_tiles = o_s.shape[0]
        for c in range(d):
            piece = o_ref[0, c]
            for j in range(n_tiles):
                o_s[j, pl.ds(c, rows // d, stride=d), :] = piece[:, j * LANES:(j + 1) * LANES]
            l_s[pl.ds(c, rows // d, stride=d), :] = l_ref[0, c]
        outs.append(jnp.concatenate([o_s[j] for j in range(n_tiles)], axis=1))
        lses.append(l_s[...])
    lane = lax.broadcasted_iota(jnp.int32, lses[0].shape, 1)
    mx = functools.reduce(jnp.maximum, lses)
    es = [jnp.exp2(l - mx) for l in lses]
    sums = [pltpu.roll(l, LANES - DSW_HEADS_PER_GROUP, 1) for l in lses]
    total = functools.reduce(lambda a, b: a + b, [e * s for e, s in zip(es, sums)])
    merged = jnp.zeros(outs[0].shape, F32)
    for e, o in zip(es, outs):
        wgt = jnp.where(lane < DSW_HEADS_PER_GROUP, e / total, 0.0)
        hi = wgt.astype(BF16)
        lo = (wgt - hi.astype(F32)).astype(BF16)
        spread = (jnp.dot(hi, e_ref[...], preferred_element_type=F32)
                  + jnp.dot(lo, e_ref[...], preferred_element_type=F32))
        merged = merged + spread * o
    out_ref[0] = x_ref[0] + jnp.dot(merged.astype(BF16), wo_ref[...].astype(BF16),
                                    preferred_element_type=F32)


def _merge_wo(os, lses, x, wo, dilations, tm):
    batch, seq, d_model = x.shape
    width = os[0].shape[-1]
    head_of_lane = jnp.arange(width) // HEAD_DIM
    expand = (jnp.arange(LANES)[:, None] == head_of_lane[None, :]).astype(BF16)
    const = lambda b, t: (0, 0)
    blk = lambda d, w: pl.BlockSpec((1, d, tm // d, w), lambda b, t: (b, 0, t, 0))
    x_spec = pl.BlockSpec((1, tm, d_model), lambda b, t: (b, t, 0))
    scratch = []
    for d in dilations:
        if d > 1:
            scratch += [pltpu.VMEM((width // LANES, tm, LANES), F32), pltpu.VMEM((tm, LANES), F32)]
    return pl.pallas_call(
        functools.partial(_merge_wo_kernel, dilations=tuple(dilations)),
        grid=(batch, seq // tm),
        in_specs=([blk(d, width) for d in dilations] + [blk(d, LANES) for d in dilations]
                  + [x_spec, pl.BlockSpec(wo.shape, const), pl.BlockSpec(expand.shape, const)]),
        out_specs=x_spec,
        out_shape=jax.ShapeDtypeStruct(x.shape, F32),
        scratch_shapes=scratch,
        compiler_params=_params(2),
        name="dsw_merge_wo",
    )(*os, *lses, x, wo, expand)


def _ffn_kernel(*refs, has_attn, ff_chunk):
    if has_attn:
        h_ref, a_ref, wo_ref, g_ref, w1_ref, w2_ref, out_ref = refs
        h = h_ref[...] + jnp.dot(a_ref[...], wo_ref[...].astype(BF16), preferred_element_type=F32)
    else:
        h_ref, g_ref, w1_ref, w2_ref, out_ref = refs
        h = h_ref[...]
    u = _rmsnorm_bf16(h, g_ref[...])
    acc = h
    for c in range(w1_ref.shape[2] // ff_chunk):
        cols = slice(c * ff_chunk, (c + 1) * ff_chunk)
        a = jnp.dot(u, w1_ref[0, :, cols].astype(BF16), preferred_element_type=F32)
        a = jnp.square(jnp.maximum(a, 0.0)).astype(BF16)
        acc = acc + jnp.dot(a, w2_ref[0, cols, :].astype(BF16), preferred_element_type=F32)
    out_ref[...] = acc


def _ffn(h, gain, w1, w2, layer, tm, attn=None, wo=None, name="ffn"):
    n_rows, d_model = h.shape
    const = lambda i: (0, 0)
    row_spec = pl.BlockSpec((tm, d_model), lambda i: (i, 0))
    in_specs = [row_spec]
    args = [h]
    if attn is not None:
        in_specs += [pl.BlockSpec((tm, attn.shape[1]), lambda i: (i, 0)), pl.BlockSpec(wo.shape, const)]
        args += [attn, wo]
    in_specs += [pl.BlockSpec((1, d_model), const),
                 pl.BlockSpec((1,) + w1.shape[1:], lambda i: (layer, 0, 0), pipeline_mode=pl.Buffered(1)),
                 pl.BlockSpec((1,) + w2.shape[1:], lambda i: (layer, 0, 0), pipeline_mode=pl.Buffered(1))]
    args += [gain, w1, w2]
    return pl.pallas_call(
        functools.partial(_ffn_kernel, has_attn=attn is not None, ff_chunk=1024),
        grid=(n_rows // tm,),
        in_specs=in_specs,
        out_specs=row_spec,
        out_shape=jax.ShapeDtypeStruct(h.shape, F32),
        compiler_params=_params(1),
        name=name,
    )(*args)


def _moba_kernel(q_ref, k_ref, v_ref, wn_ref, o_ref,
                 bias_s, kp_s, vt_s, qa_s, s_s, acc_s, *, n_blocks, chunk):
    b = pl.program_id(1)
    blk = MOBA_BLOCK
    seq = n_blocks * blk
    span = chunk * blk
    lane = lax.broadcasted_iota(jnp.int32, (blk, LANES), 1)
    low_lanes = lane < HEAD_DIM
    v_rows = vt_s.shape[1]
    ones_row = (lax.broadcasted_iota(jnp.int32, (v_rows - HEAD_DIM, blk), 0) == 0).astype(F32)

    @pl.when(b == 0)
    def _build_bias():
        for hh in range(2):
            for dlt in range(n_blocks):
                u = jnp.concatenate([wn_ref[0, hh:hh + 1, (dlt + 1) * blk:(dlt + 2) * blk],
                                     wn_ref[0, hh:hh + 1, dlt * blk:(dlt + 1) * blk]], axis=1)
                r = n_blocks - 1 - dlt
                bias_s[hh, r * blk:(r + 1) * blk, :] = _toeplitz(u, blk)[:, :blk] * LOG2E
            bias_s[hh, seq:, :] = jnp.full((span - blk, blk), NEG, F32)

    sub = lax.broadcasted_iota(jnp.int32, (n_blocks, LANES), 0)
    kmean = jnp.zeros((n_blocks, LANES), F32)
    for j in range(n_blocks):
        rows = slice(j * blk, (j + 1) * blk)
        kj = k_ref[0, rows, :].astype(F32)
        vj_t = v_ref[0, rows, :].astype(F32).T
        kmean = jnp.where(sub == j, jnp.mean(kj, axis=0, keepdims=True), kmean)
        kp_s[0, rows, :] = jnp.where(low_lanes, kj, (lane == HEAD_DIM + j).astype(F32)).astype(BF16)
        kp_s[1, rows, :] = jnp.where(low_lanes, (lane == j).astype(F32), kj).astype(BF16)
        for hh in range(2):
            vt_s[hh, :, rows] = jnp.concatenate(
                [vj_t[hh * HEAD_DIM:(hh + 1) * HEAD_DIM], ones_row], axis=0).astype(BF16)
    km_hi = kmean.astype(BF16)
    km_lo = (kmean - km_hi.astype(F32)).astype(BF16)

    blk_row = lax.broadcasted_iota(jnp.int32, (n_blocks, blk), 0)
    blk_row_f = blk_row.astype(F32)
    for n in range(n_blocks):
        rows = slice(n * blk, (n + 1) * blk)
        q2 = q_ref[0, rows, :]
        for hh in range(2):
            own = low_lanes if hh == 0 else jnp.logical_not(low_lanes)
            qm = jnp.where(own, q2, jnp.zeros_like(q2))
            gate = (lax.dot_general(km_hi, qm, NT_DIMS, preferred_element_type=F32)
                    + lax.dot_general(km_lo, qm, NT_DIMS, preferred_element_type=F32))
            gate = jnp.where(blk_row < n, gate, -jnp.inf)
            chosen = blk_row == n
            for _ in range(min(MOBA_TOPK, n)):
                best = jnp.max(gate, axis=0, keepdims=True)
                cand = jnp.where(gate == best, blk_row_f, float(n_blocks))
                first = jnp.min(cand, axis=0, keepdims=True)
                pick = blk_row_f == first
                chosen = jnp.logical_or(chosen, pick)
                gate = jnp.where(pick, -jnp.inf, gate)
            pen_t = jnp.where(chosen, 0.0, NEG)
            pad_lo = HEAD_DIM if hh == 0 else 0
            pieces = [jnp.zeros((pad_lo, blk), F32)] if pad_lo else []
            pieces += [pen_t, jnp.zeros((LANES - pad_lo - n_blocks, blk), F32)]
            pen = jnp.concatenate(pieces, axis=0).T
            qa_s[hh, rows, :] = jnp.where(own, q2, pen.astype(BF16))

    def scores(hh, n, n_chunks):
        q_aug = qa_s[hh, pl.ds(_aligned(n * blk, blk), blk), :]
        bias_row0 = (n_blocks - 1 - n) * blk
        m = None
        for c in range(n_chunks):
            k_rows = slice(c * span, (c + 1) * span)
            b_rows = pl.ds(_aligned(bias_row0 + c * span, blk), span)
            s_t = lax.dot_general(kp_s[hh, k_rows, :], q_aug, NT_DIMS,
                                  preferred_element_type=F32)
            s_t = s_t + bias_s[hh, b_rows, :]
            s_s[hh, k_rows, :] = s_t
            cm = jnp.max(s_t, axis=0, keepdims=True)
            m = cm if m is None else jnp.maximum(m, cm)
        return m

    def weighted_values(hh, m, n_chunks):
        n_keys = n_chunks * span
        p_t = jnp.exp2(s_s[hh, :n_keys, :] - m).astype(BF16)
        return jnp.dot(vt_s[hh, :, :n_keys], p_t, preferred_element_type=F32)

    def emit(n):
        o_t = jnp.concatenate([acc_s[hh, :HEAD_DIM] / acc_s[hh, HEAD_DIM:HEAD_DIM + 1]
                               for hh in range(2)], axis=0)
        o_ref[0, pl.ds(_aligned(n * blk, blk), blk), :] = o_t.T.astype(o_ref.dtype)

    def step(n, m0, n_chunks, next_chunks):
        emit(jnp.maximum(n - 1, 0))
        acc0 = weighted_values(0, m0, n_chunks)
        m1 = scores(1, n, n_chunks)
        m0_next = scores(0, n + 1, next_chunks) if next_chunks else m0
        acc1 = weighted_values(1, m1, n_chunks)
        acc_s[0] = acc0
        acc_s[1] = acc1
        return m0_next

    acc_s[...] = jnp.ones(acc_s.shape, F32)
    n_groups = n_blocks // chunk
    m0 = scores(0, 0, 1)
    for g in range(n_groups):
        m0 = lax.fori_loop(0, chunk - 1,
                           lambda t, m, g=g: step(g * chunk + t, m, g + 1, g + 1), m0)
        m0 = step(g * chunk + chunk - 1, m0, g + 1, g + 2 if g + 1 < n_groups else 0)
    emit(n_blocks - 1)


def _moba_attention(q, k, v, moba_vec, batch, seq, chunk):
    n_rows, width = q.shape
    n_pairs = width // LANES
    n_blocks = seq // MOBA_BLOCK
    assert n_blocks % chunk == 0
    wn = moba_vec.reshape(n_pairs, 2, moba_vec.shape[1])
    spec = pl.BlockSpec((1, seq, LANES), lambda p, b: (b, 0, p))
    shape3 = (batch, seq, width)
    out = pl.pallas_call(
        functools.partial(_moba_kernel, n_blocks=n_blocks, chunk=chunk),
        grid=(n_pairs, batch),
        in_specs=[spec, spec, spec, pl.BlockSpec((1, 2, wn.shape[2]), lambda p, b: (p, 0, 0))],
        out_specs=spec,
        out_shape=jax.ShapeDtypeStruct(shape3, BF16),
        scratch_shapes=[pltpu.VMEM((2, seq + (chunk - 1) * MOBA_BLOCK, MOBA_BLOCK), F32),
                        pltpu.VMEM((2, seq, LANES), BF16),
                        pltpu.VMEM((2, HEAD_DIM + 16, seq), BF16),
                        pltpu.VMEM((2, seq, LANES), BF16),
                        pltpu.VMEM((2, seq, MOBA_BLOCK), F32),
                        pltpu.VMEM((2, HEAD_DIM + 16, MOBA_BLOCK), F32)],
        compiler_params=_params(2),
        name="moba_attention",
    )(q.reshape(shape3), k.reshape(shape3), v.reshape(shape3), wn)
    return out.reshape(n_rows, width)


def _head_mean_matrix(size):
    idx = jnp.arange(size) // HEAD_DIM
    return (idx[:, None] == idx[None, :]).astype(BF16) * (1.0 / HEAD_DIM)


def _gain_row(gain, size, scale):
    return (jnp.tile(gain.astype(F32), size // HEAD_DIM) * scale).reshape(1, size)


def kernel(x, rel_bias, norm_mix, norm_ffn, a_w_qkv, a_q_gain, a_k_gain, a_w_o,
           b_w_qkv, b_q_gain, b_k_gain, b_w_o, ffn_w1, ffn_w2):
    batch, seq, d_model = x.shape
    n_rows = batch * seq
    n_groups = len(DSW_GROUPS)
    dilations = [d for _, d in DSW_GROUPS]
    gw = DSW_HEADS_PER_GROUP * HEAD_DIM
    moba_heads = b_w_o.shape[1] // HEAD_DIM
    bd = _head_mean_matrix(2 * LANES)

    dsw_vec, moba_vec = _bias_tables(rel_bias, moba_heads, seq)
    dsw_tiles = _dsw_bias_tiles(dsw_vec)

    qkv = _qkv_project(x, norm_mix[0].reshape(1, d_model), a_w_qkv[0],
                       _gain_row(a_q_gain[0], 2 * LANES, SCALE * LOG2E), _gain_row(a_k_gain[0], 2 * LANES, 1.0),
                       bd, dilations, tm=512, name="dsw_qkv")
    os, lses = [], []
    for g, d in enumerate(dilations):
        q_g, k_g, v_g = (a.reshape(n_rows, gw) for a in qkv[g])
        o_g, lse_g = _dsw_attention(q_g, k_g, v_g, dsw_tiles, g, tq=512,
                                    blocks_per_seq=seq // d // DSW_BLK)
        os.append(o_g.reshape(batch, d, seq // d, gw))
        lses.append(lse_g.reshape(batch, d, seq // d, LANES))
    h = _merge_wo(os, lses, x, a_w_o[0], dilations, tm=512)
    h = h.reshape(n_rows, d_model)
    w1, w2 = ffn_w1, ffn_w2
    h = _ffn(h, norm_ffn[0].reshape(1, d_model), w1, w2, 0, tm=512, name="ffn0")

    width = moba_heads * HEAD_DIM
    (q, k, v), = _qkv_project(
        h.reshape(batch, seq, d_model), norm_mix[1].reshape(1, d_model), b_w_qkv[0],
        _gain_row(b_q_gain[0], 2 * LANES, SCALE * LOG2E), _gain_row(b_k_gain[0], 2 * LANES, 1.0),
        bd, [1], tm=512, name="moba_qkv")
    q, k, v = (a.reshape(n_rows, width) for a in (q, k, v))
    attn = _moba_attention(q, k, v, moba_vec, batch, seq, chunk=4)
    h = _ffn(h, norm_ffn[1].reshape(1, d_model), w1, w2, 1,
             tm=512, attn=attn, wo=b_w_o[0], name="wo_ffn1")
    return h.reshape(batch, seq, d_model)
```

```python
import functools
import math

import jax
import jax.numpy as jnp
from jax import lax
from jax.experimental import pallas as pl
from jax.experimental.pallas import tpu as pltpu

HEAD_DIM = 64
LANES = 128
DSW_GROUPS = ((128, 1), (512, 4), (2048, 16))
DSW_BLK = 128
DSW_HEADS_PER_GROUP = 8
MOBA_BLOCK = 256
MOBA_TOPK = 3
REL_BUCKETS = 32
REL_MAX_DISTANCE = 2048
EPS = 1e-6
NEG = -1e30
SCALE = HEAD_DIM ** -0.5
LOG2E = 1.4426950408889634
EXP_RANGE = 100.0
VMEM_LIMIT_BYTES = 56 * 1024 * 1024

F32 = jnp.float32
BF16 = jnp.bfloat16
NT_DIMS = (((1,), (1,)), ((), ()))


def _params(n_axes):
    return pltpu.CompilerParams(dimension_semantics=("arbitrary",) * n_axes,
                                vmem_limit_bytes=VMEM_LIMIT_BYTES)


def _aligned(start, multiple):
    return start if isinstance(start, int) else pl.multiple_of(start, multiple)


def _t5_bucket(dist):
    n = jnp.maximum(dist, 0)
    max_exact = REL_BUCKETS // 2
    nf = jnp.maximum(n, 1).astype(F32)
    large = max_exact + (jnp.log(nf / max_exact) / math.log(REL_MAX_DISTANCE / max_exact)
                         * (REL_BUCKETS - max_exact)).astype(jnp.int32)
    large = jnp.minimum(large, REL_BUCKETS - 1)
    return jnp.where(n < max_exact, n, large)


def _lookup(bucket, table_t):
    acc = jnp.zeros(bucket.shape, F32)
    for b in range(REL_BUCKETS):
        acc = jnp.where(bucket == b, table_t[:, b:b + 1], acc)
    return acc


def _bias_tables_kernel(tab_ref, dsw_ref, moba_ref, *, n_moba_heads, seq):
    tab = tab_ref[...]
    n_heads = dsw_ref.shape[0]
    m = lax.broadcasted_iota(jnp.int32, (n_heads, 2 * DSW_BLK), 1)
    row = lax.broadcasted_iota(jnp.int32, (n_heads, 2 * DSW_BLK), 0)
    dil = jnp.where(row < DSW_HEADS_PER_GROUP, DSW_GROUPS[0][1],
                    jnp.where(row < 2 * DSW_HEADS_PER_GROUP, DSW_GROUPS[1][1], DSW_GROUPS[2][1]))
    sub = DSW_BLK - m
    vals = _lookup(_t5_bucket(sub * dil), tab)
    dsw_ref[...] = jnp.where(sub >= 0, vals * LOG2E, NEG)
    width = moba_ref.shape[1]
    t = lax.broadcasted_iota(jnp.int32, (n_moba_heads, width), 1)
    dist = t - MOBA_BLOCK
    vals = _lookup(_t5_bucket(dist), tab[:n_moba_heads])
    moba_ref[...] = jnp.where(dist >= 0, vals, NEG)


def _bias_tables(rel_bias, n_moba_heads, seq):
    n_heads = rel_bias.shape[1]
    width = seq + MOBA_BLOCK
    return pl.pallas_call(
        functools.partial(_bias_tables_kernel, n_moba_heads=n_moba_heads, seq=seq),
        out_shape=(jax.ShapeDtypeStruct((n_heads, 2 * DSW_BLK), F32),
                   jax.ShapeDtypeStruct((n_moba_heads, width), F32)),
        name="bias_tables",
    )(rel_bias.T)


def _toeplitz(u_row, rows):
    x = jnp.broadcast_to(u_row, (rows, u_row.shape[1]))
    return pltpu.roll(x, 0, 1, stride=1, stride_axis=0)


def _dsw_bias_kernel(vec_ref, out_ref):
    lane = lax.broadcasted_iota(jnp.int32, (DSW_BLK, 2 * DSW_BLK), 1)
    for h in range(DSW_HEADS_PER_GROUP):
        tile = _toeplitz(vec_ref[0, h:h + 1, :], DSW_BLK)
        out_ref[0, h, 0] = tile
        out_ref[0, h, 1] = jnp.where(lane < DSW_BLK, NEG, tile)


def _dsw_bias_tiles(dsw_vec):
    n_groups = len(DSW_GROUPS)
    hg = DSW_HEADS_PER_GROUP
    vec = dsw_vec.reshape(n_groups, hg, 2 * DSW_BLK)
    return pl.pallas_call(
        _dsw_bias_kernel,
        grid=(n_groups,),
        in_specs=[pl.BlockSpec((1, hg, 2 * DSW_BLK), lambda g: (g, 0, 0))],
        out_specs=pl.BlockSpec((1, hg, 2, DSW_BLK, 2 * DSW_BLK), lambda g: (g, 0, 0, 0, 0)),
        out_shape=jax.ShapeDtypeStruct((n_groups, hg, 2, DSW_BLK, 2 * DSW_BLK), F32),
        compiler_params=_params(1),
        name="dsw_bias_tiles",
    )(vec)


def _rmsnorm_bf16(x, gain):
    ms = jnp.mean(x * x, axis=-1, keepdims=True)
    return (x * lax.rsqrt(ms + EPS) * gain).astype(BF16)


def _qkv_kernel(x_ref, g_ref, w_ref, qg_ref, kg_ref, bd_ref, *rest, width, dilations):
    n_groups = len(dilations)
    out_refs = rest[:3 * n_groups]
    u_s = rest[3 * n_groups]
    x = x_ref[0]
    rows = x.shape[0]
    ms = jnp.mean(x * x, axis=-1, keepdims=True)
    u = x * lax.rsqrt(ms + EPS) * g_ref[...]
    n_tiles = x.shape[1] // LANES
    if any(d > 1 for d in dilations):
        for j in range(n_tiles):
            u_s[j] = u[:, j * LANES:(j + 1) * LANES]
    chunk = bd_ref.shape[0]
    for gi, d in enumerate(dilations):
        per = rows // d
        if d == 1:
            lhs = u.astype(BF16)
        else:
            lhs = jnp.concatenate(
                [jnp.concatenate([u_s[j, pl.ds(c, per, stride=d), :] for j in range(n_tiles)], axis=1)
                 for c in range(d)], axis=0).astype(BF16)
        q_ref, k_ref, v_ref = out_refs[3 * gi:3 * gi + 3]
        for part, (gain_ref, out_ref) in enumerate(((qg_ref, q_ref), (kg_ref, k_ref), (None, v_ref))):
            col = (part * n_groups + gi) * width
            y_part = jnp.dot(lhs, w_ref[:, col:col + width].astype(BF16), preferred_element_type=F32)
            for cc in range(width // chunk):
                y = y_part[:, cc * chunk:(cc + 1) * chunk]
                if gain_ref is not None:
                    msq = jnp.dot((y * y).astype(BF16), bd_ref[...], preferred_element_type=F32)
                    y = y * lax.rsqrt(msq + EPS) * gain_ref[...]
                y = y.astype(BF16)
                for c in range(d):
                    out_ref[0, c, :, cc * chunk:(cc + 1) * chunk] = y[c * per:(c + 1) * per]


def _qkv_project(x, gain, w, q_gain_row, k_gain_row, bd, dilations, tm, name):
    batch, seq, d_model = x.shape
    width = w.shape[1] // (3 * len(dilations))
    const = lambda b, t: (0, 0)
    out_specs, out_shapes = [], []
    for d in dilations:
        out_specs += [pl.BlockSpec((1, d, tm // d, width), lambda b, t: (b, 0, t, 0))] * 3
        out_shapes += [jax.ShapeDtypeStruct((batch, d, seq // d, width), BF16)] * 3
    outs = pl.pallas_call(
        functools.partial(_qkv_kernel, width=width, dilations=tuple(dilations)),
        grid=(batch, seq // tm),
        in_specs=[pl.BlockSpec((1, tm, d_model), lambda b, t: (b, t, 0)),
                  pl.BlockSpec((1, d_model), const),
                  pl.BlockSpec(w.shape, const, pipeline_mode=pl.Buffered(1)),
                  pl.BlockSpec(q_gain_row.shape, const),
                  pl.BlockSpec(k_gain_row.shape, const),
                  pl.BlockSpec(bd.shape, const)],
        out_specs=tuple(out_specs),
        out_shape=tuple(out_shapes),
        scratch_shapes=[pltpu.VMEM((d_model // LANES, tm, LANES), F32)],
        compiler_params=_params(2),
        name=name,
    )(x, gain, w, q_gain_row, k_gain_row, bd)
    return [tuple(outs[3 * i:3 * i + 3]) for i in range(len(dilations))]


def _dsw_attn_kernel(q_ref, kc_ref, vc_ref, kp_ref, vp_ref, bias_ref, o_ref, stat_ref, *,
                     tq, blocks_per_seq):
    t = pl.program_id(0)
    n_blk = tq // DSW_BLK
    lane = lax.broadcasted_iota(jnp.int32, (DSW_BLK, LANES), 1)
    low_half = lane < HEAD_DIM
    n_pairs = q_ref.shape[1] // LANES
    for qi in range(n_blk):
        rows = slice(qi * DSW_BLK, (qi + 1) * DSW_BLK)
        first = (jnp.bitwise_and(t * n_blk + qi, blocks_per_seq - 1) == 0).astype(jnp.int32)
        stat_tile = jnp.zeros((DSW_BLK, LANES), F32)
        for hp in range(n_pairs):
            cols = slice(hp * LANES, (hp + 1) * LANES)
            q2 = q_ref[rows, cols]
            if qi == 0:
                k_prev, v_prev = kp_ref[:, cols], vp_ref[:, cols]
            else:
                prev = slice((qi - 1) * DSW_BLK, qi * DSW_BLK)
                k_prev, v_prev = kc_ref[prev, cols], vc_ref[prev, cols]
            k_cat = jnp.concatenate([k_prev, kc_ref[rows, cols]], axis=0)
            v_cat = jnp.concatenate([v_prev, vc_ref[rows, cols]], axis=0)
            outs = []
            for hh in range(2):
                head_lanes = low_half if hh == 0 else jnp.logical_not(low_half)
                qm = jnp.where(head_lanes, q2, jnp.zeros_like(q2))
                s = lax.dot_general(qm, k_cat, NT_DIMS, preferred_element_type=F32)
                s = s + bias_ref[0, hp * 2 + hh, first]
                m = jnp.max(s, axis=-1, keepdims=True)
                p = jnp.exp2(s - m)
                den = jnp.sum(p, axis=-1, keepdims=True)
                outs.append(jnp.dot(p.astype(BF16), v_cat, preferred_element_type=F32))
                head = hp * 2 + hh
                stat_tile = jnp.where(lane == head, m,
                                      jnp.where(lane == DSW_HEADS_PER_GROUP + head, den, stat_tile))
            o_ref[rows, cols] = jnp.where(low_half, outs[0], outs[1])
        stat_ref[rows, :] = stat_tile


def _dsw_attention(q, k, v, bias_tiles, group, tq, blocks_per_seq):
    n_rows, width = q.shape
    assert blocks_per_seq & (blocks_per_seq - 1) == 0
    n_blk = tq // DSW_BLK
    cur = pl.BlockSpec((tq, width), lambda t: (t, 0))
    prev = pl.BlockSpec((DSW_BLK, width), lambda t: (jnp.maximum(t * n_blk - 1, 0), 0))
    hg = DSW_HEADS_PER_GROUP
    return pl.pallas_call(
        functools.partial(_dsw_attn_kernel, tq=tq, blocks_per_seq=blocks_per_seq),
        grid=(n_rows // tq,),
        in_specs=[cur, cur, cur, prev, prev,
                  pl.BlockSpec((1, hg, 2, DSW_BLK, 2 * DSW_BLK), lambda t: (group, 0, 0, 0, 0))],
        out_specs=(pl.BlockSpec((tq, width), lambda t: (t, 0)),
                   pl.BlockSpec((tq, LANES), lambda t: (t, 0))),
        out_shape=(jax.ShapeDtypeStruct((n_rows, width), F32),
                   jax.ShapeDtypeStruct((n_rows, LANES), F32)),
        compiler_params=_params(1),
        name=f"dsw_attention_g{group}",
    )(q, k, v, k, v, bias_tiles)


def _merge_wo_kernel(*refs, dilations):
    n = len(dilations)
    o_refs, l_refs = refs[:n], refs[n:2 * n]
    x_ref, wo_ref, e_ref, out_ref = refs[2 * n:2 * n + 4]
    scratch = refs[2 * n + 4:]
    rows = x_ref.shape[1]
    outs, lses = [], []
    si = 0
    for d, o_ref, l_ref in zip(dilations, o_refs, l_refs):
        if d == 1:
            outs.append(o_ref[0, 0])
            lses.append(l_ref[0, 0])
            continue
        o_s, l_s = scratch[si], scratch[si + 1]
        si += 2
        n_tiles = o_s.shape[0]
        for c in range(d):
            piece = o_ref[0, c]
            for j in range(n_tiles):
                o_s[j, pl.ds(c, rows // d, stride=d), :] = piece[:, j * LANES:(j + 1) * LANES]
            l_s[pl.ds(c, rows // d, stride=d), :] = l_ref[0, c]
        outs.append(jnp.concatenate([o_s[j] for j in range(n_tiles)], axis=1))
        lses.append(l_s[...])
    lane = lax.broadcasted_iota(jnp.int32, lses[0].shape, 1)
    mx = functools.reduce(jnp.maximum, lses)
    es = [jnp.exp2(l - mx) for l in lses]
    sums = [pltpu.roll(l, LANES - DSW_HEADS_PER_GROUP, 1) for l in lses]
    total = functools.reduce(lambda a, b: a + b, [e * s for e, s in zip(es, sums)])
    merged = jnp.zeros(outs[0].shape, F32)
    for e, o in zip(es, outs):
        wgt = jnp.where(lane < DSW_HEADS_PER_GROUP, e / total, 0.0)
        hi = wgt.astype(BF16)
        lo = (wgt - hi.astype(F32)).astype(BF16)
        spread = (jnp.dot(hi, e_ref[...], preferred_element_type=F32)
                  + jnp.dot(lo, e_ref[...], preferred_element_type=F32))
        merged = merged + spread * o
    out_ref[0] = x_ref[0] + jnp.dot(merged.astype(BF16), wo_ref[...].astype(BF16),
                                    preferred_element_type=F32)


def _merge_wo(os, lses, x, wo, dilations, tm):
    batch, seq, d_model = x.shape
    width = os[0].shape[-1]
    head_of_lane = jnp.arange(width) // HEAD_DIM
    expand = (jnp.arange(LANES)[:, None] == head_of_lane[None, :]).astype(BF16)
    const = lambda b, t: (0, 0)
    blk = lambda d, w: pl.BlockSpec((1, d, tm // d, w), lambda b, t: (b, 0, t, 0))
    x_spec = pl.BlockSpec((1, tm, d_model), lambda b, t: (b, t, 0))
    scratch = []
    for d in dilations:
        if d > 1:
            scratch += [pltpu.VMEM((width // LANES, tm, LANES), F32), pltpu.VMEM((tm, LANES), F32)]
    return pl.pallas_call(
        functools.partial(_merge_wo_kernel, dilations=tuple(dilations)),
        grid=(batch, seq // tm),
        in_specs=([blk(d, width) for d in dilations] + [blk(d, LANES) for d in dilations]
                  + [x_spec, pl.BlockSpec(wo.shape, const), pl.BlockSpec(expand.shape, const)]),
        out_specs=x_spec,
        out_shape=jax.ShapeDtypeStruct(x.shape, F32),
        scratch_shapes=scratch,
        compiler_params=_params(2),
        name="dsw_merge_wo",
    )(*os, *lses, x, wo, expand)


def _ffn_kernel(*refs, has_attn, ff_chunk):
    if has_attn:
        h_ref, a_ref, wo_ref, g_ref, w1_ref, w2_ref, out_ref = refs
        h = h_ref[...] + jnp.dot(a_ref[...], wo_ref[...].astype(BF16), preferred_element_type=F32)
    else:
        h_ref, g_ref, w1_ref, w2_ref, out_ref = refs
        h = h_ref[...]
    u = _rmsnorm_bf16(h, g_ref[...])
    acc = h
    for c in range(w1_ref.shape[2] // ff_chunk):
        cols = slice(c * ff_chunk, (c + 1) * ff_chunk)
        a = jnp.dot(u, w1_ref[0, :, cols].astype(BF16), preferred_element_type=F32)
        a = jnp.square(jnp.maximum(a, 0.0)).astype(BF16)
        acc = acc + jnp.dot(a, w2_ref[0, cols, :].astype(BF16), preferred_element_type=F32)
    out_ref[...] = acc


def _ffn(h, gain, w1, w2, layer, tm, attn=None, wo=None, name="ffn"):
    n_rows, d_model = h.shape
    const = lambda i: (0, 0)
    row_spec = pl.BlockSpec((tm, d_model), lambda i: (i, 0))
    in_specs = [row_spec]
    args = [h]
    if attn is not None:
        in_specs += [pl.BlockSpec((tm, attn.shape[1]), lambda i: (i, 0)), pl.BlockSpec(wo.shape, const)]
        args += [attn, wo]
    in_specs += [pl.BlockSpec((1, d_model), const),
                 pl.BlockSpec((1,) + w1.shape[1:], lambda i: (layer, 0, 0), pipeline_mode=pl.Buffered(1)),
                 pl.BlockSpec((1,) + w2.shape[1:], lambda i: (layer, 0, 0), pipeline_mode=pl.Buffered(1))]
    args += [gain, w1, w2]
    return pl.pallas_call(
        functools.partial(_ffn_kernel, has_attn=attn is not None, ff_chunk=1024),
        grid=(n_rows // tm,),
        in_specs=in_specs,
        out_specs=row_spec,
        out_shape=jax.ShapeDtypeStruct(h.shape, F32),
        compiler_params=_params(1),
        name=name,
    )(*args)


def _moba_kernel(q_ref, k_ref, v_ref, wn_ref, o_ref,
                 bias_s, kp_s, vt_s, qa_s, s_s, acc_s, *, n_blocks, chunk):
    b = pl.program_id(1)
    blk = MOBA_BLOCK
    seq = n_blocks * blk
    span = chunk * blk
    lane = lax.broadcasted_iota(jnp.int32, (blk, LANES), 1)
    low_lanes = lane < HEAD_DIM
    v_rows = vt_s.shape[1]
    ones_row = (lax.broadcasted_iota(jnp.int32, (v_rows - HEAD_DIM, blk), 0) == 0).astype(F32)

    @pl.when(b == 0)
    def _build_bias():
        for hh in range(2):
            for dlt in range(n_blocks):
                u = jnp.concatenate([wn_ref[0, hh:hh + 1, (dlt + 1) * blk:(dlt + 2) * blk],
                                     wn_ref[0, hh:hh + 1, dlt * blk:(dlt + 1) * blk]], axis=1)
                r = n_blocks - 1 - dlt
                bias_s[hh, r * blk:(r + 1) * blk, :] = _toeplitz(u, blk)[:, :blk] * LOG2E
            bias_s[hh, seq:, :] = jnp.full((span - blk, blk), NEG, F32)

    sub = lax.broadcasted_iota(jnp.int32, (n_blocks, LANES), 0)
    kmean = jnp.zeros((n_blocks, LANES), F32)
    kamax = []
    for j in range(n_blocks):
        rows = slice(j * blk, (j + 1) * blk)
        kj = k_ref[0, rows, :].astype(F32)
        vj_t = v_ref[0, rows, :].astype(F32).T
        kmean = jnp.where(sub == j, jnp.mean(kj, axis=0, keepdims=True), kmean)
        kamax.append(jnp.max(jnp.abs(kj), axis=0, keepdims=True))
        aux0 = jnp.logical_or(lane == HEAD_DIM + j, lane == HEAD_DIM + n_blocks).astype(F32)
        aux1 = jnp.logical_or(lane == j, lane == n_blocks).astype(F32)
        kp_s[0, rows, :] = jnp.where(low_lanes, kj, aux0).astype(BF16)
        kp_s[1, rows, :] = jnp.where(low_lanes, aux1, kj).astype(BF16)
        for hh in range(2):
            vt_s[hh, :, rows] = jnp.concatenate(
                [vj_t[hh * HEAD_DIM:(hh + 1) * HEAD_DIM], ones_row], axis=0).astype(BF16)
    km_hi = kmean.astype(BF16)
    km_lo = (kmean - km_hi.astype(F32)).astype(BF16)
    kamax_upto = [kamax[0]]
    for j in range(1, n_blocks):
        kamax_upto.append(jnp.maximum(kamax_upto[-1], kamax[j]))
    pad_rows = jnp.zeros((6, LANES), F32)
    bias_max = [jnp.max(wn_ref[0, hh:hh + 1, blk:], axis=1, keepdims=True) * LOG2E for hh in range(2)]
    bias_zero = [wn_ref[0, hh:hh + 1, blk:blk + 1] * LOG2E for hh in range(2)]

    blk_row = lax.broadcasted_iota(jnp.int32, (n_blocks, blk), 0)
    blk_row_f = blk_row.astype(F32)
    gap = jnp.full((1, blk), -jnp.inf, F32)
    for n in range(n_blocks):
        rows = slice(n * blk, (n + 1) * blk)
        q2 = q_ref[0, rows, :]
        q_abs = jnp.abs(q2)
        k_bounds = jnp.concatenate([kamax_upto[n], kamax[n], pad_rows], axis=0).astype(BF16)
        for hh in range(2):
            own = low_lanes if hh == 0 else jnp.logical_not(low_lanes)
            qm = jnp.where(own, q2, jnp.zeros_like(q2))
            gate = (lax.dot_general(km_hi, qm, NT_DIMS, preferred_element_type=F32)
                    + lax.dot_general(km_lo, qm, NT_DIMS, preferred_element_type=F32))
            gate = jnp.where(blk_row < n, gate, -jnp.inf)
            chosen = blk_row == n
            for _ in range(min(MOBA_TOPK, n)):
                best = jnp.max(gate, axis=0, keepdims=True)
                cand = jnp.where(gate == best, blk_row_f, float(n_blocks))
                first = jnp.min(cand, axis=0, keepdims=True)
                pick = blk_row_f == first
                chosen = jnp.logical_or(chosen, pick)
                gate = jnp.where(pick, -jnp.inf, gate)
            pen_t = jnp.where(chosen, 0.0, NEG)
            sums = lax.dot_general(k_bounds, jnp.where(own, q_abs, jnp.zeros_like(q_abs)),
                                   NT_DIMS, preferred_element_type=F32)
            bound = sums[0:1] + bias_max[hh] + 1.0
            gap = jnp.maximum(gap, bound - (bias_zero[hh] - sums[1:2]))
            pad_lo = HEAD_DIM if hh == 0 else 0
            pieces = [jnp.zeros((pad_lo, blk), F32)] if pad_lo else []
            pieces += [pen_t, -bound, jnp.zeros((LANES - pad_lo - n_blocks - 1, blk), F32)]
            pen = jnp.concatenate(pieces, axis=0).T
            qa_s[hh, rows, :] = jnp.where(own, q2, pen.astype(BF16))

    def emit(n, width):
        cols = width * blk
        o_t = jnp.concatenate([acc_s[hh, :HEAD_DIM, :cols] / acc_s[hh, HEAD_DIM:HEAD_DIM + 1, :cols]
                               for hh in range(2)], axis=0)
        o_ref[0, pl.ds(_aligned(n * blk, blk), cols), :] = o_t.T.astype(o_ref.dtype)

    def raw_scores(hh, n, c, width):
        q_aug = qa_s[hh, pl.ds(_aligned(n * blk, blk), width * blk), :]
        s_t = lax.dot_general(kp_s[hh, c * span:(c + 1) * span, :], q_aug, NT_DIMS,
                              preferred_element_type=F32)
        bias = [bias_s[hh, pl.ds(_aligned((n_blocks - 1 - n - i) * blk + c * span, blk), span), :]
                for i in range(width)]
        return s_t + (bias[0] if width == 1 else jnp.concatenate(bias, axis=1))

    acc_s[...] = jnp.ones(acc_s.shape, F32)
    n_groups = n_blocks // chunk
    bound_is_tight = jnp.max(gap) < EXP_RANGE

    @pl.when(bound_is_tight)
    def _single_pass():
        def q_pair(n, n_chunks):
            emit(jnp.maximum(n - 2, 0), 2)
            for hh in range(2):
                acc = None
                for c in range(n_chunks):
                    p_t = jnp.exp2(raw_scores(hh, n, c, 2)).astype(BF16)
                    part = jnp.dot(vt_s[hh, :, c * span:(c + 1) * span], p_t,
                                   preferred_element_type=F32)
                    acc = part if acc is None else acc + part
                acc_s[hh] = acc

        for g in range(n_groups):
            lax.fori_loop(0, chunk // 2,
                          lambda t, c, g=g: (q_pair(g * chunk + 2 * t, g + 1), c)[1], 0)
        emit(n_blocks - 2, 2)

    @pl.when(jnp.logical_not(bound_is_tight))
    def _two_pass():
        def scores(hh, n, n_chunks):
            m = None
            for c in range(n_chunks):
                s_t = raw_scores(hh, n, c, 1)
                s_s[hh, c * span:(c + 1) * span, :] = s_t
                cm = jnp.max(s_t, axis=0, keepdims=True)
                m = cm if m is None else jnp.maximum(m, cm)
            return m

        def weighted_values(hh, m, n_chunks):
            n_keys = n_chunks * span
            p_t = jnp.exp2(s_s[hh, :n_keys, :] - m).astype(BF16)
            return jnp.dot(vt_s[hh, :, :n_keys], p_t, preferred_element_type=F32)

        def step(n, m0, n_chunks, next_chunks):
            emit(jnp.maximum(n - 1, 0), 1)
            acc0 = weighted_values(0, m0, n_chunks)
            m1 = scores(1, n, n_chunks)
            m0_next = scores(0, n + 1, next_chunks) if next_chunks else m0
            acc1 = weighted_values(1, m1, n_chunks)
            acc_s[0, :, :blk] = acc0
            acc_s[1, :, :blk] = acc1
            return m0_next

        m0 = scores(0, 0, 1)
        for g in range(n_groups):
            m0 = lax.fori_loop(0, chunk - 1,
                               lambda t, m, g=g: step(g * chunk + t, m, g + 1, g + 1), m0)
            m0 = step(g * chunk + chunk - 1, m0, g + 1, g + 2 if g + 1 < n_groups else 0)
        emit(n_blocks - 1, 1)


def _moba_attention(q, k, v, moba_vec, batch, seq, chunk):
    n_rows, width = q.shape
    n_pairs = width // LANES
    n_blocks = seq // MOBA_BLOCK
    assert n_blocks % chunk == 0
    wn = moba_vec.reshape(n_pairs, 2, moba_vec.shape[1])
    spec = pl.BlockSpec((1, seq, LANES), lambda p, b: (b, 0, p))
    shape3 = (batch, seq, width)
    out = pl.pallas_call(
        functools.partial(_moba_kernel, n_blocks=n_blocks, chunk=chunk),
        grid=(n_pairs, batch),
        in_specs=[spec, spec, spec, pl.BlockSpec((1, 2, wn.shape[2]), lambda p, b: (p, 0, 0))],
        out_specs=spec,
        out_shape=jax.ShapeDtypeStruct(shape3, BF16),
        scratch_shapes=[pltpu.VMEM((2, seq + (chunk - 1) * MOBA_BLOCK, MOBA_BLOCK), F32),
                        pltpu.VMEM((2, seq, LANES), BF16),
                        pltpu.VMEM((2, HEAD_DIM + 16, seq), BF16),
                        pltpu.VMEM((2, seq, LANES), BF16),
                        pltpu.VMEM((2, seq, MOBA_BLOCK), F32),
                        pltpu.VMEM((2, HEAD_DIM + 16, 2 * MOBA_BLOCK), F32)],
        compiler_params=_params(2),
        name="moba_attention",
    )(q.reshape(shape3), k.reshape(shape3), v.reshape(shape3), wn)
    return out.reshape(n_rows, width)


def _head_mean_matrix(size):
    idx = jnp.arange(size) // HEAD_DIM
    return (idx[:, None] == idx[None, :]).astype(BF16) * (1.0 / HEAD_DIM)


def _gain_row(gain, size, scale):
    return (jnp.tile(gain.astype(F32), size // HEAD_DIM) * scale).reshape(1, size)


def kernel(x, rel_bias, norm_mix, norm_ffn, a_w_qkv, a_q_gain, a_k_gain, a_w_o,
           b_w_qkv, b_q_gain, b_k_gain, b_w_o, ffn_w1, ffn_w2):
    batch, seq, d_model = x.shape
    n_rows = batch * seq
    n_groups = len(DSW_GROUPS)
    dilations = [d for _, d in DSW_GROUPS]
    gw = DSW_HEADS_PER_GROUP * HEAD_DIM
    moba_heads = b_w_o.shape[1] // HEAD_DIM
    bd = _head_mean_matrix(2 * LANES)

    dsw_vec, moba_vec = _bias_tables(rel_bias, moba_heads, seq)
    dsw_tiles = _dsw_bias_tiles(dsw_vec)

    qkv = _qkv_project(x, norm_mix[0].reshape(1, d_model), a_w_qkv[0],
                       _gain_row(a_q_gain[0], 2 * LANES, SCALE * LOG2E), _gain_row(a_k_gain[0], 2 * LANES, 1.0),
                       bd, dilations, tm=512, name="dsw_qkv")
    os, lses = [], []
    for g, d in enumerate(dilations):
        q_g, k_g, v_g = (a.reshape(n_rows, gw) for a in qkv[g])
        o_g, lse_g = _dsw_attention(q_g, k_g, v_g, dsw_tiles, g, tq=512,
                                    blocks_per_seq=seq // d // DSW_BLK)
        os.append(o_g.reshape(batch, d, seq // d, gw))
        lses.append(lse_g.reshape(batch, d, seq // d, LANES))
    h = _merge_wo(os, lses, x, a_w_o[0], dilations, tm=512)
    h = h.reshape(n_rows, d_model)
    w1, w2 = ffn_w1, ffn_w2
    h = _ffn(h, norm_ffn[0].reshape(1, d_model), w1, w2, 0, tm=512, name="ffn0")

    width = moba_heads * HEAD_DIM
    (q, k, v), = _qkv_project(
        h.reshape(batch, seq, d_model), norm_mix[1].reshape(1, d_model), b_w_qkv[0],
        _gain_row(b_q_gain[0], 2 * LANES, SCALE * LOG2E), _gain_row(b_k_gain[0], 2 * LANES, 1.0),
        bd, [1], tm=512, name="moba_qkv")
    q, k, v = (a.reshape(n_rows, width) for a in (q, k, v))
    attn = _moba_attention(q, k, v, moba_vec, batch, seq, chunk=4)
    h = _ffn(h, norm_ffn[1].reshape(1, d_model), w1, w2, 1,
             tm=512, attn=attn, wo=b_w_o[0], name="wo_ffn1")
    return h.reshape(batch, seq, d_model)
```

```python
import functools
import math

import jax
import jax.numpy as jnp
from jax import lax
from jax.experimental import pallas as pl
from jax.experimental.pallas import tpu as pltpu

HEAD_DIM = 64
LANES = 128
DSW_GROUPS = ((128, 1), (512, 4), (2048, 16))
DSW_BLK = 128
DSW_HEADS_PER_GROUP = 8
MOBA_BLOCK = 256
MOBA_TOPK = 3
REL_BUCKETS = 32
REL_MAX_DISTANCE = 2048
EPS = 1e-6
NEG = -1e30
SCALE = HEAD_DIM ** -0.5
LOG2E = 1.4426950408889634
EXP_RANGE = 100.0
VMEM_LIMIT_BYTES = 56 * 1024 * 1024

F32 = jnp.float32
BF16 = jnp.bfloat16
NT_DIMS = (((1,), (1,)), ((), ()))


def _params(n_axes):
    return pltpu.CompilerParams(dimension_semantics=("arbitrary",) * n_axes,
                                vmem_limit_bytes=VMEM_LIMIT_BYTES)


def _aligned(start, multiple):
    return start if isinstance(start, int) else pl.multiple_of(start, multiple)


def _t5_bucket(dist):
    n = jnp.maximum(dist, 0)
    max_exact = REL_BUCKETS // 2
    nf = jnp.maximum(n, 1).astype(F32)
    large = max_exact + (jnp.log(nf / max_exact) / math.log(REL_MAX_DISTANCE / max_exact)
                         * (REL_BUCKETS - max_exact)).astype(jnp.int32)
    large = jnp.minimum(large, REL_BUCKETS - 1)
    return jnp.where(n < max_exact, n, large)


def _lookup(bucket, table_t):
    acc = jnp.zeros(bucket.shape, F32)
    for b in range(REL_BUCKETS):
        acc = jnp.where(bucket == b, table_t[:, b:b + 1], acc)
    return acc


def _bias_tables_kernel(tab_ref, dsw_ref, moba_ref, *, n_moba_heads, seq):
    tab = tab_ref[...]
    n_heads = dsw_ref.shape[0]
    m = lax.broadcasted_iota(jnp.int32, (n_heads, 2 * DSW_BLK), 1)
    row = lax.broadcasted_iota(jnp.int32, (n_heads, 2 * DSW_BLK), 0)
    dil = jnp.where(row < DSW_HEADS_PER_GROUP, DSW_GROUPS[0][1],
                    jnp.where(row < 2 * DSW_HEADS_PER_GROUP, DSW_GROUPS[1][1], DSW_GROUPS[2][1]))
    sub = DSW_BLK - m
    vals = _lookup(_t5_bucket(sub * dil), tab)
    dsw_ref[...] = jnp.where(sub >= 0, vals * LOG2E, NEG)
    width = moba_ref.shape[1]
    t = lax.broadcasted_iota(jnp.int32, (n_moba_heads, width), 1)
    dist = t - MOBA_BLOCK
    vals = _lookup(_t5_bucket(dist), tab[:n_moba_heads])
    moba_ref[...] = jnp.where(dist >= 0, vals, NEG)


def _bias_tables(rel_bias, n_moba_heads, seq):
    n_heads = rel_bias.shape[1]
    width = seq + MOBA_BLOCK
    return pl.pallas_call(
        functools.partial(_bias_tables_kernel, n_moba_heads=n_moba_heads, seq=seq),
        out_shape=(jax.ShapeDtypeStruct((n_heads, 2 * DSW_BLK), F32),
                   jax.ShapeDtypeStruct((n_moba_heads, width), F32)),
        name="bias_tables",
    )(rel_bias.T)


def _toeplitz(u_row, rows):
    x = jnp.broadcast_to(u_row, (rows, u_row.shape[1]))
    return pltpu.roll(x, 0, 1, stride=1, stride_axis=0)


def _dsw_bias_kernel(vec_ref, out_ref):
    lane = lax.broadcasted_iota(jnp.int32, (DSW_BLK, 2 * DSW_BLK), 1)
    for h in range(DSW_HEADS_PER_GROUP):
        tile = _toeplitz(vec_ref[0, h:h + 1, :], DSW_BLK)
        out_ref[0, h, 0] = tile
        out_ref[0, h, 1] = jnp.where(lane < DSW_BLK, NEG, tile)


def _dsw_bias_tiles(dsw_vec):
    n_groups = len(DSW_GROUPS)
    hg = DSW_HEADS_PER_GROUP
    vec = dsw_vec.reshape(n_groups, hg, 2 * DSW_BLK)
    return pl.pallas_call(
        _dsw_bias_kernel,
        grid=(n_groups,),
        in_specs=[pl.BlockSpec((1, hg, 2 * DSW_BLK), lambda g: (g, 0, 0))],
        out_specs=pl.BlockSpec((1, hg, 2, DSW_BLK, 2 * DSW_BLK), lambda g: (g, 0, 0, 0, 0)),
        out_shape=jax.ShapeDtypeStruct((n_groups, hg, 2, DSW_BLK, 2 * DSW_BLK), F32),
        compiler_params=_params(1),
        name="dsw_bias_tiles",
    )(vec)


def _rmsnorm_bf16(x, gain):
    ms = jnp.mean(x * x, axis=-1, keepdims=True)
    return (x * lax.rsqrt(ms + EPS) * gain).astype(BF16)


def _qkv_kernel(x_ref, g_ref, w_ref, qg_ref, kg_ref, bd_ref, *rest, width, dilations):
    n_groups = len(dilations)
    out_refs = rest[:3 * n_groups]
    u_s = rest[3 * n_groups]
    x = x_ref[0]
    rows = x.shape[0]
    ms = jnp.mean(x * x, axis=-1, keepdims=True)
    u = x * lax.rsqrt(ms + EPS) * g_ref[...]
    n_tiles = x.shape[1] // LANES
    if any(d > 1 for d in dilations):
        for j in range(n_tiles):
            u_s[j] = u[:, j * LANES:(j + 1) * LANES]
    chunk = bd_ref.shape[0]
    for gi, d in enumerate(dilations):
        per = rows // d
        if d == 1:
            lhs = u.astype(BF16)
        else:
            lhs = jnp.concatenate(
                [jnp.concatenate([u_s[j, pl.ds(c, per, stride=d), :] for j in range(n_tiles)], axis=1)
                 for c in range(d)], axis=0).astype(BF16)
        q_ref, k_ref, v_ref = out_refs[3 * gi:3 * gi + 3]
        for part, (gain_ref, out_ref) in enumerate(((qg_ref, q_ref), (kg_ref, k_ref), (None, v_ref))):
            col = (part * n_groups + gi) * width
            y_part = jnp.dot(lhs, w_ref[:, col:col + width].astype(BF16), preferred_element_type=F32)
            for cc in range(width // chunk):
                y = y_part[:, cc * chunk:(cc + 1) * chunk]
                if gain_ref is not None:
                    msq = jnp.dot((y * y).astype(BF16), bd_ref[...], preferred_element_type=F32)
                    y = y * lax.rsqrt(msq + EPS) * gain_ref[...]
                y = y.astype(BF16)
                for c in range(d):
                    out_ref[0, c, :, cc * chunk:(cc + 1) * chunk] = y[c * per:(c + 1) * per]


def _qkv_project(x, gain, w, q_gain_row, k_gain_row, bd, dilations, tm, name):
    batch, seq, d_model = x.shape
    width = w.shape[1] // (3 * len(dilations))
    const = lambda b, t: (0, 0)
    out_specs, out_shapes = [], []
    for d in dilations:
        out_specs += [pl.BlockSpec((1, d, tm // d, width), lambda b, t: (b, 0, t, 0))] * 3
        out_shapes += [jax.ShapeDtypeStruct((batch, d, seq // d, width), BF16)] * 3
    outs = pl.pallas_call(
        functools.partial(_qkv_kernel, width=width, dilations=tuple(dilations)),
        grid=(batch, seq // tm),
        in_specs=[pl.BlockSpec((1, tm, d_model), lambda b, t: (b, t, 0)),
                  pl.BlockSpec((1, d_model), const),
                  pl.BlockSpec(w.shape, const, pipeline_mode=pl.Buffered(1)),
                  pl.BlockSpec(q_gain_row.shape, const),
                  pl.BlockSpec(k_gain_row.shape, const),
                  pl.BlockSpec(bd.shape, const)],
        out_specs=tuple(out_specs),
        out_shape=tuple(out_shapes),
        scratch_shapes=[pltpu.VMEM((d_model // LANES, tm, LANES), F32)],
        compiler_params=_params(2),
        name=name,
    )(x, gain, w, q_gain_row, k_gain_row, bd)
    return [tuple(outs[3 * i:3 * i + 3]) for i in range(len(dilations))]


def _dsw_attn_kernel(q_ref, kc_ref, vc_ref, kp_ref, vp_ref, bias_ref, o_ref, stat_ref, *,
                     tq, blocks_per_seq):
    t = pl.program_id(0)
    n_blk = tq // DSW_BLK
    lane = lax.broadcasted_iota(jnp.int32, (DSW_BLK, LANES), 1)
    low_half = lane < HEAD_DIM
    n_pairs = q_ref.shape[1] // LANES
    for qi in range(n_blk):
        rows = slice(qi * DSW_BLK, (qi + 1) * DSW_BLK)
        first = (jnp.bitwise_and(t * n_blk + qi, blocks_per_seq - 1) == 0).astype(jnp.int32)
        stat_tile = jnp.zeros((DSW_BLK, LANES), F32)
        for hp in range(n_pairs):
            cols = slice(hp * LANES, (hp + 1) * LANES)
            q2 = q_ref[rows, cols]
            if qi == 0:
                k_prev, v_prev = kp_ref[:, cols], vp_ref[:, cols]
            else:
                prev = slice((qi - 1) * DSW_BLK, qi * DSW_BLK)
                k_prev, v_prev = kc_ref[prev, cols], vc_ref[prev, cols]
            k_cat = jnp.concatenate([k_prev, kc_ref[rows, cols]], axis=0)
            v_cat = jnp.concatenate([v_prev, vc_ref[rows, cols]], axis=0)
            outs = []
            for hh in range(2):
                head_lanes = low_half if hh == 0 else jnp.logical_not(low_half)
                qm = jnp.where(head_lanes, q2, jnp.zeros_like(q2))
                s = lax.dot_general(qm, k_cat, NT_DIMS, preferred_element_type=F32)
                s = s + bias_ref[0, hp * 2 + hh, first]
                m = jnp.max(s, axis=-1, keepdims=True)
                p = jnp.exp2(s - m)
                den = jnp.sum(p, axis=-1, keepdims=True)
                outs.append(jnp.dot(p.astype(BF16), v_cat, preferred_element_type=F32))
                head = hp * 2 + hh
                stat_tile = jnp.where(lane == head, m,
                                      jnp.where(lane == DSW_HEADS_PER_GROUP + head, den, stat_tile))
            o_ref[rows, cols] = jnp.where(low_half, outs[0], outs[1]).astype(o_ref.dtype)
        stat_ref[rows, :] = stat_tile


def _dsw_attention(q, k, v, bias_tiles, group, tq, blocks_per_seq):
    n_rows, width = q.shape
    assert blocks_per_seq & (blocks_per_seq - 1) == 0
    n_blk = tq // DSW_BLK
    cur = pl.BlockSpec((tq, width), lambda t: (t, 0))
    prev = pl.BlockSpec((DSW_BLK, width), lambda t: (jnp.maximum(t * n_blk - 1, 0), 0))
    hg = DSW_HEADS_PER_GROUP
    return pl.pallas_call(
        functools.partial(_dsw_attn_kernel, tq=tq, blocks_per_seq=blocks_per_seq),
        grid=(n_rows // tq,),
        in_specs=[cur, cur, cur, prev, prev,
                  pl.BlockSpec((1, hg, 2, DSW_BLK, 2 * DSW_BLK), lambda t: (group, 0, 0, 0, 0))],
        out_specs=(pl.BlockSpec((tq, width), lambda t: (t, 0)),
                   pl.BlockSpec((tq, LANES), lambda t: (t, 0))),
        out_shape=(jax.ShapeDtypeStruct((n_rows, width), BF16),
                   jax.ShapeDtypeStruct((n_rows, LANES), F32)),
        compiler_params=_params(1),
        name=f"dsw_attention_g{group}",
    )(q, k, v, k, v, bias_tiles)


def _merge_wo_kernel(*refs, dilations):
    n = len(dilations)
    o_refs, l_refs = refs[:n], refs[n:2 * n]
    x_ref, wo_ref, e_ref, out_ref = refs[2 * n:2 * n + 4]
    scratch = refs[2 * n + 4:]
    rows = x_ref.shape[1]
    outs, lses = [], []
    si = 0
    for d, o_ref, l_ref in zip(dilations, o_refs, l_refs):
        if d == 1:
            outs.append(o_ref[0, 0].astype(F32))
            lses.append(l_ref[0, 0])
            continue
        o_s, l_s = scratch[si], scratch[si + 1]
        si += 2
        n_tiles = o_s.shape[0]
        for c in range(d):
            piece = o_ref[0, c].astype(F32)
            for j in range(n_tiles):
                o_s[j, pl.ds(c, rows // d, stride=d), :] = piece[:, j * LANES:(j + 1) * LANES]
            l_s[pl.ds(c, rows // d, stride=d), :] = l_ref[0, c]
        outs.append(jnp.concatenate([o_s[j] for j in range(n_tiles)], axis=1))
        lses.append(l_s[...])
    lane = lax.broadcasted_iota(jnp.int32, lses[0].shape, 1)
    mx = functools.reduce(jnp.maximum, lses)
    es = [jnp.exp2(l - mx) for l in lses]
    sums = [pltpu.roll(l, LANES - DSW_HEADS_PER_GROUP, 1) for l in lses]
    total = functools.reduce(lambda a, b: a + b, [e * s for e, s in zip(es, sums)])
    merged = jnp.zeros(outs[0].shape, F32)
    for e, o in zip(es, outs):
        wgt = jnp.where(lane < DSW_HEADS_PER_GROUP, e / total, 0.0)
        hi = wgt.astype(BF16)
        lo = (wgt - hi.astype(F32)).astype(BF16)
        spread = (jnp.dot(hi, e_ref[...], preferred_element_type=F32)
                  + jnp.dot(lo, e_ref[...], preferred_element_type=F32))
        merged = merged + spread * o
    out_ref[0] = x_ref[0] + jnp.dot(merged.astype(BF16), wo_ref[...].astype(BF16),
                                    preferred_element_type=F32)


def _merge_wo(os, lses, x, wo, dilations, tm):
    batch, seq, d_model = x.shape
    width = os[0].shape[-1]
    head_of_lane = jnp.arange(width) // HEAD_DIM
    expand = (jnp.arange(LANES)[:, None] == head_of_lane[None, :]).astype(BF16)
    const = lambda b, t: (0, 0)
    blk = lambda d, w: pl.BlockSpec((1, d, tm // d, w), lambda b, t: (b, 0, t, 0))
    x_spec = pl.BlockSpec((1, tm, d_model), lambda b, t: (b, t, 0))
    scratch = []
    for d in dilations:
        if d > 1:
            scratch += [pltpu.VMEM((width // LANES, tm, LANES), F32), pltpu.VMEM((tm, LANES), F32)]
    return pl.pallas_call(
        functools.partial(_merge_wo_kernel, dilations=tuple(dilations)),
        grid=(batch, seq // tm),
        in_specs=([blk(d, width) for d in dilations] + [blk(d, LANES) for d in dilations]
                  + [x_spec, pl.BlockSpec(wo.shape, const), pl.BlockSpec(expand.shape, const)]),
        out_specs=x_spec,
        out_shape=jax.ShapeDtypeStruct(x.shape, F32),
        scratch_shapes=scratch,
        compiler_params=_params(2),
        name="dsw_merge_wo",
    )(*os, *lses, x, wo, expand)


def _ffn_kernel(*refs, has_attn, ff_chunk):
    if has_attn:
        h_ref, a_ref, wo_ref, g_ref, w1_ref, w2_ref, out_ref = refs
        h = h_ref[...] + jnp.dot(a_ref[...], wo_ref[...].astype(BF16), preferred_element_type=F32)
    else:
        h_ref, g_ref, w1_ref, w2_ref, out_ref = refs
        h = h_ref[...]
    u = _rmsnorm_bf16(h, g_ref[...])
    acc = h
    for c in range(w1_ref.shape[2] // ff_chunk):
        cols = slice(c * ff_chunk, (c + 1) * ff_chunk)
        a = jnp.dot(u, w1_ref[0, :, cols].astype(BF16), preferred_element_type=F32)
        a = jnp.square(jnp.maximum(a, 0.0)).astype(BF16)
        acc = acc + jnp.dot(a, w2_ref[0, cols, :].astype(BF16), preferred_element_type=F32)
    out_ref[...] = acc


def _ffn(h, gain, w1, w2, layer, tm, attn=None, wo=None, name="ffn"):
    n_rows, d_model = h.shape
    const = lambda i: (0, 0)
    row_spec = pl.BlockSpec((tm, d_model), lambda i: (i, 0))
    in_specs = [row_spec]
    args = [h]
    if attn is not None:
        in_specs += [pl.BlockSpec((tm, attn.shape[1]), lambda i: (i, 0)), pl.BlockSpec(wo.shape, const)]
        args += [attn, wo]
    in_specs += [pl.BlockSpec((1, d_model), const),
                 pl.BlockSpec((1,) + w1.shape[1:], lambda i: (layer, 0, 0), pipeline_mode=pl.Buffered(1)),
                 pl.BlockSpec((1,) + w2.shape[1:], lambda i: (layer, 0, 0), pipeline_mode=pl.Buffered(1))]
    args += [gain, w1, w2]
    return pl.pallas_call(
        functools.partial(_ffn_kernel, has_attn=attn is not None, ff_chunk=1024),
        grid=(n_rows // tm,),
        in_specs=in_specs,
        out_specs=row_spec,
        out_shape=jax.ShapeDtypeStruct(h.shape, F32),
        compiler_params=_params(1),
        name=name,
    )(*args)


def _moba_kernel(q_ref, k_ref, v_ref, wn_ref, o_ref,
                 bias_s, kp_s, vt_s, qa_s, s_s, acc_s, *, n_blocks, chunk):
    b = pl.program_id(1)
    blk = MOBA_BLOCK
    seq = n_blocks * blk
    span = chunk * blk
    lane = lax.broadcasted_iota(jnp.int32, (blk, LANES), 1)
    low_lanes = lane < HEAD_DIM
    v_rows = vt_s.shape[1]
    ones_row = (lax.broadcasted_iota(jnp.int32, (v_rows - HEAD_DIM, blk), 0) == 0).astype(F32)

    @pl.when(b == 0)
    def _build_bias():
        for hh in range(2):
            for dlt in range(n_blocks):
                u = jnp.concatenate([wn_ref[0, hh:hh + 1, (dlt + 1) * blk:(dlt + 2) * blk],
                                     wn_ref[0, hh:hh + 1, dlt * blk:(dlt + 1) * blk]], axis=1)
                r = n_blocks - 1 - dlt
                bias_s[hh, r * blk:(r + 1) * blk, :] = _toeplitz(u, blk)[:, :blk] * LOG2E
            bias_s[hh, seq:, :] = jnp.full((span - blk, blk), NEG, F32)

    sub = lax.broadcasted_iota(jnp.int32, (n_blocks, LANES), 0)
    kmean = jnp.zeros((n_blocks, LANES), F32)
    kamax = []
    for j in range(n_blocks):
        rows = slice(j * blk, (j + 1) * blk)
        kj = k_ref[0, rows, :].astype(F32)
        vj_t = v_ref[0, rows, :].astype(F32).T
        kmean = jnp.where(sub == j, jnp.mean(kj, axis=0, keepdims=True), kmean)
        kamax.append(jnp.max(jnp.abs(kj), axis=0, keepdims=True))
        aux0 = jnp.logical_or(lane == HEAD_DIM + j, lane == HEAD_DIM + n_blocks).astype(F32)
        aux1 = jnp.logical_or(lane == j, lane == n_blocks).astype(F32)
        kp_s[0, rows, :] = jnp.where(low_lanes, kj, aux0).astype(BF16)
        kp_s[1, rows, :] = jnp.where(low_lanes, aux1, kj).astype(BF16)
        for hh in range(2):
            vt_s[hh, :, rows] = jnp.concatenate(
                [vj_t[hh * HEAD_DIM:(hh + 1) * HEAD_DIM], ones_row], axis=0).astype(BF16)
    km_hi = kmean.astype(BF16)
    km_lo = (kmean - km_hi.astype(F32)).astype(BF16)
    kamax_upto = [kamax[0]]
    for j in range(1, n_blocks):
        kamax_upto.append(jnp.maximum(kamax_upto[-1], kamax[j]))
    pad_rows = jnp.zeros((6, LANES), F32)
    bias_max = [jnp.max(wn_ref[0, hh:hh + 1, blk:], axis=1, keepdims=True) * LOG2E for hh in range(2)]
    bias_zero = [wn_ref[0, hh:hh + 1, blk:blk + 1] * LOG2E for hh in range(2)]

    blk_row = lax.broadcasted_iota(jnp.int32, (n_blocks, blk), 0)
    blk_row_f = blk_row.astype(F32)
    gap = jnp.full((1, blk), -jnp.inf, F32)
    for n in range(n_blocks):
        rows = slice(n * blk, (n + 1) * blk)
        q2 = q_ref[0, rows, :]
        q_abs = jnp.abs(q2)
        k_bounds = jnp.concatenate([kamax_upto[n], kamax[n], pad_rows], axis=0).astype(BF16)
        for hh in range(2):
            own = low_lanes if hh == 0 else jnp.logical_not(low_lanes)
            qm = jnp.where(own, q2, jnp.zeros_like(q2))
            gate = (lax.dot_general(km_hi, qm, NT_DIMS, preferred_element_type=F32)
                    + lax.dot_general(km_lo, qm, NT_DIMS, preferred_element_type=F32))
            gate = jnp.where(blk_row < n, gate, -jnp.inf)
            chosen = blk_row == n
            for _ in range(min(MOBA_TOPK, n)):
                best = jnp.max(gate, axis=0, keepdims=True)
                cand = jnp.where(gate == best, blk_row_f, float(n_blocks))
                first = jnp.min(cand, axis=0, keepdims=True)
                pick = blk_row_f == first
                chosen = jnp.logical_or(chosen, pick)
                gate = jnp.where(pick, -jnp.inf, gate)
            pen_t = jnp.where(chosen, 0.0, NEG)
            sums = lax.dot_general(k_bounds, jnp.where(own, q_abs, jnp.zeros_like(q_abs)),
                                   NT_DIMS, preferred_element_type=F32)
            bound = sums[0:1] + bias_max[hh] + 1.0
            gap = jnp.maximum(gap, bound - (bias_zero[hh] - sums[1:2]))
            pad_lo = HEAD_DIM if hh == 0 else 0
            pieces = [jnp.zeros((pad_lo, blk), F32)] if pad_lo else []
            pieces += [pen_t, -bound, jnp.zeros((LANES - pad_lo - n_blocks - 1, blk), F32)]
            pen = jnp.concatenate(pieces, axis=0).T
            qa_s[hh, rows, :] = jnp.where(own, q2, pen.astype(BF16))

    def emit(n, width):
        cols = width * blk
        o_t = jnp.concatenate([acc_s[hh, :HEAD_DIM, :cols] / acc_s[hh, HEAD_DIM:HEAD_DIM + 1, :cols]
                               for hh in range(2)], axis=0)
        o_ref[0, pl.ds(_aligned(n * blk, blk), cols), :] = o_t.T.astype(o_ref.dtype)

    def raw_scores(hh, n, c, width):
        q_aug = qa_s[hh, pl.ds(_aligned(n * blk, blk), width * blk), :]
        s_t = lax.dot_general(kp_s[hh, c * span:(c + 1) * span, :], q_aug, NT_DIMS,
                              preferred_element_type=F32)
        bias = [bias_s[hh, pl.ds(_aligned((n_blocks - 1 - n - i) * blk + c * span, blk), span), :]
                for i in range(width)]
        return s_t + (bias[0] if width == 1 else jnp.concatenate(bias, axis=1))

    acc_s[...] = jnp.ones(acc_s.shape, F32)
    n_groups = n_blocks // chunk
    bound_is_tight = jnp.max(gap) < EXP_RANGE

    @pl.when(bound_is_tight)
    def _single_pass():
        def q_pair(n, n_chunks):
            emit(jnp.maximum(n - 2, 0), 2)
            for hh in range(2):
                acc = None
                for c in range(n_chunks):
                    p_t = jnp.exp2(raw_scores(hh, n, c, 2)).astype(BF16)
                    part = jnp.dot(vt_s[hh, :, c * span:(c + 1) * span], p_t,
                                   preferred_element_type=F32)
                    acc = part if acc is None else acc + part
                acc_s[hh] = acc

        for g in range(n_groups):
            lax.fori_loop(0, chunk // 2,
                          lambda t, c, g=g: (q_pair(g * chunk + 2 * t, g + 1), c)[1], 0)
        emit(n_blocks - 2, 2)

    @pl.when(jnp.logical_not(bound_is_tight))
    def _two_pass():
        def scores(hh, n, n_chunks):
            m = None
            for c in range(n_chunks):
                s_t = raw_scores(hh, n, c, 1)
                s_s[hh, c * span:(c + 1) * span, :] = s_t
                cm = jnp.max(s_t, axis=0, keepdims=True)
                m = cm if m is None else jnp.maximum(m, cm)
            return m

        def weighted_values(hh, m, n_chunks):
            n_keys = n_chunks * span
            p_t = jnp.exp2(s_s[hh, :n_keys, :] - m).astype(BF16)
            return jnp.dot(vt_s[hh, :, :n_keys], p_t, preferred_element_type=F32)

        def step(n, m0, n_chunks, next_chunks):
            emit(jnp.maximum(n - 1, 0), 1)
            acc0 = weighted_values(0, m0, n_chunks)
            m1 = scores(1, n, n_chunks)
            m0_next = scores(0, n + 1, next_chunks) if next_chunks else m0
            acc1 = weighted_values(1, m1, n_chunks)
            acc_s[0, :, :blk] = acc0
            acc_s[1, :, :blk] = acc1
            return m0_next

        m0 = scores(0, 0, 1)
        for g in range(n_groups):
            m0 = lax.fori_loop(0, chunk - 1,
                               lambda t, m, g=g: step(g * chunk + t, m, g + 1, g + 1), m0)
            m0 = step(g * chunk + chunk - 1, m0, g + 1, g + 2 if g + 1 < n_groups else 0)
        emit(n_blocks - 1, 1)


def _moba_attention(q, k, v, moba_vec, batch, seq, chunk):
    n_rows, width = q.shape
    n_pairs = width // LANES
    n_blocks = seq // MOBA_BLOCK
    assert n_blocks % chunk == 0
    wn = moba_vec.reshape(n_pairs, 2, moba_vec.shape[1])
    spec = pl.BlockSpec((1, seq, LANES), lambda p, b: (b, 0, p))
    shape3 = (batch, seq, width)
    out = pl.pallas_call(
        functools.partial(_moba_kernel, n_blocks=n_blocks, chunk=chunk),
        grid=(n_pairs, batch),
        in_specs=[spec, spec, spec, pl.BlockSpec((1, 2, wn.shape[2]), lambda p, b: (p, 0, 0))],
        out_specs=spec,
        out_shape=jax.ShapeDtypeStruct(shape3, BF16),
        scratch_shapes=[pltpu.VMEM((2, seq + (chunk - 1) * MOBA_BLOCK, MOBA_BLOCK), F32),
                        pltpu.VMEM((2, seq, LANES), BF16),
                        pltpu.VMEM((2, HEAD_DIM + 16, seq), BF16),
                        pltpu.VMEM((2, seq, LANES), BF16),
                        pltpu.VMEM((2, seq, MOBA_BLOCK), F32),
                        pltpu.VMEM((2, HEAD_DIM + 16, 2 * MOBA_BLOCK), F32)],
        compiler_params=_params(2),
        name="moba_attention",
    )(q.reshape(shape3), k.reshape(shape3), v.reshape(shape3), wn)
    return out.reshape(n_rows, width)


def _head_mean_matrix(size):
    idx = jnp.arange(size) // HEAD_DIM
    return (idx[:, None] == idx[None, :]).astype(BF16) * (1.0 / HEAD_DIM)


def _gain_row(gain, size, scale):
    return (jnp.tile(gain.astype(F32), size // HEAD_DIM) * scale).reshape(1, size)


def kernel(x, rel_bias, norm_mix, norm_ffn, a_w_qkv, a_q_gain, a_k_gain, a_w_o,
           b_w_qkv, b_q_gain, b_k_gain, b_w_o, ffn_w1, ffn_w2):
    batch, seq, d_model = x.shape
    n_rows = batch * seq
    n_groups = len(DSW_GROUPS)
    dilations = [d for _, d in DSW_GROUPS]
    gw = DSW_HEADS_PER_GROUP * HEAD_DIM
    moba_heads = b_w_o.shape[1] // HEAD_DIM
    bd = _head_mean_matrix(2 * LANES)

    dsw_vec, moba_vec = _bias_tables(rel_bias, moba_heads, seq)
    dsw_tiles = _dsw_bias_tiles(dsw_vec)

    qkv = _qkv_project(x, norm_mix[0].reshape(1, d_model), a_w_qkv[0],
                       _gain_row(a_q_gain[0], 2 * LANES, SCALE * LOG2E), _gain_row(a_k_gain[0], 2 * LANES, 1.0),
                       bd, dilations, tm=512, name="dsw_qkv")
    os, lses = [], []
    for g, d in enumerate(dilations):
        q_g, k_g, v_g = (a.reshape(n_rows, gw) for a in qkv[g])
        o_g, lse_g = _dsw_attention(q_g, k_g, v_g, dsw_tiles, g, tq=1024,
                                    blocks_per_seq=seq // d // DSW_BLK)
        os.append(o_g.reshape(batch, d, seq // d, gw))
        lses.append(lse_g.reshape(batch, d, seq // d, LANES))
    h = _merge_wo(os, lses, x, a_w_o[0], dilations, tm=512)
    h = h.reshape(n_rows, d_model)
    w1, w2 = ffn_w1, ffn_w2
    h = _ffn(h, norm_ffn[0].reshape(1, d_model), w1, w2, 0, tm=512, name="ffn0")

    width = moba_heads * HEAD_DIM
    (q, k, v), = _qkv_project(
        h.reshape(batch, seq, d_model), norm_mix[1].reshape(1, d_model), b_w_qkv[0],
        _gain_row(b_q_gain[0], 2 * LANES, SCALE * LOG2E), _gain_row(b_k_gain[0], 2 * LANES, 1.0),
        bd, [1], tm=512, name="moba_qkv")
    q, k, v = (a.reshape(n_rows, width) for a in (q, k, v))
    attn = _moba_attention(q, k, v, moba_vec, batch, seq, chunk=4)
    h = _ffn(h, norm_ffn[1].reshape(1, d_model), w1, w2, 1,
             tm=512, attn=attn, wo=b_w_o[0], name="wo_ffn1")
    return h.reshape(batch, seq, d_model)
```

```python
import functools
import math

import jax
import jax.numpy as jnp
from jax import lax
from jax.experimental import pallas as pl
from jax.experimental.pallas import tpu as pltpu

HEAD_DIM = 64
LANES = 128
DSW_GROUPS = ((128, 1), (512, 4), (2048, 16))
DSW_BLK = 128
DSW_HEADS_PER_GROUP = 8
MOBA_BLOCK = 256
MOBA_TOPK = 3
REL_BUCKETS = 32
REL_MAX_DISTANCE = 2048
EPS = 1e-6
NEG = -1e30
SCALE = HEAD_DIM ** -0.5
LOG2E = 1.4426950408889634
EXP_RANGE = 100.0
VMEM_LIMIT_BYTES = 56 * 1024 * 1024

F32 = jnp.float32
BF16 = jnp.bfloat16
NT_DIMS = (((1,), (1,)), ((), ()))


def _params(n_axes):
    return pltpu.CompilerParams(dimension_semantics=("arbitrary",) * n_axes,
                                vmem_limit_bytes=VMEM_LIMIT_BYTES)


def _aligned(start, multiple):
    return start if isinstance(start, int) else pl.multiple_of(start, multiple)


def _t5_bucket(dist):
    n = jnp.maximum(dist, 0)
    max_exact = REL_BUCKETS // 2
    nf = jnp.maximum(n, 1).astype(F32)
    large = max_exact + (jnp.log(nf / max_exact) / math.log(REL_MAX_DISTANCE / max_exact)
                         * (REL_BUCKETS - max_exact)).astype(jnp.int32)
    large = jnp.minimum(large, REL_BUCKETS - 1)
    return jnp.where(n < max_exact, n, large)


def _lookup(bucket, table_t):
    acc = jnp.zeros(bucket.shape, F32)
    for b in range(REL_BUCKETS):
        acc = jnp.where(bucket == b, table_t[:, b:b + 1], acc)
    return acc


def _bias_tables_kernel(tab_ref, dsw_ref, moba_ref, *, n_moba_heads, seq):
    tab = tab_ref[...]
    n_heads = dsw_ref.shape[0]
    m = lax.broadcasted_iota(jnp.int32, (n_heads, 2 * DSW_BLK), 1)
    row = lax.broadcasted_iota(jnp.int32, (n_heads, 2 * DSW_BLK), 0)
    dil = jnp.where(row < DSW_HEADS_PER_GROUP, DSW_GROUPS[0][1],
                    jnp.where(row < 2 * DSW_HEADS_PER_GROUP, DSW_GROUPS[1][1], DSW_GROUPS[2][1]))
    sub = DSW_BLK - m
    vals = _lookup(_t5_bucket(sub * dil), tab)
    dsw_ref[...] = jnp.where(sub >= 0, vals * LOG2E, NEG)
    width = moba_ref.shape[1]
    t = lax.broadcasted_iota(jnp.int32, (n_moba_heads, width), 1)
    dist = t - MOBA_BLOCK
    vals = _lookup(_t5_bucket(dist), tab[:n_moba_heads])
    moba_ref[...] = jnp.where(dist >= 0, vals, NEG)


def _bias_tables(rel_bias, n_moba_heads, seq):
    n_heads = rel_bias.shape[1]
    width = seq + MOBA_BLOCK
    return pl.pallas_call(
        functools.partial(_bias_tables_kernel, n_moba_heads=n_moba_heads, seq=seq),
        out_shape=(jax.ShapeDtypeStruct((n_heads, 2 * DSW_BLK), F32),
                   jax.ShapeDtypeStruct((n_moba_heads, width), F32)),
        name="bias_tables",
    )(rel_bias.T)


def _toeplitz(u_row, rows):
    x = jnp.broadcast_to(u_row, (rows, u_row.shape[1]))
    return pltpu.roll(x, 0, 1, stride=1, stride_axis=0)


def _dsw_bias_kernel(vec_ref, out_ref):
    lane = lax.broadcasted_iota(jnp.int32, (DSW_BLK, 2 * DSW_BLK), 1)
    for h in range(DSW_HEADS_PER_GROUP):
        tile = _toeplitz(vec_ref[0, h:h + 1, :], DSW_BLK)
        out_ref[0, h, 0] = tile
        out_ref[0, h, 1] = jnp.where(lane < DSW_BLK, NEG, tile)


def _dsw_bias_tiles(dsw_vec):
    n_groups = len(DSW_GROUPS)
    hg = DSW_HEADS_PER_GROUP
    vec = dsw_vec.reshape(n_groups, hg, 2 * DSW_BLK)
    return pl.pallas_call(
        _dsw_bias_kernel,
        grid=(n_groups,),
        in_specs=[pl.BlockSpec((1, hg, 2 * DSW_BLK), lambda g: (g, 0, 0))],
        out_specs=pl.BlockSpec((1, hg, 2, DSW_BLK, 2 * DSW_BLK), lambda g: (g, 0, 0, 0, 0)),
        out_shape=jax.ShapeDtypeStruct((n_groups, hg, 2, DSW_BLK, 2 * DSW_BLK), F32),
        compiler_params=_params(1),
        name="dsw_bias_tiles",
    )(vec)


def _rmsnorm_bf16(x, gain):
    ms = jnp.mean(x * x, axis=-1, keepdims=True)
    return (x * lax.rsqrt(ms + EPS) * gain).astype(BF16)


def _qkv_kernel(x_ref, g_ref, w_ref, qg_ref, kg_ref, *rest, width, dilations):
    n_groups = len(dilations)
    out_refs = rest[:3 * n_groups]
    u_s = rest[3 * n_groups]
    x = x_ref[0]
    rows = x.shape[0]
    ms = jnp.mean(x * x, axis=-1, keepdims=True)
    u = x * lax.rsqrt(ms + EPS) * g_ref[...]
    n_tiles = x.shape[1] // LANES
    if any(d > 1 for d in dilations):
        for j in range(n_tiles):
            u_s[j] = u[:, j * LANES:(j + 1) * LANES]
    low_lanes = lax.broadcasted_iota(jnp.int32, (rows, LANES), 1) < HEAD_DIM
    for gi, d in enumerate(dilations):
        per = rows // d
        if d == 1:
            lhs = u.astype(BF16)
        else:
            lhs = jnp.concatenate(
                [jnp.concatenate([u_s[j, pl.ds(c, per, stride=d), :] for j in range(n_tiles)], axis=1)
                 for c in range(d)], axis=0).astype(BF16)
        q_ref, k_ref, v_ref = out_refs[3 * gi:3 * gi + 3]
        for part, (gain_ref, out_ref) in enumerate(((qg_ref, q_ref), (kg_ref, k_ref), (None, v_ref))):
            col = (part * n_groups + gi) * width
            y_part = jnp.dot(lhs, w_ref[:, col:col + width].astype(BF16), preferred_element_type=F32)
            for cc in range(width // LANES):
                y = y_part[:, cc * LANES:(cc + 1) * LANES]
                if gain_ref is not None:
                    sq = y * y
                    both = jnp.sum(sq, axis=-1, keepdims=True)
                    low = jnp.sum(jnp.where(low_lanes, sq, 0.0), axis=-1, keepdims=True)
                    msq = jnp.where(low_lanes, low, both - low) * (1.0 / HEAD_DIM)
                    y = y * lax.rsqrt(msq + EPS) * gain_ref[...]
                y = y.astype(BF16)
                for c in range(d):
                    out_ref[0, c, :, cc * LANES:(cc + 1) * LANES] = y[c * per:(c + 1) * per]


def _qkv_project(x, gain, w, q_gain_row, k_gain_row, dilations, tm, name):
    batch, seq, d_model = x.shape
    width = w.shape[1] // (3 * len(dilations))
    const = lambda b, t: (0, 0)
    out_specs, out_shapes = [], []
    for d in dilations:
        out_specs += [pl.BlockSpec((1, d, tm // d, width), lambda b, t: (b, 0, t, 0))] * 3
        out_shapes += [jax.ShapeDtypeStruct((batch, d, seq // d, width), BF16)] * 3
    outs = pl.pallas_call(
        functools.partial(_qkv_kernel, width=width, dilations=tuple(dilations)),
        grid=(batch, seq // tm),
        in_specs=[pl.BlockSpec((1, tm, d_model), lambda b, t: (b, t, 0)),
                  pl.BlockSpec((1, d_model), const),
                  pl.BlockSpec(w.shape, const, pipeline_mode=pl.Buffered(1)),
                  pl.BlockSpec(q_gain_row.shape, const),
                  pl.BlockSpec(k_gain_row.shape, const)],
        out_specs=tuple(out_specs),
        out_shape=tuple(out_shapes),
        scratch_shapes=[pltpu.VMEM((d_model // LANES, tm, LANES), F32)],
        compiler_params=_params(2),
        name=name,
    )(x, gain, w, q_gain_row, k_gain_row)
    return [tuple(outs[3 * i:3 * i + 3]) for i in range(len(dilations))]


def _dsw_attn_kernel(q_ref, kc_ref, vc_ref, kp_ref, vp_ref, bias_ref, o_ref, stat_ref, *,
                     tq, blocks_per_seq):
    t = pl.program_id(0)
    n_blk = tq // DSW_BLK
    lane = lax.broadcasted_iota(jnp.int32, (DSW_BLK, LANES), 1)
    low_half = lane < HEAD_DIM
    n_pairs = q_ref.shape[1] // LANES
    for qi in range(n_blk):
        rows = slice(qi * DSW_BLK, (qi + 1) * DSW_BLK)
        first = (jnp.bitwise_and(t * n_blk + qi, blocks_per_seq - 1) == 0).astype(jnp.int32)
        stat_tile = jnp.zeros((DSW_BLK, LANES), F32)
        for hp in range(n_pairs):
            cols = slice(hp * LANES, (hp + 1) * LANES)
            q2 = q_ref[rows, cols]
            if qi == 0:
                k_prev, v_prev = kp_ref[:, cols], vp_ref[:, cols]
            else:
                prev = slice((qi - 1) * DSW_BLK, qi * DSW_BLK)
                k_prev, v_prev = kc_ref[prev, cols], vc_ref[prev, cols]
            k_cat = jnp.concatenate([k_prev, kc_ref[rows, cols]], axis=0)
            v_cat = jnp.concatenate([v_prev, vc_ref[rows, cols]], axis=0)
            outs = []
            for hh in range(2):
                head_lanes = low_half if hh == 0 else jnp.logical_not(low_half)
                qm = jnp.where(head_lanes, q2, jnp.zeros_like(q2))
                s = lax.dot_general(qm, k_cat, NT_DIMS, preferred_element_type=F32)
                s = s + bias_ref[0, hp * 2 + hh, first]
                m = jnp.max(s, axis=-1, keepdims=True)
                p = jnp.exp2(s - m)
                den = jnp.sum(p, axis=-1, keepdims=True)
                outs.append(jnp.dot(p.astype(BF16), v_cat, preferred_element_type=F32))
                head = hp * 2 + hh
                stat_tile = jnp.where(lane == head, m,
                                      jnp.where(lane == DSW_HEADS_PER_GROUP + head, den, stat_tile))
            o_ref[rows, cols] = jnp.where(low_half, outs[0], outs[1]).astype(o_ref.dtype)
        stat_ref[rows, :] = stat_tile


def _dsw_attention(q, k, v, bias_tiles, group, tq, blocks_per_seq):
    n_rows, width = q.shape
    assert blocks_per_seq & (blocks_per_seq - 1) == 0
    n_blk = tq // DSW_BLK
    cur = pl.BlockSpec((tq, width), lambda t: (t, 0))
    prev = pl.BlockSpec((DSW_BLK, width), lambda t: (jnp.maximum(t * n_blk - 1, 0), 0))
    hg = DSW_HEADS_PER_GROUP
    return pl.pallas_call(
        functools.partial(_dsw_attn_kernel, tq=tq, blocks_per_seq=blocks_per_seq),
        grid=(n_rows // tq,),
        in_specs=[cur, cur, cur, prev, prev,
                  pl.BlockSpec((1, hg, 2, DSW_BLK, 2 * DSW_BLK), lambda t: (group, 0, 0, 0, 0))],
        out_specs=(pl.BlockSpec((tq, width), lambda t: (t, 0)),
                   pl.BlockSpec((tq, LANES), lambda t: (t, 0))),
        out_shape=(jax.ShapeDtypeStruct((n_rows, width), BF16),
                   jax.ShapeDtypeStruct((n_rows, LANES), F32)),
        compiler_params=_params(1),
        name=f"dsw_attention_g{group}",
    )(q, k, v, k, v, bias_tiles)


def _merge_wo_kernel(*refs, dilations):
    n = len(dilations)
    o_refs, l_refs = refs[:n], refs[n:2 * n]
    x_ref, wo_ref, e_ref, out_ref = refs[2 * n:2 * n + 4]
    scratch = refs[2 * n + 4:]
    rows = x_ref.shape[1]
    outs, lses = [], []
    si = 0
    for d, o_ref, l_ref in zip(dilations, o_refs, l_refs):
        if d == 1:
            outs.append(o_ref[0, 0].astype(F32))
            lses.append(l_ref[0, 0])
            continue
        o_s, l_s = scratch[si], scratch[si + 1]
        si += 2
        n_tiles = o_s.shape[0]
        for c in range(d):
            piece = o_ref[0, c].astype(F32)
            for j in range(n_tiles):
                o_s[j, pl.ds(c, rows // d, stride=d), :] = piece[:, j * LANES:(j + 1) * LANES]
            l_s[pl.ds(c, rows // d, stride=d), :] = l_ref[0, c]
        outs.append(jnp.concatenate([o_s[j] for j in range(n_tiles)], axis=1))
        lses.append(l_s[...])
    lane = lax.broadcasted_iota(jnp.int32, lses[0].shape, 1)
    mx = functools.reduce(jnp.maximum, lses)
    es = [jnp.exp2(l - mx) for l in lses]
    sums = [pltpu.roll(l, LANES - DSW_HEADS_PER_GROUP, 1) for l in lses]
    total = functools.reduce(lambda a, b: a + b, [e * s for e, s in zip(es, sums)])
    merged = jnp.zeros(outs[0].shape, F32)
    for e, o in zip(es, outs):
        wgt = jnp.where(lane < DSW_HEADS_PER_GROUP, e / total, 0.0)
        hi = wgt.astype(BF16)
        lo = (wgt - hi.astype(F32)).astype(BF16)
        spread = (jnp.dot(hi, e_ref[...], preferred_element_type=F32)
                  + jnp.dot(lo, e_ref[...], preferred_element_type=F32))
        merged = merged + spread * o
    out_ref[0] = x_ref[0] + jnp.dot(merged.astype(BF16), wo_ref[...].astype(BF16),
                                    preferred_element_type=F32)


def _merge_wo(os, lses, x, wo, dilations, tm):
    batch, seq, d_model = x.shape
    width = os[0].shape[-1]
    head_of_lane = jnp.arange(width) // HEAD_DIM
    expand = (jnp.arange(LANES)[:, None] == head_of_lane[None, :]).astype(BF16)
    const = lambda b, t: (0, 0)
    blk = lambda d, w: pl.BlockSpec((1, d, tm // d, w), lambda b, t: (b, 0, t, 0))
    x_spec = pl.BlockSpec((1, tm, d_model), lambda b, t: (b, t, 0))
    scratch = []
    for d in dilations:
        if d > 1:
            scratch += [pltpu.VMEM((width // LANES, tm, LANES), F32), pltpu.VMEM((tm, LANES), F32)]
    return pl.pallas_call(
        functools.partial(_merge_wo_kernel, dilations=tuple(dilations)),
        grid=(batch, seq // tm),
        in_specs=([blk(d, width) for d in dilations] + [blk(d, LANES) for d in dilations]
                  + [x_spec, pl.BlockSpec(wo.shape, const), pl.BlockSpec(expand.shape, const)]),
        out_specs=x_spec,
        out_shape=jax.ShapeDtypeStruct(x.shape, F32),
        scratch_shapes=scratch,
        compiler_params=_params(2),
        name="dsw_merge_wo",
    )(*os, *lses, x, wo, expand)


def _ffn_kernel(*refs, has_attn, ff_chunk):
    if has_attn:
        h_ref, a_ref, wo_ref, g_ref, w1_ref, w2_ref, out_ref = refs
        h = h_ref[...] + jnp.dot(a_ref[...], wo_ref[...].astype(BF16), preferred_element_type=F32)
    else:
        h_ref, g_ref, w1_ref, w2_ref, out_ref = refs
        h = h_ref[...]
    u = _rmsnorm_bf16(h, g_ref[...])
    acc = h
    for c in range(w1_ref.shape[2] // ff_chunk):
        cols = slice(c * ff_chunk, (c + 1) * ff_chunk)
        a = jnp.dot(u, w1_ref[0, :, cols].astype(BF16), preferred_element_type=F32)
        a = jnp.square(jnp.maximum(a, 0.0)).astype(BF16)
        acc = acc + jnp.dot(a, w2_ref[0, cols, :].astype(BF16), preferred_element_type=F32)
    out_ref[...] = acc


def _ffn(h, gain, w1, w2, layer, tm, attn=None, wo=None, name="ffn"):
    n_rows, d_model = h.shape
    const = lambda i: (0, 0)
    row_spec = pl.BlockSpec((tm, d_model), lambda i: (i, 0))
    in_specs = [row_spec]
    args = [h]
    if attn is not None:
        in_specs += [pl.BlockSpec((tm, attn.shape[1]), lambda i: (i, 0)), pl.BlockSpec(wo.shape, const)]
        args += [attn, wo]
    in_specs += [pl.BlockSpec((1, d_model), const),
                 pl.BlockSpec((1,) + w1.shape[1:], lambda i: (layer, 0, 0), pipeline_mode=pl.Buffered(1)),
                 pl.BlockSpec((1,) + w2.shape[1:], lambda i: (layer, 0, 0), pipeline_mode=pl.Buffered(1))]
    args += [gain, w1, w2]
    return pl.pallas_call(
        functools.partial(_ffn_kernel, has_attn=attn is not None, ff_chunk=1024),
        grid=(n_rows // tm,),
        in_specs=in_specs,
        out_specs=row_spec,
        out_shape=jax.ShapeDtypeStruct(h.shape, F32),
        compiler_params=_params(1),
        name=name,
    )(*args)


def _moba_kernel(q_ref, k_ref, v_ref, wn_ref, o_ref,
                 bias_s, kp_s, vt_s, qa_s, s_s, acc_s, *, n_blocks, chunk):
    b = pl.program_id(1)
    blk = MOBA_BLOCK
    seq = n_blocks * blk
    span = chunk * blk
    lane = lax.broadcasted_iota(jnp.int32, (blk, LANES), 1)
    low_lanes = lane < HEAD_DIM
    v_rows = vt_s.shape[1]
    ones_row = (lax.broadcasted_iota(jnp.int32, (v_rows - HEAD_DIM, blk), 0) == 0).astype(F32)

    @pl.when(b == 0)
    def _build_bias():
        for hh in range(2):
            for dlt in range(n_blocks):
                u = jnp.concatenate([wn_ref[0, hh:hh + 1, (dlt + 1) * blk:(dlt + 2) * blk],
                                     wn_ref[0, hh:hh + 1, dlt * blk:(dlt + 1) * blk]], axis=1)
                r = n_blocks - 1 - dlt
                bias_s[hh, r * blk:(r + 1) * blk, :] = _toeplitz(u, blk)[:, :blk] * LOG2E
            bias_s[hh, seq:, :] = jnp.full((span - blk, blk), NEG, F32)

    sub = lax.broadcasted_iota(jnp.int32, (n_blocks, LANES), 0)
    kmean = jnp.zeros((n_blocks, LANES), F32)
    kamax = []
    for j in range(n_blocks):
        rows = slice(j * blk, (j + 1) * blk)
        kj = k_ref[0, rows, :].astype(F32)
        vj_t = v_ref[0, rows, :].astype(F32).T
        kmean = jnp.where(sub == j, jnp.mean(kj, axis=0, keepdims=True), kmean)
        kamax.append(jnp.max(jnp.abs(kj), axis=0, keepdims=True))
        aux0 = jnp.logical_or(lane == HEAD_DIM + j, lane == HEAD_DIM + n_blocks).astype(F32)
        aux1 = jnp.logical_or(lane == j, lane == n_blocks).astype(F32)
        kp_s[0, rows, :] = jnp.where(low_lanes, kj, aux0).astype(BF16)
        kp_s[1, rows, :] = jnp.where(low_lanes, aux1, kj).astype(BF16)
        for hh in range(2):
            vt_s[hh, :, rows] = jnp.concatenate(
                [vj_t[hh * HEAD_DIM:(hh + 1) * HEAD_DIM], ones_row], axis=0).astype(BF16)
    km_hi = kmean.astype(BF16)
    km_lo = (kmean - km_hi.astype(F32)).astype(BF16)
    kamax_upto = [kamax[0]]
    for j in range(1, n_blocks):
        kamax_upto.append(jnp.maximum(kamax_upto[-1], kamax[j]))
    pad_rows = jnp.zeros((6, LANES), F32)
    bias_max = [jnp.max(wn_ref[0, hh:hh + 1, blk:], axis=1, keepdims=True) * LOG2E for hh in range(2)]
    bias_zero = [wn_ref[0, hh:hh + 1, blk:blk + 1] * LOG2E for hh in range(2)]

    blk_row = lax.broadcasted_iota(jnp.int32, (n_blocks, blk), 0)
    blk_row_f = blk_row.astype(F32)
    gap = jnp.full((1, blk), -jnp.inf, F32)
    for n in range(n_blocks):
        rows = slice(n * blk, (n + 1) * blk)
        q2 = q_ref[0, rows, :]
        q_abs = jnp.abs(q2)
        k_bounds = jnp.concatenate([kamax_upto[n], kamax[n], pad_rows], axis=0).astype(BF16)
        for hh in range(2):
            own = low_lanes if hh == 0 else jnp.logical_not(low_lanes)
            qm = jnp.where(own, q2, jnp.zeros_like(q2))
            gate = (lax.dot_general(km_hi, qm, NT_DIMS, preferred_element_type=F32)
                    + lax.dot_general(km_lo, qm, NT_DIMS, preferred_element_type=F32))
            gate = jnp.where(blk_row < n, gate, -jnp.inf)
            chosen = blk_row == n
            for _ in range(min(MOBA_TOPK, n)):
                best = jnp.max(gate, axis=0, keepdims=True)
                cand = jnp.where(gate == best, blk_row_f, float(n_blocks))
                first = jnp.min(cand, axis=0, keepdims=True)
                pick = blk_row_f == first
                chosen = jnp.logical_or(chosen, pick)
                gate = jnp.where(pick, -jnp.inf, gate)
            pen_t = jnp.where(chosen, 0.0, NEG)
            sums = lax.dot_general(k_bounds, jnp.where(own, q_abs, jnp.zeros_like(q_abs)),
                                   NT_DIMS, preferred_element_type=F32)
            bound = sums[0:1] + bias_max[hh] + 1.0
            gap = jnp.maximum(gap, bound - (bias_zero[hh] - sums[1:2]))
            pad_lo = HEAD_DIM if hh == 0 else 0
            pieces = [jnp.zeros((pad_lo, blk), F32)] if pad_lo else []
            pieces += [pen_t, -bound, jnp.zeros((LANES - pad_lo - n_blocks - 1, blk), F32)]
            pen = jnp.concatenate(pieces, axis=0).T
            qa_s[hh, rows, :] = jnp.where(own, q2, pen.astype(BF16))

    def emit(n, width):
        cols = width * blk
        o_t = jnp.concatenate([acc_s[hh, :HEAD_DIM, :cols] / acc_s[hh, HEAD_DIM:HEAD_DIM + 1, :cols]
                               for hh in range(2)], axis=0)
        o_ref[0, pl.ds(_aligned(n * blk, blk), cols), :] = o_t.T.astype(o_ref.dtype)

    def raw_scores(hh, n, c, width):
        q_aug = qa_s[hh, pl.ds(_aligned(n * blk, blk), width * blk), :]
        s_t = lax.dot_general(kp_s[hh, c * span:(c + 1) * span, :], q_aug, NT_DIMS,
                              preferred_element_type=F32)
        bias = [bias_s[hh, pl.ds(_aligned((n_blocks - 1 - n - i) * blk + c * span, blk), span), :]
                for i in range(width)]
        return s_t + (bias[0] if width == 1 else jnp.concatenate(bias, axis=1))

    acc_s[...] = jnp.ones(acc_s.shape, F32)
    n_groups = n_blocks // chunk
    bound_is_tight = jnp.max(gap) < EXP_RANGE

    @pl.when(bound_is_tight)
    def _single_pass():
        def q_pair(n, n_chunks):
            emit(jnp.maximum(n - 2, 0), 2)
            for hh in range(2):
                acc = None
                for c in range(n_chunks):
                    p_t = jnp.exp2(raw_scores(hh, n, c, 2)).astype(BF16)
                    part = jnp.dot(vt_s[hh, :, c * span:(c + 1) * span], p_t,
                                   preferred_element_type=F32)
                    acc = part if acc is None else acc + part
                acc_s[hh] = acc

        for g in range(n_groups):
            lax.fori_loop(0, chunk // 2,
                          lambda t, c, g=g: (q_pair(g * chunk + 2 * t, g + 1), c)[1], 0)
        emit(n_blocks - 2, 2)

    @pl.when(jnp.logical_not(bound_is_tight))
    def _two_pass():
        def scores(hh, n, n_chunks):
            m = None
            for c in range(n_chunks):
                s_t = raw_scores(hh, n, c, 1)
                s_s[hh, c * span:(c + 1) * span, :] = s_t
                cm = jnp.max(s_t, axis=0, keepdims=True)
                m = cm if m is None else jnp.maximum(m, cm)
            return m

        def weighted_values(hh, m, n_chunks):
            n_keys = n_chunks * span
            p_t = jnp.exp2(s_s[hh, :n_keys, :] - m).astype(BF16)
            return jnp.dot(vt_s[hh, :, :n_keys], p_t, preferred_element_type=F32)

        def step(n, m0, n_chunks, next_chunks):
            emit(jnp.maximum(n - 1, 0), 1)
            acc0 = weighted_values(0, m0, n_chunks)
            m1 = scores(1, n, n_chunks)
            m0_next = scores(0, n + 1, next_chunks) if next_chunks else m0
            acc1 = weighted_values(1, m1, n_chunks)
            acc_s[0, :, :blk] = acc0
            acc_s[1, :, :blk] = acc1
            return m0_next

        m0 = scores(0, 0, 1)
        for g in range(n_groups):
            m0 = lax.fori_loop(0, chunk - 1,
                               lambda t, m, g=g: step(g * chunk + t, m, g + 1, g + 1), m0)
            m0 = step(g * chunk + chunk - 1, m0, g + 1, g + 2 if g + 1 < n_groups else 0)
        emit(n_blocks - 1, 1)


def _moba_attention(q, k, v, moba_vec, batch, seq, chunk):
    n_rows, width = q.shape
    n_pairs = width // LANES
    n_blocks = seq // MOBA_BLOCK
    assert n_blocks % chunk == 0
    wn = moba_vec.reshape(n_pairs, 2, moba_vec.shape[1])
    spec = pl.BlockSpec((1, seq, LANES), lambda p, b: (b, 0, p))
    shape3 = (batch, seq, width)
    out = pl.pallas_call(
        functools.partial(_moba_kernel, n_blocks=n_blocks, chunk=chunk),
        grid=(n_pairs, batch),
        in_specs=[spec, spec, spec, pl.BlockSpec((1, 2, wn.shape[2]), lambda p, b: (p, 0, 0))],
        out_specs=spec,
        out_shape=jax.ShapeDtypeStruct(shape3, BF16),
        scratch_shapes=[pltpu.VMEM((2, seq + (chunk - 1) * MOBA_BLOCK, MOBA_BLOCK), F32),
                        pltpu.VMEM((2, seq, LANES), BF16),
                        pltpu.VMEM((2, HEAD_DIM + 16, seq), BF16),
                        pltpu.VMEM((2, seq, LANES), BF16),
                        pltpu.VMEM((2, seq, MOBA_BLOCK), F32),
                        pltpu.VMEM((2, HEAD_DIM + 16, 2 * MOBA_BLOCK), F32)],
        compiler_params=_params(2),
        name="moba_attention",
    )(q.reshape(shape3), k.reshape(shape3), v.reshape(shape3), wn)
    return out.reshape(n_rows, width)


def _gain_row(gain, size, scale):
    return (jnp.tile(gain.astype(F32), size // HEAD_DIM) * scale).reshape(1, size)


def kernel(x, rel_bias, norm_mix, norm_ffn, a_w_qkv, a_q_gain, a_k_gain, a_w_o,
           b_w_qkv, b_q_gain, b_k_gain, b_w_o, ffn_w1, ffn_w2):
    batch, seq, d_model = x.shape
    n_rows = batch * seq
    n_groups = len(DSW_GROUPS)
    dilations = [d for _, d in DSW_GROUPS]
    gw = DSW_HEADS_PER_GROUP * HEAD_DIM
    moba_heads = b_w_o.shape[1] // HEAD_DIM

    dsw_vec, moba_vec = _bias_tables(rel_bias, moba_heads, seq)
    dsw_tiles = _dsw_bias_tiles(dsw_vec)

    qkv = _qkv_project(x, norm_mix[0].reshape(1, d_model), a_w_qkv[0],
                       _gain_row(a_q_gain[0], LANES, SCALE * LOG2E), _gain_row(a_k_gain[0], LANES, 1.0),
                       dilations, tm=512, name="dsw_qkv")
    os, lses = [], []
    for g, d in enumerate(dilations):
        q_g, k_g, v_g = (a.reshape(n_rows, gw) for a in qkv[g])
        o_g, lse_g = _dsw_attention(q_g, k_g, v_g, dsw_tiles, g, tq=1024,
                                    blocks_per_seq=seq // d // DSW_BLK)
        os.append(o_g.reshape(batch, d, seq // d, gw))
        lses.append(lse_g.reshape(batch, d, seq // d, LANES))
    h = _merge_wo(os, lses, x, a_w_o[0], dilations, tm=512)
    h = h.reshape(n_rows, d_model)
    w1, w2 = ffn_w1, ffn_w2
    h = _ffn(h, norm_ffn[0].reshape(1, d_model), w1, w2, 0, tm=512, name="ffn0")

    width = moba_heads * HEAD_DIM
    (q, k, v), = _qkv_project(
        h.reshape(batch, seq, d_model), norm_mix[1].reshape(1, d_model), b_w_qkv[0],
        _gain_row(b_q_gain[0], LANES, SCALE * LOG2E), _gain_row(b_k_gain[0], LANES, 1.0),
        [1], tm=512, name="moba_qkv")
    q, k, v = (a.reshape(n_rows, width) for a in (q, k, v))
    attn = _moba_attention(q, k, v, moba_vec, batch, seq, chunk=4)
    h = _ffn(h, norm_ffn[1].reshape(1, d_model), w1, w2, 1,
             tm=512, attn=attn, wo=b_w_o[0], name="wo_ffn1")
    return h.reshape(batch, seq, d_model)
```

```python
import functools
import math

import jax
import jax.numpy as jnp
from jax import lax
from jax.experimental import pallas as pl
from jax.experimental.pallas import tpu as pltpu

HEAD_DIM = 64
LANES = 128
DSW_GROUPS = ((128, 1), (512, 4), (2048, 16))
DSW_BLK = 128
DSW_HEADS_PER_GROUP = 8
MOBA_BLOCK = 256
MOBA_TOPK = 3
REL_BUCKETS = 32
REL_MAX_DISTANCE = 2048
EPS = 1e-6
NEG = -1e30
SCALE = HEAD_DIM ** -0.5
LOG2E = 1.4426950408889634
EXP_RANGE = 100.0
VMEM_LIMIT_BYTES = 56 * 1024 * 1024

F32 = jnp.float32
BF16 = jnp.bfloat16
NT_DIMS = (((1,), (1,)), ((), ()))


def _params(n_axes):
    return pltpu.CompilerParams(dimension_semantics=("arbitrary",) * n_axes,
                                vmem_limit_bytes=VMEM_LIMIT_BYTES)


def _aligned(start, multiple):
    return start if isinstance(start, int) else pl.multiple_of(start, multiple)


def _t5_bucket(dist):
    n = jnp.maximum(dist, 0)
    max_exact = REL_BUCKETS // 2
    nf = jnp.maximum(n, 1).astype(F32)
    large = max_exact + jnp.floor(jnp.log(nf / max_exact) / math.log(REL_MAX_DISTANCE / max_exact)
                                  * (REL_BUCKETS - max_exact)).astype(jnp.int32)
    large = jnp.minimum(large, REL_BUCKETS - 1)
    return jnp.where(n < max_exact, n, large)


def _lookup(bucket, table_t):
    acc = jnp.zeros(bucket.shape, F32)
    for b in range(REL_BUCKETS):
        acc = jnp.where(bucket == b, table_t[:, b:b + 1], acc)
    return acc


def _bias_tables_kernel(tab_ref, dsw_ref, moba_ref, *, n_moba_heads, seq):
    tab = tab_ref[...]
    n_heads = dsw_ref.shape[0]
    m = lax.broadcasted_iota(jnp.int32, (n_heads, 2 * DSW_BLK), 1)
    row = lax.broadcasted_iota(jnp.int32, (n_heads, 2 * DSW_BLK), 0)
    dil = jnp.where(row < DSW_HEADS_PER_GROUP, DSW_GROUPS[0][1],
                    jnp.where(row < 2 * DSW_HEADS_PER_GROUP, DSW_GROUPS[1][1], DSW_GROUPS[2][1]))
    sub = DSW_BLK - m
    vals = _lookup(_t5_bucket(sub * dil), tab)
    dsw_ref[...] = jnp.where(sub >= 0, vals * LOG2E, NEG)
    width = moba_ref.shape[1]
    t = lax.broadcasted_iota(jnp.int32, (n_moba_heads, width), 1)
    dist = t - MOBA_BLOCK
    vals = _lookup(_t5_bucket(dist), tab[:n_moba_heads])
    moba_ref[...] = jnp.where(dist >= 0, vals, NEG)


def _bias_tables(rel_bias, n_moba_heads, seq):
    n_heads = rel_bias.shape[1]
    width = seq + MOBA_BLOCK
    return pl.pallas_call(
        functools.partial(_bias_tables_kernel, n_moba_heads=n_moba_heads, seq=seq),
        out_shape=(jax.ShapeDtypeStruct((n_heads, 2 * DSW_BLK), F32),
                   jax.ShapeDtypeStruct((n_moba_heads, width), F32)),
        name="bias_tables",
    )(rel_bias.T)


def _toeplitz(u_row, rows):
    x = jnp.broadcast_to(u_row, (rows, u_row.shape[1]))
    return pltpu.roll(x, 0, 1, stride=1, stride_axis=0)


def _dsw_bias_kernel(vec_ref, out_ref):
    lane = lax.broadcasted_iota(jnp.int32, (DSW_BLK, 2 * DSW_BLK), 1)
    for h in range(DSW_HEADS_PER_GROUP):
        tile = _toeplitz(vec_ref[0, h:h + 1, :], DSW_BLK)
        out_ref[0, h, 0] = tile
        out_ref[0, h, 1] = jnp.where(lane < DSW_BLK, NEG, tile)


def _dsw_bias_tiles(dsw_vec):
    n_groups = len(DSW_GROUPS)
    hg = DSW_HEADS_PER_GROUP
    vec = dsw_vec.reshape(n_groups, hg, 2 * DSW_BLK)
    return pl.pallas_call(
        _dsw_bias_kernel,
        grid=(n_groups,),
        in_specs=[pl.BlockSpec((1, hg, 2 * DSW_BLK), lambda g: (g, 0, 0))],
        out_specs=pl.BlockSpec((1, hg, 2, DSW_BLK, 2 * DSW_BLK), lambda g: (g, 0, 0, 0, 0)),
        out_shape=jax.ShapeDtypeStruct((n_groups, hg, 2, DSW_BLK, 2 * DSW_BLK), F32),
        compiler_params=_params(1),
        name="dsw_bias_tiles",
    )(vec)


def _rmsnorm_bf16(x, gain):
    ms = jnp.mean(x * x, axis=-1, keepdims=True)
    return (x * lax.rsqrt(ms + EPS) * gain).astype(BF16)


def _qkv_kernel(x_ref, g_ref, w_ref, qg_ref, kg_ref, *rest, width, dilations):
    n_groups = len(dilations)
    out_refs = rest[:3 * n_groups]
    u_s = rest[3 * n_groups]
    x = x_ref[0]
    rows = x.shape[0]
    ms = jnp.mean(x * x, axis=-1, keepdims=True)
    u = x * lax.rsqrt(ms + EPS) * g_ref[...]
    n_tiles = x.shape[1] // LANES
    if any(d > 1 for d in dilations):
        for j in range(n_tiles):
            u_s[j] = u[:, j * LANES:(j + 1) * LANES]
    low_lanes = lax.broadcasted_iota(jnp.int32, (rows, LANES), 1) < HEAD_DIM
    for gi, d in enumerate(dilations):
        per = rows // d
        if d == 1:
            lhs = u.astype(BF16)
        else:
            lhs = jnp.concatenate(
                [jnp.concatenate([u_s[j, pl.ds(c, per, stride=d), :] for j in range(n_tiles)], axis=1)
                 for c in range(d)], axis=0).astype(BF16)
        q_ref, k_ref, v_ref = out_refs[3 * gi:3 * gi + 3]
        for part, (gain_ref, out_ref) in enumerate(((qg_ref, q_ref), (kg_ref, k_ref), (None, v_ref))):
            col = (part * n_groups + gi) * width
            y_part = jnp.dot(lhs, w_ref[:, col:col + width].astype(BF16), preferred_element_type=F32)
            for cc in range(width // LANES):
                y = y_part[:, cc * LANES:(cc + 1) * LANES]
                if gain_ref is not None:
                    sq = y * y
                    both = jnp.sum(sq, axis=-1, keepdims=True)
                    low = jnp.sum(jnp.where(low_lanes, sq, 0.0), axis=-1, keepdims=True)
                    msq = jnp.where(low_lanes, low, both - low) * (1.0 / HEAD_DIM)
                    y = y * lax.rsqrt(msq + EPS) * gain_ref[...]
                y = y.astype(BF16)
                for c in range(d):
                    out_ref[0, c, :, cc * LANES:(cc + 1) * LANES] = y[c * per:(c + 1) * per]


def _qkv_project(x, gain, w, q_gain_row, k_gain_row, dilations, tm, name):
    batch, seq, d_model = x.shape
    width = w.shape[1] // (3 * len(dilations))
    const = lambda b, t: (0, 0)
    out_specs, out_shapes = [], []
    for d in dilations:
        out_specs += [pl.BlockSpec((1, d, tm // d, width), lambda b, t: (b, 0, t, 0))] * 3
        out_shapes += [jax.ShapeDtypeStruct((batch, d, seq // d, width), BF16)] * 3
    outs = pl.pallas_call(
        functools.partial(_qkv_kernel, width=width, dilations=tuple(dilations)),
        grid=(batch, seq // tm),
        in_specs=[pl.BlockSpec((1, tm, d_model), lambda b, t: (b, t, 0)),
                  pl.BlockSpec((1, d_model), const),
                  pl.BlockSpec(w.shape, const, pipeline_mode=pl.Buffered(1)),
                  pl.BlockSpec(q_gain_row.shape, const),
                  pl.BlockSpec(k_gain_row.shape, const)],
        out_specs=tuple(out_specs),
        out_shape=tuple(out_shapes),
        scratch_shapes=[pltpu.VMEM((d_model // LANES, tm, LANES), F32)],
        compiler_params=_params(2),
        name=name,
    )(x, gain, w, q_gain_row, k_gain_row)
    return [tuple(outs[3 * i:3 * i + 3]) for i in range(len(dilations))]


def _dsw_attn_kernel(q_ref, kc_ref, vc_ref, kp_ref, vp_ref, bias_ref, o_ref, stat_ref, *,
                     tq, blocks_per_seq):
    t = pl.program_id(0)
    n_blk = tq // DSW_BLK
    lane = lax.broadcasted_iota(jnp.int32, (DSW_BLK, LANES), 1)
    low_half = lane < HEAD_DIM
    n_pairs = q_ref.shape[1] // LANES
    for qi in range(n_blk):
        rows = slice(qi * DSW_BLK, (qi + 1) * DSW_BLK)
        first = (jnp.bitwise_and(t * n_blk + qi, blocks_per_seq - 1) == 0).astype(jnp.int32)
        stat_tile = jnp.zeros((DSW_BLK, LANES), F32)
        for hp in range(n_pairs):
            cols = slice(hp * LANES, (hp + 1) * LANES)
            q2 = q_ref[rows, cols]
            if qi == 0:
                k_prev, v_prev = kp_ref[:, cols], vp_ref[:, cols]
            else:
                prev = slice((qi - 1) * DSW_BLK, qi * DSW_BLK)
                k_prev, v_prev = kc_ref[prev, cols], vc_ref[prev, cols]
            k_cat = jnp.concatenate([k_prev, kc_ref[rows, cols]], axis=0)
            v_cat = jnp.concatenate([v_prev, vc_ref[rows, cols]], axis=0)
            outs = []
            for hh in range(2):
                head_lanes = low_half if hh == 0 else jnp.logical_not(low_half)
                qm = jnp.where(head_lanes, q2, jnp.zeros_like(q2))
                s = lax.dot_general(qm, k_cat, NT_DIMS, preferred_element_type=F32)
                s = s + bias_ref[0, hp * 2 + hh, first]
                m = jnp.max(s, axis=-1, keepdims=True)
                p = jnp.exp2(s - m)
                den = jnp.sum(p, axis=-1, keepdims=True)
                outs.append(jnp.dot(p.astype(BF16), v_cat, preferred_element_type=F32))
                head = hp * 2 + hh
                stat_tile = jnp.where(lane == head, m,
                                      jnp.where(lane == DSW_HEADS_PER_GROUP + head, den, stat_tile))
            o_ref[rows, cols] = jnp.where(low_half, outs[0], outs[1]).astype(o_ref.dtype)
        stat_ref[rows, :] = stat_tile


def _dsw_attention(q, k, v, bias_tiles, group, tq, blocks_per_seq):
    n_rows, width = q.shape
    assert blocks_per_seq & (blocks_per_seq - 1) == 0
    n_blk = tq // DSW_BLK
    cur = pl.BlockSpec((tq, width), lambda t: (t, 0))
    prev = pl.BlockSpec((DSW_BLK, width), lambda t: (jnp.maximum(t * n_blk - 1, 0), 0))
    hg = DSW_HEADS_PER_GROUP
    return pl.pallas_call(
        functools.partial(_dsw_attn_kernel, tq=tq, blocks_per_seq=blocks_per_seq),
        grid=(n_rows // tq,),
        in_specs=[cur, cur, cur, prev, prev,
                  pl.BlockSpec((1, hg, 2, DSW_BLK, 2 * DSW_BLK), lambda t: (group, 0, 0, 0, 0))],
        out_specs=(pl.BlockSpec((tq, width), lambda t: (t, 0)),
                   pl.BlockSpec((tq, LANES), lambda t: (t, 0))),
        out_shape=(jax.ShapeDtypeStruct((n_rows, width), BF16),
                   jax.ShapeDtypeStruct((n_rows, LANES), F32)),
        compiler_params=_params(1),
        name=f"dsw_attention_g{group}",
    )(q, k, v, k, v, bias_tiles)


def _merge_wo_kernel(*refs, dilations):
    n = len(dilations)
    o_refs, l_refs = refs[:n], refs[n:2 * n]
    x_ref, wo_ref, e_ref, out_ref = refs[2 * n:2 * n + 4]
    scratch = refs[2 * n + 4:]
    rows = x_ref.shape[1]
    outs, lses = [], []
    si = 0
    for d, o_ref, l_ref in zip(dilations, o_refs, l_refs):
        if d == 1:
            outs.append(o_ref[0, 0].astype(F32))
            lses.append(l_ref[0, 0])
            continue
        o_s, l_s = scratch[si], scratch[si + 1]
        si += 2
        n_tiles = o_s.shape[0]
        for c in range(d):
            piece = o_ref[0, c].astype(F32)
            for j in range(n_tiles):
                o_s[j, pl.ds(c, rows // d, stride=d), :] = piece[:, j * LANES:(j + 1) * LANES]
            l_s[pl.ds(c, rows // d, stride=d), :] = l_ref[0, c]
        outs.append(jnp.concatenate([o_s[j] for j in range(n_tiles)], axis=1))
        lses.append(l_s[...])
    lane = lax.broadcasted_iota(jnp.int32, lses[0].shape, 1)
    mx = functools.reduce(jnp.maximum, lses)
    es = [jnp.exp2(l - mx) for l in lses]
    sums = [pltpu.roll(l, LANES - DSW_HEADS_PER_GROUP, 1) for l in lses]
    total = functools.reduce(lambda a, b: a + b, [e * s for e, s in zip(es, sums)])
    merged = jnp.zeros(outs[0].shape, F32)
    for e, o in zip(es, outs):
        wgt = jnp.where(lane < DSW_HEADS_PER_GROUP, e / total, 0.0)
        hi = wgt.astype(BF16)
        lo = (wgt - hi.astype(F32)).astype(BF16)
        spread = (jnp.dot(hi, e_ref[...], preferred_element_type=F32)
                  + jnp.dot(lo, e_ref[...], preferred_element_type=F32))
        merged = merged + spread * o
    out_ref[0] = x_ref[0] + jnp.dot(merged.astype(BF16), wo_ref[...].astype(BF16),
                                    preferred_element_type=F32)


def _merge_wo(os, lses, x, wo, dilations, tm):
    batch, seq, d_model = x.shape
    width = os[0].shape[-1]
    head_of_lane = jnp.arange(width) // HEAD_DIM
    expand = (jnp.arange(LANES)[:, None] == head_of_lane[None, :]).astype(BF16)
    const = lambda b, t: (0, 0)
    blk = lambda d, w: pl.BlockSpec((1, d, tm // d, w), lambda b, t: (b, 0, t, 0))
    x_spec = pl.BlockSpec((1, tm, d_model), lambda b, t: (b, t, 0))
    scratch = []
    for d in dilations:
        if d > 1:
            scratch += [pltpu.VMEM((width // LANES, tm, LANES), F32), pltpu.VMEM((tm, LANES), F32)]
    return pl.pallas_call(
        functools.partial(_merge_wo_kernel, dilations=tuple(dilations)),
        grid=(batch, seq // tm),
        in_specs=([blk(d, width) for d in dilations] + [blk(d, LANES) for d in dilations]
                  + [x_spec, pl.BlockSpec(wo.shape, const), pl.BlockSpec(expand.shape, const)]),
        out_specs=x_spec,
        out_shape=jax.ShapeDtypeStruct(x.shape, F32),
        scratch_shapes=scratch,
        compiler_params=_params(2),
        name="dsw_merge_wo",
    )(*os, *lses, x, wo, expand)


def _ffn_kernel(*refs, has_attn, ff_chunk):
    if has_attn:
        h_ref, a_ref, wo_ref, g_ref, w1_ref, w2_ref, out_ref = refs
        h = h_ref[...] + jnp.dot(a_ref[...], wo_ref[...].astype(BF16), preferred_element_type=F32)
    else:
        h_ref, g_ref, w1_ref, w2_ref, out_ref = refs
        h = h_ref[...]
    u = _rmsnorm_bf16(h, g_ref[...])
    acc = h
    for c in range(w1_ref.shape[2] // ff_chunk):
        cols = slice(c * ff_chunk, (c + 1) * ff_chunk)
        a = jnp.dot(u, w1_ref[0, :, cols].astype(BF16), preferred_element_type=F32)
        a = jnp.square(jnp.maximum(a, 0.0)).astype(BF16)
        acc = acc + jnp.dot(a, w2_ref[0, cols, :].astype(BF16), preferred_element_type=F32)
    out_ref[...] = acc


def _ffn(h, gain, w1, w2, layer, tm, attn=None, wo=None, name="ffn"):
    n_rows, d_model = h.shape
    const = lambda i: (0, 0)
    row_spec = pl.BlockSpec((tm, d_model), lambda i: (i, 0))
    in_specs = [row_spec]
    args = [h]
    if attn is not None:
        in_specs += [pl.BlockSpec((tm, attn.shape[1]), lambda i: (i, 0)), pl.BlockSpec(wo.shape, const)]
        args += [attn, wo]
    in_specs += [pl.BlockSpec((1, d_model), const),
                 pl.BlockSpec((1,) + w1.shape[1:], lambda i: (layer, 0, 0), pipeline_mode=pl.Buffered(1)),
                 pl.BlockSpec((1,) + w2.shape[1:], lambda i: (layer, 0, 0), pipeline_mode=pl.Buffered(1))]
    args += [gain, w1, w2]
    return pl.pallas_call(
        functools.partial(_ffn_kernel, has_attn=attn is not None, ff_chunk=1024),
        grid=(n_rows // tm,),
        in_specs=in_specs,
        out_specs=row_spec,
        out_shape=jax.ShapeDtypeStruct(h.shape, F32),
        compiler_params=_params(1),
        name=name,
    )(*args)


def _moba_kernel(q_ref, k_ref, v_ref, wn_ref, o_ref,
                 bias_s, kp_s, vt_s, qa_s, s_s, acc_s, *, n_blocks, chunk):
    b = pl.program_id(1)
    blk = MOBA_BLOCK
    seq = n_blocks * blk
    span = chunk * blk
    lane = lax.broadcasted_iota(jnp.int32, (blk, LANES), 1)
    low_lanes = lane < HEAD_DIM
    v_rows = vt_s.shape[1]
    ones_row = (lax.broadcasted_iota(jnp.int32, (v_rows - HEAD_DIM, blk), 0) == 0).astype(F32)

    @pl.when(b == 0)
    def _build_bias():
        for hh in range(2):
            for dlt in range(n_blocks):
                u = jnp.concatenate([wn_ref[0, hh:hh + 1, (dlt + 1) * blk:(dlt + 2) * blk],
                                     wn_ref[0, hh:hh + 1, dlt * blk:(dlt + 1) * blk]], axis=1)
                r = n_blocks - 1 - dlt
                bias_s[hh, r * blk:(r + 1) * blk, :] = _toeplitz(u, blk)[:, :blk] * LOG2E
            bias_s[hh, seq:, :] = jnp.full((span - blk, blk), NEG, F32)

    sub = lax.broadcasted_iota(jnp.int32, (n_blocks, LANES), 0)
    kmean = jnp.zeros((n_blocks, LANES), F32)
    kamax = []
    for j in range(n_blocks):
        rows = slice(j * blk, (j + 1) * blk)
        kj = k_ref[0, rows, :].astype(F32)
        vj_t = v_ref[0, rows, :].astype(F32).T
        kmean = jnp.where(sub == j, jnp.mean(kj, axis=0, keepdims=True), kmean)
        kamax.append(jnp.max(jnp.abs(kj), axis=0, keepdims=True))
        aux0 = jnp.logical_or(lane == HEAD_DIM + j, lane == HEAD_DIM + n_blocks).astype(F32)
        aux1 = jnp.logical_or(lane == j, lane == n_blocks).astype(F32)
        kp_s[0, rows, :] = jnp.where(low_lanes, kj, aux0).astype(BF16)
        kp_s[1, rows, :] = jnp.where(low_lanes, aux1, kj).astype(BF16)
        for hh in range(2):
            vt_s[hh, :, rows] = jnp.concatenate(
                [vj_t[hh * HEAD_DIM:(hh + 1) * HEAD_DIM], ones_row], axis=0).astype(BF16)
    km_hi = kmean.astype(BF16)
    km_lo = (kmean - km_hi.astype(F32)).astype(BF16)
    kamax_upto = [kamax[0]]
    for j in range(1, n_blocks):
        kamax_upto.append(jnp.maximum(kamax_upto[-1], kamax[j]))
    pad_rows = jnp.zeros((6, LANES), F32)
    bias_max = [jnp.max(wn_ref[0, hh:hh + 1, blk:], axis=1, keepdims=True) * LOG2E for hh in range(2)]
    bias_zero = [wn_ref[0, hh:hh + 1, blk:blk + 1] * LOG2E for hh in range(2)]

    blk_row = lax.broadcasted_iota(jnp.int32, (n_blocks, blk), 0)
    blk_row_f = blk_row.astype(F32)
    gap = jnp.full((1, blk), -jnp.inf, F32)
    for n in range(n_blocks):
        rows = slice(n * blk, (n + 1) * blk)
        q2 = q_ref[0, rows, :]
        q_abs = jnp.abs(q2)
        k_bounds = jnp.concatenate([kamax_upto[n], kamax[n], pad_rows], axis=0).astype(BF16)
        for hh in range(2):
            own = low_lanes if hh == 0 else jnp.logical_not(low_lanes)
            qm = jnp.where(own, q2, jnp.zeros_like(q2))
            gate = (lax.dot_general(km_hi, qm, NT_DIMS, preferred_element_type=F32)
                    + lax.dot_general(km_lo, qm, NT_DIMS, preferred_element_type=F32))
            gate = jnp.where(blk_row < n, gate, -jnp.inf)
            chosen = blk_row == n
            for _ in range(min(MOBA_TOPK, n)):
                best = jnp.max(gate, axis=0, keepdims=True)
                cand = jnp.where(gate == best, blk_row_f, float(n_blocks))
                first = jnp.min(cand, axis=0, keepdims=True)
                pick = blk_row_f == first
                chosen = jnp.logical_or(chosen, pick)
                gate = jnp.where(pick, -jnp.inf, gate)
            pen_t = jnp.where(chosen, 0.0, NEG)
            sums = lax.dot_general(k_bounds, jnp.where(own, q_abs, jnp.zeros_like(q_abs)),
                                   NT_DIMS, preferred_element_type=F32)
            bound = sums[0:1] + bias_max[hh] + 1.0
            gap = jnp.maximum(gap, bound - (bias_zero[hh] - sums[1:2]))
            pad_lo = HEAD_DIM if hh == 0 else 0
            pieces = [jnp.zeros((pad_lo, blk), F32)] if pad_lo else []
            pieces += [pen_t, -bound, jnp.zeros((LANES - pad_lo - n_blocks - 1, blk), F32)]
            pen = jnp.concatenate(pieces, axis=0).T
            qa_s[hh, rows, :] = jnp.where(own, q2, pen.astype(BF16))

    def emit(n, width):
        cols = width * blk
        o_t = jnp.concatenate([acc_s[hh, :HEAD_DIM, :cols] / acc_s[hh, HEAD_DIM:HEAD_DIM + 1, :cols]
                               for hh in range(2)], axis=0)
        o_ref[0, pl.ds(_aligned(n * blk, blk), cols), :] = o_t.T.astype(o_ref.dtype)

    def raw_scores(hh, n, key0, n_keys, width):
        q_aug = qa_s[hh, pl.ds(_aligned(n * blk, blk), width * blk), :]
        s_t = lax.dot_general(kp_s[hh, key0:key0 + n_keys, :], q_aug, NT_DIMS,
                              preferred_element_type=F32)
        bias = [bias_s[hh, pl.ds(_aligned((n_blocks - 1 - n - i) * blk + key0, blk), n_keys), :]
                for i in range(width)]
        return s_t + (bias[0] if width == 1 else jnp.concatenate(bias, axis=1))

    acc_s[...] = jnp.ones(acc_s.shape, F32)
    n_groups = n_blocks // chunk
    bound_is_tight = jnp.max(gap) < EXP_RANGE
    one = jnp.minimum(b + 1, 1)

    @pl.when(bound_is_tight)
    def _single_pass():
        def q_pair(pair):
            n = 2 * pair
            own_chunk, own_blocks = divmod(n + 2, chunk)
            emit(jnp.maximum(n - 2, 0), 2)
            ranges = [(c * span, span) for c in range(own_chunk)]
            if own_blocks:
                ranges.append((own_chunk * span, own_blocks * blk))
            for hh in range(2):
                acc = None
                for key0, n_keys in ranges:
                    p_t = jnp.exp2(raw_scores(hh, n, key0, n_keys, 2)).astype(BF16)
                    part = jnp.dot(vt_s[hh, :, key0:key0 + n_keys], p_t,
                                   preferred_element_type=F32)
                    acc = part if acc is None else acc + part
                acc_s[hh] = acc

        for pair in range(n_blocks // 2):
            lax.fori_loop(0, one, lambda t, c, pair=pair: (q_pair(pair), c)[1], 0)
        emit(n_blocks - 2, 2)

    @pl.when(jnp.logical_not(bound_is_tight))
    def _two_pass():
        def scores(hh, n, n_chunks):
            m = None
            for c in range(n_chunks):
                s_t = raw_scores(hh, n, c * span, span, 1)
                s_s[hh, c * span:(c + 1) * span, :] = s_t
                cm = jnp.max(s_t, axis=0, keepdims=True)
                m = cm if m is None else jnp.maximum(m, cm)
            return m

        def weighted_values(hh, m, n_chunks):
            n_keys = n_chunks * span
            p_t = jnp.exp2(s_s[hh, :n_keys, :] - m).astype(BF16)
            return jnp.dot(vt_s[hh, :, :n_keys], p_t, preferred_element_type=F32)

        def step(n, m0, n_chunks, next_chunks):
            emit(jnp.maximum(n - 1, 0), 1)
            acc0 = weighted_values(0, m0, n_chunks)
            m1 = scores(1, n, n_chunks)
            m0_next = scores(0, n + 1, next_chunks) if next_chunks else m0
            acc1 = weighted_values(1, m1, n_chunks)
            acc_s[0, :, :blk] = acc0
            acc_s[1, :, :blk] = acc1
            return m0_next

        m0 = scores(0, 0, 1)
        for g in range(n_groups):
            m0 = lax.fori_loop(0, chunk - 1,
                               lambda t, m, g=g: step(g * chunk + t, m, g + 1, g + 1), m0)
            m0 = step(g * chunk + chunk - 1, m0, g + 1, g + 2 if g + 1 < n_groups else 0)
        emit(n_blocks - 1, 1)


def _moba_attention(q, k, v, moba_vec, batch, seq, chunk):
    n_rows, width = q.shape
    n_pairs = width // LANES
    n_blocks = seq // MOBA_BLOCK
    assert n_blocks % chunk == 0
    wn = moba_vec.reshape(n_pairs, 2, moba_vec.shape[1])
    spec = pl.BlockSpec((1, seq, LANES), lambda p, b: (b, 0, p))
    shape3 = (batch, seq, width)
    out = pl.pallas_call(
        functools.partial(_moba_kernel, n_blocks=n_blocks, chunk=chunk),
        grid=(n_pairs, batch),
        in_specs=[spec, spec, spec, pl.BlockSpec((1, 2, wn.shape[2]), lambda p, b: (p, 0, 0))],
        out_specs=spec,
        out_shape=jax.ShapeDtypeStruct(shape3, BF16),
        scratch_shapes=[pltpu.VMEM((2, seq + (chunk - 1) * MOBA_BLOCK, MOBA_BLOCK), F32),
                        pltpu.VMEM((2, seq, LANES), BF16),
                        pltpu.VMEM((2, HEAD_DIM + 16, seq), BF16),
                        pltpu.VMEM((2, seq, LANES), BF16),
                        pltpu.VMEM((2, seq, MOBA_BLOCK), F32),
                        pltpu.VMEM((2, HEAD_DIM + 16, 2 * MOBA_BLOCK), F32)],
        compiler_params=_params(2),
        name="moba_attention",
    )(q.reshape(shape3), k.reshape(shape3), v.reshape(shape3), wn)
    return out.reshape(n_rows, width)


def _gain_row(gain, size, scale):
    return (jnp.tile(gain.astype(F32), size // HEAD_DIM) * scale).reshape(1, size)


def kernel(x, rel_bias, norm_mix, norm_ffn, a_w_qkv, a_q_gain, a_k_gain, a_w_o,
           b_w_qkv, b_q_gain, b_k_gain, b_w_o, ffn_w1, ffn_w2):
    batch, seq, d_model = x.shape
    n_rows = batch * seq
    n_groups = len(DSW_GROUPS)
    dilations = [d for _, d in DSW_GROUPS]
    gw = DSW_HEADS_PER_GROUP * HEAD_DIM
    moba_heads = b_w_o.shape[1] // HEAD_DIM

    dsw_vec, moba_vec = _bias_tables(rel_bias, moba_heads, seq)
    dsw_tiles = _dsw_bias_tiles(dsw_vec)

    qkv = _qkv_project(x, norm_mix[0].reshape(1, d_model), a_w_qkv[0],
                       _gain_row(a_q_gain[0], LANES, SCALE * LOG2E), _gain_row(a_k_gain[0], LANES, 1.0),
                       dilations, tm=512, name="dsw_qkv")
    os, lses = [], []
    for g, d in enumerate(dilations):
        q_g, k_g, v_g = (a.reshape(n_rows, gw) for a in qkv[g])
        o_g, lse_g = _dsw_attention(q_g, k_g, v_g, dsw_tiles, g, tq=1024,
                                    blocks_per_seq=seq // d // DSW_BLK)
        os.append(o_g.reshape(batch, d, seq // d, gw))
        lses.append(lse_g.reshape(batch, d, seq // d, LANES))
    h = _merge_wo(os, lses, x, a_w_o[0], dilations, tm=512)
    h = h.reshape(n_rows, d_model)
    w1, w2 = ffn_w1, ffn_w2
    h = _ffn(h, norm_ffn[0].reshape(1, d_model), w1, w2, 0, tm=512, name="ffn0")

    width = moba_heads * HEAD_DIM
    (q, k, v), = _qkv_project(
        h.reshape(batch, seq, d_model), norm_mix[1].reshape(1, d_model), b_w_qkv[0],
        _gain_row(b_q_gain[0], LANES, SCALE * LOG2E), _gain_row(b_k_gain[0], LANES, 1.0),
        [1], tm=512, name="moba_qkv")
    q, k, v = (a.reshape(n_rows, width) for a in (q, k, v))
    attn = _moba_attention(q, k, v, moba_vec, batch, seq, chunk=4)
    h = _ffn(h, norm_ffn[1].reshape(1, d_model), w1, w2, 1,
             tm=512, attn=attn, wo=b_w_o[0], name="wo_ffn1")
    return h.reshape(batch, seq, d_model)
```

```python
import functools
import math

import jax
import jax.numpy as jnp
from jax import lax
from jax.experimental import pallas as pl
from jax.experimental.pallas import tpu as pltpu

HEAD_DIM = 64
LANES = 128
DSW_GROUPS = ((128, 1), (512, 4), (2048, 16))
DSW_BLK = 128
DSW_HEADS_PER_GROUP = 8
MOBA_BLOCK = 256
MOBA_TOPK = 3
REL_BUCKETS = 32
REL_MAX_DISTANCE = 2048
EPS = 1e-6
NEG = -1e30
SCALE = HEAD_DIM ** -0.5
LOG2E = 1.4426950408889634
EXP_RANGE = 100.0
VMEM_LIMIT_BYTES = 56 * 1024 * 1024
ROWS_PER_STEP = 512
DSW_ROWS_PER_STEP = 1024
MOBA_CHUNK_BLOCKS = 4

F32 = jnp.float32
BF16 = jnp.bfloat16
NT_DIMS = (((1,), (1,)), ((), ()))


def _params(n_axes):
    return pltpu.CompilerParams(dimension_semantics=("arbitrary",) * n_axes,
                                vmem_limit_bytes=VMEM_LIMIT_BYTES)


def _aligned(start, multiple):
    return start if isinstance(start, int) else pl.multiple_of(start, multiple)


def _t5_bucket(dist):
    n = jnp.maximum(dist, 0)
    max_exact = REL_BUCKETS // 2
    nf = jnp.maximum(n, 1).astype(F32)
    large = max_exact + jnp.floor(jnp.log(nf / max_exact) / math.log(REL_MAX_DISTANCE / max_exact)
                                  * (REL_BUCKETS - max_exact)).astype(jnp.int32)
    large = jnp.minimum(large, REL_BUCKETS - 1)
    return jnp.where(n < max_exact, n, large)


def _lookup(bucket, table_t):
    acc = jnp.zeros(bucket.shape, F32)
    for b in range(REL_BUCKETS):
        acc = jnp.where(bucket == b, table_t[:, b:b + 1], acc)
    return acc


def _bias_tables_kernel(tab_ref, dsw_ref, moba_ref, *, n_moba_heads, seq):
    tab = tab_ref[...]
    n_heads = dsw_ref.shape[0]
    m = lax.broadcasted_iota(jnp.int32, (n_heads, 2 * DSW_BLK), 1)
    row = lax.broadcasted_iota(jnp.int32, (n_heads, 2 * DSW_BLK), 0)
    dil = jnp.where(row < DSW_HEADS_PER_GROUP, DSW_GROUPS[0][1],
                    jnp.where(row < 2 * DSW_HEADS_PER_GROUP, DSW_GROUPS[1][1], DSW_GROUPS[2][1]))
    sub = DSW_BLK - m
    vals = _lookup(_t5_bucket(sub * dil), tab)
    dsw_ref[...] = jnp.where(sub >= 0, vals * LOG2E, NEG)
    width = moba_ref.shape[1]
    t = lax.broadcasted_iota(jnp.int32, (n_moba_heads, width), 1)
    dist = t - MOBA_BLOCK
    vals = _lookup(_t5_bucket(dist), tab[:n_moba_heads])
    moba_ref[...] = jnp.where(dist >= 0, vals, NEG)


def _bias_tables(rel_bias, n_moba_heads, seq):
    n_heads = rel_bias.shape[1]
    width = seq + MOBA_BLOCK
    return pl.pallas_call(
        functools.partial(_bias_tables_kernel, n_moba_heads=n_moba_heads, seq=seq),
        out_shape=(jax.ShapeDtypeStruct((n_heads, 2 * DSW_BLK), F32),
                   jax.ShapeDtypeStruct((n_moba_heads, width), F32)),
        name="bias_tables",
    )(rel_bias.T)


def _toeplitz(u_row, rows):
    x = jnp.broadcast_to(u_row, (rows, u_row.shape[1]))
    return pltpu.roll(x, 0, 1, stride=1, stride_axis=0)


def _dsw_bias_kernel(vec_ref, out_ref):
    lane = lax.broadcasted_iota(jnp.int32, (DSW_BLK, 2 * DSW_BLK), 1)
    for h in range(DSW_HEADS_PER_GROUP):
        tile = _toeplitz(vec_ref[0, h:h + 1, :], DSW_BLK)
        out_ref[0, h, 0] = tile
        out_ref[0, h, 1] = jnp.where(lane < DSW_BLK, NEG, tile)


def _dsw_bias_tiles(dsw_vec):
    n_groups = len(DSW_GROUPS)
    hg = DSW_HEADS_PER_GROUP
    vec = dsw_vec.reshape(n_groups, hg, 2 * DSW_BLK)
    return pl.pallas_call(
        _dsw_bias_kernel,
        grid=(n_groups,),
        in_specs=[pl.BlockSpec((1, hg, 2 * DSW_BLK), lambda g: (g, 0, 0))],
        out_specs=pl.BlockSpec((1, hg, 2, DSW_BLK, 2 * DSW_BLK), lambda g: (g, 0, 0, 0, 0)),
        out_shape=jax.ShapeDtypeStruct((n_groups, hg, 2, DSW_BLK, 2 * DSW_BLK), F32),
        compiler_params=_params(1),
        name="dsw_bias_tiles",
    )(vec)


def _rmsnorm_bf16(x, gain):
    ms = jnp.mean(x * x, axis=-1, keepdims=True)
    return (x * lax.rsqrt(ms + EPS) * gain).astype(BF16)


def _qkv_kernel(x_ref, g_ref, w_ref, qg_ref, kg_ref, *rest, width, dilations):
    n_groups = len(dilations)
    out_refs = rest[:3 * n_groups]
    u_s = rest[3 * n_groups]
    x = x_ref[0]
    rows = x.shape[0]
    ms = jnp.mean(x * x, axis=-1, keepdims=True)
    u = x * lax.rsqrt(ms + EPS) * g_ref[...]
    n_tiles = x.shape[1] // LANES
    if any(d > 1 for d in dilations):
        for j in range(n_tiles):
            u_s[j] = u[:, j * LANES:(j + 1) * LANES]
    low_lanes = lax.broadcasted_iota(jnp.int32, (rows, LANES), 1) < HEAD_DIM
    for gi, d in enumerate(dilations):
        per = rows // d
        if d == 1:
            lhs = u.astype(BF16)
        else:
            lhs = jnp.concatenate(
                [jnp.concatenate([u_s[j, pl.ds(c, per, stride=d), :] for j in range(n_tiles)], axis=1)
                 for c in range(d)], axis=0).astype(BF16)
        q_ref, k_ref, v_ref = out_refs[3 * gi:3 * gi + 3]
        for part, (gain_ref, out_ref) in enumerate(((qg_ref, q_ref), (kg_ref, k_ref), (None, v_ref))):
            col = (part * n_groups + gi) * width
            y_part = jnp.dot(lhs, w_ref[:, col:col + width].astype(BF16), preferred_element_type=F32)
            for cc in range(width // LANES):
                y = y_part[:, cc * LANES:(cc + 1) * LANES]
                if gain_ref is not None:
                    sq = y * y
                    both = jnp.sum(sq, axis=-1, keepdims=True)
                    low = jnp.sum(jnp.where(low_lanes, sq, 0.0), axis=-1, keepdims=True)
                    msq = jnp.where(low_lanes, low, both - low) * (1.0 / HEAD_DIM)
                    y = y * lax.rsqrt(msq + EPS) * gain_ref[...]
                y = y.astype(BF16)
                for c in range(d):
                    out_ref[0, c, :, cc * LANES:(cc + 1) * LANES] = y[c * per:(c + 1) * per]


def _qkv_project(x, gain, w, q_gain_row, k_gain_row, dilations, tm, name):
    batch, seq, d_model = x.shape
    width = w.shape[1] // (3 * len(dilations))
    const = lambda b, t: (0, 0)
    out_specs, out_shapes = [], []
    for d in dilations:
        out_specs += [pl.BlockSpec((1, d, tm // d, width), lambda b, t: (b, 0, t, 0))] * 3
        out_shapes += [jax.ShapeDtypeStruct((batch, d, seq // d, width), BF16)] * 3
    outs = pl.pallas_call(
        functools.partial(_qkv_kernel, width=width, dilations=tuple(dilations)),
        grid=(batch, seq // tm),
        in_specs=[pl.BlockSpec((1, tm, d_model), lambda b, t: (b, t, 0)),
                  pl.BlockSpec((1, d_model), const),
                  pl.BlockSpec(w.shape, const, pipeline_mode=pl.Buffered(1)),
                  pl.BlockSpec(q_gain_row.shape, const),
                  pl.BlockSpec(k_gain_row.shape, const)],
        out_specs=tuple(out_specs),
        out_shape=tuple(out_shapes),
        scratch_shapes=[pltpu.VMEM((d_model // LANES, tm, LANES), F32)],
        compiler_params=_params(2),
        name=name,
    )(x, gain, w, q_gain_row, k_gain_row)
    return [tuple(outs[3 * i:3 * i + 3]) for i in range(len(dilations))]


def _dsw_attn_kernel(q_ref, kc_ref, vc_ref, kp_ref, vp_ref, bias_ref, o_ref, stat_ref, *,
                     tq, blocks_per_seq):
    t = pl.program_id(0)
    n_blk = tq // DSW_BLK
    lane = lax.broadcasted_iota(jnp.int32, (DSW_BLK, LANES), 1)
    low_half = lane < HEAD_DIM
    n_pairs = q_ref.shape[1] // LANES
    for qi in range(n_blk):
        rows = slice(qi * DSW_BLK, (qi + 1) * DSW_BLK)
        first = (jnp.bitwise_and(t * n_blk + qi, blocks_per_seq - 1) == 0).astype(jnp.int32)
        stat_tile = jnp.zeros((DSW_BLK, LANES), F32)
        for hp in range(n_pairs):
            cols = slice(hp * LANES, (hp + 1) * LANES)
            q2 = q_ref[rows, cols]
            if qi == 0:
                k_prev, v_prev = kp_ref[:, cols], vp_ref[:, cols]
            else:
                prev = slice((qi - 1) * DSW_BLK, qi * DSW_BLK)
                k_prev, v_prev = kc_ref[prev, cols], vc_ref[prev, cols]
            k_cat = jnp.concatenate([k_prev, kc_ref[rows, cols]], axis=0)
            v_cat = jnp.concatenate([v_prev, vc_ref[rows, cols]], axis=0)
            outs = []
            for hh in range(2):
                head_lanes = low_half if hh == 0 else jnp.logical_not(low_half)
                qm = jnp.where(head_lanes, q2, jnp.zeros_like(q2))
                s = lax.dot_general(qm, k_cat, NT_DIMS, preferred_element_type=F32)
                s = s + bias_ref[0, hp * 2 + hh, first]
                m = jnp.max(s, axis=-1, keepdims=True)
                p = jnp.exp2(s - m)
                den = jnp.sum(p, axis=-1, keepdims=True)
                outs.append(jnp.dot(p.astype(BF16), v_cat, preferred_element_type=F32))
                head = hp * 2 + hh
                stat_tile = jnp.where(lane == head, m,
                                      jnp.where(lane == DSW_HEADS_PER_GROUP + head, den, stat_tile))
            o_ref[rows, cols] = jnp.where(low_half, outs[0], outs[1]).astype(o_ref.dtype)
        stat_ref[rows, :] = stat_tile


def _dsw_attention(q, k, v, bias_tiles, group, tq, blocks_per_seq):
    n_rows, width = q.shape
    assert blocks_per_seq & (blocks_per_seq - 1) == 0
    n_blk = tq // DSW_BLK
    cur = pl.BlockSpec((tq, width), lambda t: (t, 0))
    prev = pl.BlockSpec((DSW_BLK, width), lambda t: (jnp.maximum(t * n_blk - 1, 0), 0))
    hg = DSW_HEADS_PER_GROUP
    return pl.pallas_call(
        functools.partial(_dsw_attn_kernel, tq=tq, blocks_per_seq=blocks_per_seq),
        grid=(n_rows // tq,),
        in_specs=[cur, cur, cur, prev, prev,
                  pl.BlockSpec((1, hg, 2, DSW_BLK, 2 * DSW_BLK), lambda t: (group, 0, 0, 0, 0))],
        out_specs=(pl.BlockSpec((tq, width), lambda t: (t, 0)),
                   pl.BlockSpec((tq, LANES), lambda t: (t, 0))),
        out_shape=(jax.ShapeDtypeStruct((n_rows, width), BF16),
                   jax.ShapeDtypeStruct((n_rows, LANES), F32)),
        compiler_params=_params(1),
        name=f"dsw_attention_g{group}",
    )(q, k, v, k, v, bias_tiles)


def _merge_wo_kernel(*refs, dilations):
    n = len(dilations)
    o_refs, l_refs = refs[:n], refs[n:2 * n]
    x_ref, wo_ref, e_ref, out_ref = refs[2 * n:2 * n + 4]
    scratch = refs[2 * n + 4:]
    rows = x_ref.shape[1]
    outs, lses = [], []
    si = 0
    for d, o_ref, l_ref in zip(dilations, o_refs, l_refs):
        if d == 1:
            outs.append(o_ref[0, 0].astype(F32))
            lses.append(l_ref[0, 0])
            continue
        o_s, l_s = scratch[si], scratch[si + 1]
        si += 2
        n_tiles = o_s.shape[0]
        for c in range(d):
            piece = o_ref[0, c].astype(F32)
            for j in range(n_tiles):
                o_s[j, pl.ds(c, rows // d, stride=d), :] = piece[:, j * LANES:(j + 1) * LANES]
            l_s[pl.ds(c, rows // d, stride=d), :] = l_ref[0, c]
        outs.append(jnp.concatenate([o_s[j] for j in range(n_tiles)], axis=1))
        lses.append(l_s[...])
    lane = lax.broadcasted_iota(jnp.int32, lses[0].shape, 1)
    mx = functools.reduce(jnp.maximum, lses)
    es = [jnp.exp2(l - mx) for l in lses]
    sums = [pltpu.roll(l, LANES - DSW_HEADS_PER_GROUP, 1) for l in lses]
    total = functools.reduce(lambda a, b: a + b, [e * s for e, s in zip(es, sums)])
    merged = jnp.zeros(outs[0].shape, F32)
    for e, o in zip(es, outs):
        wgt = jnp.where(lane < DSW_HEADS_PER_GROUP, e / total, 0.0)
        hi = wgt.astype(BF16)
        lo = (wgt - hi.astype(F32)).astype(BF16)
        spread = (jnp.dot(hi, e_ref[...], preferred_element_type=F32)
                  + jnp.dot(lo, e_ref[...], preferred_element_type=F32))
        merged = merged + spread * o
    out_ref[0] = x_ref[0] + jnp.dot(merged.astype(BF16), wo_ref[...].astype(BF16),
                                    preferred_element_type=F32)


def _merge_wo(os, lses, x, wo, dilations, tm):
    batch, seq, d_model = x.shape
    width = os[0].shape[-1]
    head_of_lane = jnp.arange(width) // HEAD_DIM
    expand = (jnp.arange(LANES)[:, None] == head_of_lane[None, :]).astype(BF16)
    const = lambda b, t: (0, 0)
    blk = lambda d, w: pl.BlockSpec((1, d, tm // d, w), lambda b, t: (b, 0, t, 0))
    x_spec = pl.BlockSpec((1, tm, d_model), lambda b, t: (b, t, 0))
    scratch = []
    for d in dilations:
        if d > 1:
            scratch += [pltpu.VMEM((width // LANES, tm, LANES), F32), pltpu.VMEM((tm, LANES), F32)]
    return pl.pallas_call(
        functools.partial(_merge_wo_kernel, dilations=tuple(dilations)),
        grid=(batch, seq // tm),
        in_specs=([blk(d, width) for d in dilations] + [blk(d, LANES) for d in dilations]
                  + [x_spec, pl.BlockSpec(wo.shape, const), pl.BlockSpec(expand.shape, const)]),
        out_specs=x_spec,
        out_shape=jax.ShapeDtypeStruct(x.shape, F32),
        scratch_shapes=scratch,
        compiler_params=_params(2),
        name="dsw_merge_wo",
    )(*os, *lses, x, wo, expand)


def _ffn_kernel(*refs, has_attn):
    if has_attn:
        h_ref, a_ref, wo_ref, g_ref, w1_ref, w2_ref, out_ref = refs
        h = h_ref[...] + jnp.dot(a_ref[...], wo_ref[...].astype(BF16), preferred_element_type=F32)
    else:
        h_ref, g_ref, w1_ref, w2_ref, out_ref = refs
        h = h_ref[...]
    u = _rmsnorm_bf16(h, g_ref[...])
    a = jnp.dot(u, w1_ref[0].astype(BF16), preferred_element_type=F32)
    a = jnp.square(jnp.maximum(a, 0.0)).astype(BF16)
    out_ref[...] = h + jnp.dot(a, w2_ref[0].astype(BF16), preferred_element_type=F32)


def _ffn(h, gain, w1, w2, layer, tm, attn=None, wo=None, name="ffn"):
    n_rows, d_model = h.shape
    const = lambda i: (0, 0)
    row_spec = pl.BlockSpec((tm, d_model), lambda i: (i, 0))
    in_specs = [row_spec]
    args = [h]
    if attn is not None:
        in_specs += [pl.BlockSpec((tm, attn.shape[1]), lambda i: (i, 0)), pl.BlockSpec(wo.shape, const)]
        args += [attn, wo]
    in_specs += [pl.BlockSpec((1, d_model), const),
                 pl.BlockSpec((1,) + w1.shape[1:], lambda i: (layer, 0, 0), pipeline_mode=pl.Buffered(1)),
                 pl.BlockSpec((1,) + w2.shape[1:], lambda i: (layer, 0, 0), pipeline_mode=pl.Buffered(1))]
    args += [gain, w1, w2]
    return pl.pallas_call(
        functools.partial(_ffn_kernel, has_attn=attn is not None),
        grid=(n_rows // tm,),
        in_specs=in_specs,
        out_specs=row_spec,
        out_shape=jax.ShapeDtypeStruct(h.shape, F32),
        compiler_params=_params(1),
        name=name,
    )(*args)


def _moba_kernel(q_ref, k_ref, v_ref, wn_ref, o_ref,
                 bias_s, kp_s, vt_s, qa_s, s_s, acc_s, *, n_blocks, chunk):
    b = pl.program_id(1)
    blk = MOBA_BLOCK
    seq = n_blocks * blk
    span = chunk * blk
    lane = lax.broadcasted_iota(jnp.int32, (blk, LANES), 1)
    low_lanes = lane < HEAD_DIM
    v_rows = vt_s.shape[1]
    ones_row = (lax.broadcasted_iota(jnp.int32, (v_rows - HEAD_DIM, blk), 0) == 0).astype(F32)

    @pl.when(b == 0)
    def _build_bias():
        for hh in range(2):
            for dlt in range(n_blocks):
                u = jnp.concatenate([wn_ref[0, hh:hh + 1, (dlt + 1) * blk:(dlt + 2) * blk],
                                     wn_ref[0, hh:hh + 1, dlt * blk:(dlt + 1) * blk]], axis=1)
                r = n_blocks - 1 - dlt
                bias_s[hh, r * blk:(r + 1) * blk, :] = _toeplitz(u, blk)[:, :blk] * LOG2E
            bias_s[hh, seq:, :] = jnp.full((span - blk, blk), NEG, F32)

    sub = lax.broadcasted_iota(jnp.int32, (n_blocks, LANES), 0)
    kmean = jnp.zeros((n_blocks, LANES), F32)
    kamax = []
    for j in range(n_blocks):
        rows = slice(j * blk, (j + 1) * blk)
        kj = k_ref[0, rows, :].astype(F32)
        vj_t = v_ref[0, rows, :].astype(F32).T
        kmean = jnp.where(sub == j, jnp.mean(kj, axis=0, keepdims=True), kmean)
        kamax.append(jnp.max(jnp.abs(kj), axis=0, keepdims=True))
        aux0 = jnp.logical_or(lane == HEAD_DIM + j, lane == HEAD_DIM + n_blocks).astype(F32)
        aux1 = jnp.logical_or(lane == j, lane == n_blocks).astype(F32)
        kp_s[0, rows, :] = jnp.where(low_lanes, kj, aux0).astype(BF16)
        kp_s[1, rows, :] = jnp.where(low_lanes, aux1, kj).astype(BF16)
        for hh in range(2):
            vt_s[hh, :, rows] = jnp.concatenate(
                [vj_t[hh * HEAD_DIM:(hh + 1) * HEAD_DIM], ones_row], axis=0).astype(BF16)
    km_hi = kmean.astype(BF16)
    km_lo = (kmean - km_hi.astype(F32)).astype(BF16)
    kamax_upto = [kamax[0]]
    for j in range(1, n_blocks):
        kamax_upto.append(jnp.maximum(kamax_upto[-1], kamax[j]))
    pad_rows = jnp.zeros((6, LANES), F32)
    bias_max = [jnp.max(wn_ref[0, hh:hh + 1, blk:], axis=1, keepdims=True) * LOG2E for hh in range(2)]
    bias_zero = [wn_ref[0, hh:hh + 1, blk:blk + 1] * LOG2E for hh in range(2)]

    blk_row = lax.broadcasted_iota(jnp.int32, (n_blocks, blk), 0)
    blk_row_f = blk_row.astype(F32)
    gap = jnp.full((1, blk), -jnp.inf, F32)
    for n in range(n_blocks):
        rows = slice(n * blk, (n + 1) * blk)
        q2 = q_ref[0, rows, :]
        q_abs = jnp.abs(q2)
        k_bounds = jnp.concatenate([kamax_upto[n], kamax[n], pad_rows], axis=0).astype(BF16)
        for hh in range(2):
            own = low_lanes if hh == 0 else jnp.logical_not(low_lanes)
            qm = jnp.where(own, q2, jnp.zeros_like(q2))
            gate = (lax.dot_general(km_hi, qm, NT_DIMS, preferred_element_type=F32)
                    + lax.dot_general(km_lo, qm, NT_DIMS, preferred_element_type=F32))
            gate = jnp.where(blk_row < n, gate, -jnp.inf)
            chosen = blk_row == n
            for _ in range(min(MOBA_TOPK, n)):
                best = jnp.max(gate, axis=0, keepdims=True)
                cand = jnp.where(gate == best, blk_row_f, float(n_blocks))
                first = jnp.min(cand, axis=0, keepdims=True)
                pick = blk_row_f == first
                chosen = jnp.logical_or(chosen, pick)
                gate = jnp.where(pick, -jnp.inf, gate)
            pen_t = jnp.where(chosen, 0.0, NEG)
            sums = lax.dot_general(k_bounds, jnp.where(own, q_abs, jnp.zeros_like(q_abs)),
                                   NT_DIMS, preferred_element_type=F32)
            bound = sums[0:1] + bias_max[hh] + 1.0
            gap = jnp.maximum(gap, bound - (bias_zero[hh] - sums[1:2]))
            pad_lo = HEAD_DIM if hh == 0 else 0
            pieces = [jnp.zeros((pad_lo, blk), F32)] if pad_lo else []
            pieces += [pen_t, -bound, jnp.zeros((LANES - pad_lo - n_blocks - 1, blk), F32)]
            pen = jnp.concatenate(pieces, axis=0).T
            qa_s[hh, rows, :] = jnp.where(own, q2, pen.astype(BF16))

    def emit(n, width):
        cols = width * blk
        o_t = jnp.concatenate([acc_s[hh, :HEAD_DIM, :cols] / acc_s[hh, HEAD_DIM:HEAD_DIM + 1, :cols]
                               for hh in range(2)], axis=0)
        o_ref[0, pl.ds(_aligned(n * blk, blk), cols), :] = o_t.T.astype(o_ref.dtype)

    def raw_scores(hh, n, key0, n_keys, width):
        q_aug = qa_s[hh, pl.ds(_aligned(n * blk, blk), width * blk), :]
        s_t = lax.dot_general(kp_s[hh, key0:key0 + n_keys, :], q_aug, NT_DIMS,
                              preferred_element_type=F32)
        bias = [bias_s[hh, pl.ds(_aligned((n_blocks - 1 - n - i) * blk + key0, blk), n_keys), :]
                for i in range(width)]
        return s_t + (bias[0] if width == 1 else jnp.concatenate(bias, axis=1))

    acc_s[...] = jnp.ones(acc_s.shape, F32)
    n_groups = n_blocks // chunk
    bound_is_tight = jnp.max(gap) < EXP_RANGE
    one = jnp.minimum(b + 1, 1)

    @pl.when(bound_is_tight)
    def _single_pass():
        def q_pair(pair):
            n = 2 * pair
            wide = 4 * span
            own_chunk, own_blocks = divmod(n + 2, 4 * chunk)
            emit(jnp.maximum(n - 2, 0), 2)
            ranges = [(c * wide, wide) for c in range(own_chunk)]
            if own_blocks:
                ranges.append((own_chunk * wide, own_blocks * blk))
            for hh in range(2):
                acc = None
                for key0, n_keys in ranges:
                    p_t = jnp.exp2(raw_scores(hh, n, key0, n_keys, 2)).astype(BF16)
                    part = jnp.dot(vt_s[hh, :, key0:key0 + n_keys], p_t,
                                   preferred_element_type=F32)
                    acc = part if acc is None else acc + part
                acc_s[hh] = acc

        for pair in range(0, n_blocks // 2, 2):
            lax.fori_loop(0, one,
                          lambda t, c, pair=pair: (q_pair(pair), q_pair(pair + 1), c)[2], 0)
        emit(n_blocks - 2, 2)

    @pl.when(jnp.logical_not(bound_is_tight))
    def _two_pass():
        def scores(hh, n, n_chunks):
            m = None
            for c in range(n_chunks):
                s_t = raw_scores(hh, n, c * span, span, 1)
                s_s[hh, c * span:(c + 1) * span, :] = s_t
                cm = jnp.max(s_t, axis=0, keepdims=True)
                m = cm if m is None else jnp.maximum(m, cm)
            return m

        def weighted_values(hh, m, n_chunks):
            n_keys = n_chunks * span
            p_t = jnp.exp2(s_s[hh, :n_keys, :] - m).astype(BF16)
            return jnp.dot(vt_s[hh, :, :n_keys], p_t, preferred_element_type=F32)

        def step(n, m0, n_chunks, next_chunks):
            emit(jnp.maximum(n - 1, 0), 1)
            acc0 = weighted_values(0, m0, n_chunks)
            m1 = scores(1, n, n_chunks)
            m0_next = scores(0, n + 1, next_chunks) if next_chunks else m0
            acc1 = weighted_values(1, m1, n_chunks)
            acc_s[0, :, :blk] = acc0
            acc_s[1, :, :blk] = acc1
            return m0_next

        m0 = scores(0, 0, 1)
        for g in range(n_groups):
            m0 = lax.fori_loop(0, chunk - 1,
                               lambda t, m, g=g: step(g * chunk + t, m, g + 1, g + 1), m0)
            m0 = step(g * chunk + chunk - 1, m0, g + 1, g + 2 if g + 1 < n_groups else 0)
        emit(n_blocks - 1, 1)


def _moba_attention(q, k, v, moba_vec, batch, seq, chunk):
    n_rows, width = q.shape
    n_pairs = width // LANES
    n_blocks = seq // MOBA_BLOCK
    assert n_blocks % chunk == 0
    wn = moba_vec.reshape(n_pairs, 2, moba_vec.shape[1])
    spec = pl.BlockSpec((1, seq, LANES), lambda p, b: (b, 0, p))
    shape3 = (batch, seq, width)
    out = pl.pallas_call(
        functools.partial(_moba_kernel, n_blocks=n_blocks, chunk=chunk),
        grid=(n_pairs, batch),
        in_specs=[spec, spec, spec, pl.BlockSpec((1, 2, wn.shape[2]), lambda p, b: (p, 0, 0))],
        out_specs=spec,
        out_shape=jax.ShapeDtypeStruct(shape3, BF16),
        scratch_shapes=[pltpu.VMEM((2, seq + (chunk - 1) * MOBA_BLOCK, MOBA_BLOCK), F32),
                        pltpu.VMEM((2, seq, LANES), BF16),
                        pltpu.VMEM((2, HEAD_DIM + 16, seq), BF16),
                        pltpu.VMEM((2, seq, LANES), BF16),
                        pltpu.VMEM((2, seq, MOBA_BLOCK), F32),
                        pltpu.VMEM((2, HEAD_DIM + 16, 2 * MOBA_BLOCK), F32)],
        compiler_params=_params(2),
        name="moba_attention",
    )(q.reshape(shape3), k.reshape(shape3), v.reshape(shape3), wn)
    return out.reshape(n_rows, width)


def _gain_row(gain, size, scale):
    return (jnp.tile(gain.astype(F32), size // HEAD_DIM) * scale).reshape(1, size)


def kernel(x, rel_bias, norm_mix, norm_ffn, a_w_qkv, a_q_gain, a_k_gain, a_w_o,
           b_w_qkv, b_q_gain, b_k_gain, b_w_o, ffn_w1, ffn_w2):
    batch, seq, d_model = x.shape
    n_rows = batch * seq
    n_groups = len(DSW_GROUPS)
    dilations = [d for _, d in DSW_GROUPS]
    gw = DSW_HEADS_PER_GROUP * HEAD_DIM
    moba_heads = b_w_o.shape[1] // HEAD_DIM

    dsw_vec, moba_vec = _bias_tables(rel_bias, moba_heads, seq)
    dsw_tiles = _dsw_bias_tiles(dsw_vec)

    qkv = _qkv_project(x, norm_mix[0].reshape(1, d_model), a_w_qkv[0],
                       _gain_row(a_q_gain[0], LANES, SCALE * LOG2E), _gain_row(a_k_gain[0], LANES, 1.0),
                       dilations, tm=ROWS_PER_STEP, name="dsw_qkv")
    os, lses = [], []
    for g, d in enumerate(dilations):
        q_g, k_g, v_g = (a.reshape(n_rows, gw) for a in qkv[g])
        o_g, lse_g = _dsw_attention(q_g, k_g, v_g, dsw_tiles, g, tq=DSW_ROWS_PER_STEP,
                                    blocks_per_seq=seq // d // DSW_BLK)
        os.append(o_g.reshape(batch, d, seq // d, gw))
        lses.append(lse_g.reshape(batch, d, seq // d, LANES))
    h = _merge_wo(os, lses, x, a_w_o[0], dilations, tm=ROWS_PER_STEP)
    h = h.reshape(n_rows, d_model)
    w1, w2 = ffn_w1, ffn_w2
    h = _ffn(h, norm_ffn[0].reshape(1, d_model), w1, w2, 0, tm=ROWS_PER_STEP, name="ffn0")

    width = moba_heads * HEAD_DIM
    (q, k, v), = _qkv_project(
        h.reshape(batch, seq, d_model), norm_mix[1].reshape(1, d_model), b_w_qkv[0],
        _gain_row(b_q_gain[0], LANES, SCALE * LOG2E), _gain_row(b_k_gain[0], LANES, 1.0),
        [1], tm=ROWS_PER_STEP, name="moba_qkv")
    q, k, v = (a.reshape(n_rows, width) for a in (q, k, v))
    attn = _moba_attention(q, k, v, moba_vec, batch, seq, chunk=MOBA_CHUNK_BLOCKS)
    h = _ffn(h, norm_ffn[1].reshape(1, d_model), w1, w2, 1,
             tm=ROWS_PER_STEP, attn=attn, wo=b_w_o[0], name="wo_ffn1")
    return h.reshape(batch, seq, d_model)
```

```python
import functools
import math

import jax
import jax.numpy as jnp
from jax import lax
from jax.experimental import pallas as pl
from jax.experimental.pallas import tpu as pltpu

HEAD_DIM = 64
LANES = 128
DSW_GROUPS = ((128, 1), (512, 4), (2048, 16))
DSW_BLK = 128
DSW_HEADS_PER_GROUP = 8
MOBA_BLOCK = 256
MOBA_TOPK = 3
REL_BUCKETS = 32
REL_MAX_DISTANCE = 2048
EPS = 1e-6
NEG = -1e30
SCALE = HEAD_DIM ** -0.5
LOG2E = 1.4426950408889634
EXP_RANGE = 100.0
VMEM_LIMIT_BYTES = 56 * 1024 * 1024
ROWS_PER_STEP = 512
DSW_ROWS_PER_STEP = 1024
MOBA_CHUNK_BLOCKS = 4
MOBA_PAIRS_PER_REGION = 4
BF16_SUBLANES = 16

F32 = jnp.float32
BF16 = jnp.bfloat16
NT_DIMS = (((1,), (1,)), ((), ()))


def _params(n_axes):
    return pltpu.CompilerParams(dimension_semantics=("arbitrary",) * n_axes,
                                vmem_limit_bytes=VMEM_LIMIT_BYTES)


def _aligned(start, multiple):
    return start if isinstance(start, int) else pl.multiple_of(start, multiple)


def _t5_bucket(dist):
    n = jnp.maximum(dist, 0)
    max_exact = REL_BUCKETS // 2
    nf = jnp.maximum(n, 1).astype(F32)
    large = max_exact + jnp.floor(jnp.log(nf / max_exact) / math.log(REL_MAX_DISTANCE / max_exact)
                                  * (REL_BUCKETS - max_exact)).astype(jnp.int32)
    large = jnp.minimum(large, REL_BUCKETS - 1)
    return jnp.where(n < max_exact, n, large)


def _lookup(bucket, table_t):
    acc = jnp.zeros(bucket.shape, F32)
    for b in range(REL_BUCKETS):
        acc = jnp.where(bucket == b, table_t[:, b:b + 1], acc)
    return acc


def _bias_tables_kernel(tab_ref, dsw_ref, moba_ref, *, n_moba_heads, seq):
    tab = tab_ref[...]
    n_heads = dsw_ref.shape[0]
    m = lax.broadcasted_iota(jnp.int32, (n_heads, 2 * DSW_BLK), 1)
    row = lax.broadcasted_iota(jnp.int32, (n_heads, 2 * DSW_BLK), 0)
    dil = jnp.where(row < DSW_HEADS_PER_GROUP, DSW_GROUPS[0][1],
                    jnp.where(row < 2 * DSW_HEADS_PER_GROUP, DSW_GROUPS[1][1], DSW_GROUPS[2][1]))
    sub = DSW_BLK - m
    vals = _lookup(_t5_bucket(sub * dil), tab)
    dsw_ref[...] = jnp.where(sub >= 0, vals * LOG2E, NEG)
    width = moba_ref.shape[1]
    t = lax.broadcasted_iota(jnp.int32, (n_moba_heads, width), 1)
    dist = t - MOBA_BLOCK
    vals = _lookup(_t5_bucket(dist), tab[:n_moba_heads])
    moba_ref[...] = jnp.where(dist >= 0, vals, NEG)


def _bias_tables(rel_bias, n_moba_heads, seq):
    n_heads = rel_bias.shape[1]
    width = seq + MOBA_BLOCK
    return pl.pallas_call(
        functools.partial(_bias_tables_kernel, n_moba_heads=n_moba_heads, seq=seq),
        out_shape=(jax.ShapeDtypeStruct((n_heads, 2 * DSW_BLK), F32),
                   jax.ShapeDtypeStruct((n_moba_heads, width), F32)),
        name="bias_tables",
    )(rel_bias.T)


def _toeplitz(u_row, rows):
    x = jnp.broadcast_to(u_row, (rows, u_row.shape[1]))
    return pltpu.roll(x, 0, 1, stride=1, stride_axis=0)


def _dsw_bias_kernel(vec_ref, out_ref):
    lane = lax.broadcasted_iota(jnp.int32, (DSW_BLK, 2 * DSW_BLK), 1)
    for h in range(DSW_HEADS_PER_GROUP):
        tile = _toeplitz(vec_ref[0, h:h + 1, :], DSW_BLK)
        out_ref[0, h, 0] = tile
        out_ref[0, h, 1] = jnp.where(lane < DSW_BLK, NEG, tile)


def _dsw_bias_tiles(dsw_vec):
    n_groups = len(DSW_GROUPS)
    hg = DSW_HEADS_PER_GROUP
    vec = dsw_vec.reshape(n_groups, hg, 2 * DSW_BLK)
    return pl.pallas_call(
        _dsw_bias_kernel,
        grid=(n_groups,),
        in_specs=[pl.BlockSpec((1, hg, 2 * DSW_BLK), lambda g: (g, 0, 0))],
        out_specs=pl.BlockSpec((1, hg, 2, DSW_BLK, 2 * DSW_BLK), lambda g: (g, 0, 0, 0, 0)),
        out_shape=jax.ShapeDtypeStruct((n_groups, hg, 2, DSW_BLK, 2 * DSW_BLK), F32),
        compiler_params=_params(1),
        name="dsw_bias_tiles",
    )(vec)


def _rmsnorm_bf16(x, gain):
    ms = jnp.mean(x * x, axis=-1, keepdims=True)
    return (x * lax.rsqrt(ms + EPS) * gain).astype(BF16)


def _qkv_kernel(x_ref, g_ref, w_ref, qg_ref, kg_ref, *rest, width, dilations):
    n_groups = len(dilations)
    out_refs = rest[:3 * n_groups]
    u_s = rest[3 * n_groups]
    x = x_ref[0]
    rows = x.shape[0]
    ms = jnp.mean(x * x, axis=-1, keepdims=True)
    u = x * lax.rsqrt(ms + EPS) * g_ref[...]
    n_tiles = x.shape[1] // LANES
    if any(d > 1 for d in dilations):
        for j in range(n_tiles):
            u_s[j] = u[:, j * LANES:(j + 1) * LANES]
    low_lanes = lax.broadcasted_iota(jnp.int32, (rows, LANES), 1) < HEAD_DIM
    for gi, d in enumerate(dilations):
        per = rows // d
        if d == 1:
            lhs = u.astype(BF16)
        else:
            lhs = jnp.concatenate(
                [jnp.concatenate([u_s[j, pl.ds(c, per, stride=d), :] for j in range(n_tiles)], axis=1)
                 for c in range(d)], axis=0).astype(BF16)
        q_ref, k_ref, v_ref = out_refs[3 * gi:3 * gi + 3]
        for part, (gain_ref, out_ref) in enumerate(((qg_ref, q_ref), (kg_ref, k_ref), (None, v_ref))):
            col = (part * n_groups + gi) * width
            y_part = jnp.dot(lhs, w_ref[:, col:col + width].astype(BF16), preferred_element_type=F32)
            for cc in range(width // LANES):
                y = y_part[:, cc * LANES:(cc + 1) * LANES]
                if gain_ref is not None:
                    sq = y * y
                    both = jnp.sum(sq, axis=-1, keepdims=True)
                    low = jnp.sum(jnp.where(low_lanes, sq, 0.0), axis=-1, keepdims=True)
                    msq = jnp.where(low_lanes, low, both - low) * (1.0 / HEAD_DIM)
                    y = y * lax.rsqrt(msq + EPS) * gain_ref[...]
                y = y.astype(BF16)
                for c in range(d):
                    out_ref[0, c, :, cc * LANES:(cc + 1) * LANES] = y[c * per:(c + 1) * per]


def _qkv_project(x, gain, w, q_gain_row, k_gain_row, dilations, tm, name):
    batch, seq, d_model = x.shape
    width = w.shape[1] // (3 * len(dilations))
    const = lambda b, t: (0, 0)
    out_specs, out_shapes = [], []
    for d in dilations:
        out_specs += [pl.BlockSpec((1, d, tm // d, width), lambda b, t: (b, 0, t, 0))] * 3
        out_shapes += [jax.ShapeDtypeStruct((batch, d, seq // d, width), BF16)] * 3
    outs = pl.pallas_call(
        functools.partial(_qkv_kernel, width=width, dilations=tuple(dilations)),
        grid=(batch, seq // tm),
        in_specs=[pl.BlockSpec((1, tm, d_model), lambda b, t: (b, t, 0)),
                  pl.BlockSpec((1, d_model), const),
                  pl.BlockSpec(w.shape, const, pipeline_mode=pl.Buffered(1)),
                  pl.BlockSpec(q_gain_row.shape, const),
                  pl.BlockSpec(k_gain_row.shape, const)],
        out_specs=tuple(out_specs),
        out_shape=tuple(out_shapes),
        scratch_shapes=[pltpu.VMEM((d_model // LANES, tm, LANES), F32)],
        compiler_params=_params(2),
        name=name,
    )(x, gain, w, q_gain_row, k_gain_row)
    return [tuple(outs[3 * i:3 * i + 3]) for i in range(len(dilations))]


def _dsw_attn_kernel(q_ref, kc_ref, vc_ref, kp_ref, vp_ref, bias_ref, o_ref, stat_ref, *,
                     tq, blocks_per_seq):
    t = pl.program_id(0)
    n_blk = tq // DSW_BLK
    lane = lax.broadcasted_iota(jnp.int32, (DSW_BLK, LANES), 1)
    low_half = lane < HEAD_DIM
    n_pairs = q_ref.shape[1] // LANES
    for qi in range(n_blk):
        rows = slice(qi * DSW_BLK, (qi + 1) * DSW_BLK)
        first = (jnp.bitwise_and(t * n_blk + qi, blocks_per_seq - 1) == 0).astype(jnp.int32)
        stat_tile = jnp.zeros((DSW_BLK, LANES), F32)
        for hp in range(n_pairs):
            cols = slice(hp * LANES, (hp + 1) * LANES)
            q2 = q_ref[rows, cols]
            if qi == 0:
                k_prev, v_prev = kp_ref[:, cols], vp_ref[:, cols]
            else:
                prev = slice((qi - 1) * DSW_BLK, qi * DSW_BLK)
                k_prev, v_prev = kc_ref[prev, cols], vc_ref[prev, cols]
            k_cat = jnp.concatenate([k_prev, kc_ref[rows, cols]], axis=0)
            v_cat = jnp.concatenate([v_prev, vc_ref[rows, cols]], axis=0)
            outs = []
            for hh in range(2):
                head_lanes = low_half if hh == 0 else jnp.logical_not(low_half)
                qm = jnp.where(head_lanes, q2, jnp.zeros_like(q2))
                s = lax.dot_general(qm, k_cat, NT_DIMS, preferred_element_type=F32)
                s = s + bias_ref[0, hp * 2 + hh, first]
                m = jnp.max(s, axis=-1, keepdims=True)
                p = jnp.exp2(s - m)
                den = jnp.sum(p, axis=-1, keepdims=True)
                outs.append(jnp.dot(p.astype(BF16), v_cat, preferred_element_type=F32))
                head = hp * 2 + hh
                stat_tile = jnp.where(lane == head, m,
                                      jnp.where(lane == DSW_HEADS_PER_GROUP + head, den, stat_tile))
            o_ref[rows, cols] = jnp.where(low_half, outs[0], outs[1]).astype(o_ref.dtype)
        stat_ref[rows, :] = stat_tile


def _dsw_attention(q, k, v, bias_tiles, group, tq, blocks_per_seq):
    n_rows, width = q.shape
    assert blocks_per_seq & (blocks_per_seq - 1) == 0
    n_blk = tq // DSW_BLK
    cur = pl.BlockSpec((tq, width), lambda t: (t, 0))
    prev = pl.BlockSpec((DSW_BLK, width), lambda t: (jnp.maximum(t * n_blk - 1, 0), 0))
    hg = DSW_HEADS_PER_GROUP
    return pl.pallas_call(
        functools.partial(_dsw_attn_kernel, tq=tq, blocks_per_seq=blocks_per_seq),
        grid=(n_rows // tq,),
        in_specs=[cur, cur, cur, prev, prev,
                  pl.BlockSpec((1, hg, 2, DSW_BLK, 2 * DSW_BLK), lambda t: (group, 0, 0, 0, 0))],
        out_specs=(pl.BlockSpec((tq, width), lambda t: (t, 0)),
                   pl.BlockSpec((tq, LANES), lambda t: (t, 0))),
        out_shape=(jax.ShapeDtypeStruct((n_rows, width), BF16),
                   jax.ShapeDtypeStruct((n_rows, LANES), F32)),
        compiler_params=_params(1),
        name=f"dsw_attention_g{group}",
    )(q, k, v, k, v, bias_tiles)


def _merge_wo_kernel(*refs, dilations):
    n = len(dilations)
    o_refs, l_refs = refs[:n], refs[n:2 * n]
    x_ref, wo_ref, e_ref, out_ref = refs[2 * n:2 * n + 4]
    scratch = refs[2 * n + 4:]
    rows = x_ref.shape[1]
    outs, lses = [], []
    si = 0
    for d, o_ref, l_ref in zip(dilations, o_refs, l_refs):
        if d == 1:
            outs.append(o_ref[0, 0].astype(F32))
            lses.append(l_ref[0, 0])
            continue
        o_s, l_s = scratch[si], scratch[si + 1]
        si += 2
        n_tiles = o_s.shape[0]
        for c in range(d):
            piece = o_ref[0, c].astype(F32)
            for j in range(n_tiles):
                o_s[j, pl.ds(c, rows // d, stride=d), :] = piece[:, j * LANES:(j + 1) * LANES]
            l_s[pl.ds(c, rows // d, stride=d), :] = l_ref[0, c]
        outs.append(jnp.concatenate([o_s[j] for j in range(n_tiles)], axis=1))
        lses.append(l_s[...])
    lane = lax.broadcasted_iota(jnp.int32, lses[0].shape, 1)
    mx = functools.reduce(jnp.maximum, lses)
    es = [jnp.exp2(l - mx) for l in lses]
    sums = [pltpu.roll(l, LANES - DSW_HEADS_PER_GROUP, 1) for l in lses]
    total = functools.reduce(lambda a, b: a + b, [e * s for e, s in zip(es, sums)])
    merged = jnp.zeros(outs[0].shape, F32)
    for e, o in zip(es, outs):
        wgt = jnp.where(lane < DSW_HEADS_PER_GROUP, e / total, 0.0)
        hi = wgt.astype(BF16)
        lo = (wgt - hi.astype(F32)).astype(BF16)
        spread = (jnp.dot(hi, e_ref[...], preferred_element_type=F32)
                  + jnp.dot(lo, e_ref[...], preferred_element_type=F32))
        merged = merged + spread * o
    out_ref[0] = x_ref[0] + jnp.dot(merged.astype(BF16), wo_ref[...].astype(BF16),
                                    preferred_element_type=F32)


def _merge_wo(os, lses, x, wo, dilations, tm):
    batch, seq, d_model = x.shape
    width = os[0].shape[-1]
    head_of_lane = jnp.arange(width) // HEAD_DIM
    expand = (jnp.arange(LANES)[:, None] == head_of_lane[None, :]).astype(BF16)
    const = lambda b, t: (0, 0)
    blk = lambda d, w: pl.BlockSpec((1, d, tm // d, w), lambda b, t: (b, 0, t, 0))
    x_spec = pl.BlockSpec((1, tm, d_model), lambda b, t: (b, t, 0))
    scratch = []
    for d in dilations:
        if d > 1:
            scratch += [pltpu.VMEM((width // LANES, tm, LANES), F32), pltpu.VMEM((tm, LANES), F32)]
    return pl.pallas_call(
        functools.partial(_merge_wo_kernel, dilations=tuple(dilations)),
        grid=(batch, seq // tm),
        in_specs=([blk(d, width) for d in dilations] + [blk(d, LANES) for d in dilations]
                  + [x_spec, pl.BlockSpec(wo.shape, const), pl.BlockSpec(expand.shape, const)]),
        out_specs=x_spec,
        out_shape=jax.ShapeDtypeStruct(x.shape, F32),
        scratch_shapes=scratch,
        compiler_params=_params(2),
        name="dsw_merge_wo",
    )(*os, *lses, x, wo, expand)


def _ffn_kernel(*refs, has_attn):
    if has_attn:
        h_ref, a_ref, wo_ref, g_ref, w1_ref, w2_ref, out_ref = refs
        h = h_ref[...] + jnp.dot(a_ref[...], wo_ref[...].astype(BF16), preferred_element_type=F32)
    else:
        h_ref, g_ref, w1_ref, w2_ref, out_ref = refs
        h = h_ref[...]
    u = _rmsnorm_bf16(h, g_ref[...])
    a = jnp.dot(u, w1_ref[0].astype(BF16), preferred_element_type=F32)
    a = jnp.square(jnp.maximum(a, 0.0)).astype(BF16)
    out_ref[...] = h + jnp.dot(a, w2_ref[0].astype(BF16), preferred_element_type=F32)


def _ffn(h, gain, w1, w2, layer, tm, attn=None, wo=None, name="ffn"):
    n_rows, d_model = h.shape
    const = lambda i: (0, 0)
    row_spec = pl.BlockSpec((tm, d_model), lambda i: (i, 0))
    in_specs = [row_spec]
    args = [h]
    if attn is not None:
        in_specs += [pl.BlockSpec((tm, attn.shape[1]), lambda i: (i, 0)), pl.BlockSpec(wo.shape, const)]
        args += [attn, wo]
    in_specs += [pl.BlockSpec((1, d_model), const),
                 pl.BlockSpec((1,) + w1.shape[1:], lambda i: (layer, 0, 0), pipeline_mode=pl.Buffered(1)),
                 pl.BlockSpec((1,) + w2.shape[1:], lambda i: (layer, 0, 0), pipeline_mode=pl.Buffered(1))]
    args += [gain, w1, w2]
    return pl.pallas_call(
        functools.partial(_ffn_kernel, has_attn=attn is not None),
        grid=(n_rows // tm,),
        in_specs=in_specs,
        out_specs=row_spec,
        out_shape=jax.ShapeDtypeStruct(h.shape, F32),
        compiler_params=_params(1),
        name=name,
    )(*args)


def _moba_kernel(q_ref, k_ref, v_ref, wn_ref, o_ref,
                 bias_s, kp_s, vt_s, qa_s, s_s, acc_s, *, n_blocks, chunk):
    b = pl.program_id(1)
    blk = MOBA_BLOCK
    seq = n_blocks * blk
    span = chunk * blk
    lane = lax.broadcasted_iota(jnp.int32, (blk, LANES), 1)
    low_lanes = lane < HEAD_DIM
    v_rows = vt_s.shape[1]
    ones_row = (lax.broadcasted_iota(jnp.int32, (v_rows - HEAD_DIM, blk), 0) == 0).astype(F32)

    @pl.when(b == 0)
    def _build_bias():
        for hh in range(2):
            for dlt in range(n_blocks):
                u = jnp.concatenate([wn_ref[0, hh:hh + 1, (dlt + 1) * blk:(dlt + 2) * blk],
                                     wn_ref[0, hh:hh + 1, dlt * blk:(dlt + 1) * blk]], axis=1)
                r = n_blocks - 1 - dlt
                bias_s[hh, r * blk:(r + 1) * blk, :] = _toeplitz(u, blk)[:, :blk] * LOG2E
            bias_s[hh, seq:, :] = jnp.full((span - blk, blk), NEG, F32)

    sub = lax.broadcasted_iota(jnp.int32, (n_blocks, LANES), 0)
    kmean = jnp.zeros((n_blocks, LANES), F32)
    kamax = []
    for j in range(n_blocks):
        rows = slice(j * blk, (j + 1) * blk)
        kj = k_ref[0, rows, :].astype(F32)
        vj_t = v_ref[0, rows, :].astype(F32).T
        kmean = jnp.where(sub == j, jnp.mean(kj, axis=0, keepdims=True), kmean)
        kamax.append(jnp.max(jnp.abs(kj), axis=0, keepdims=True))
        aux0 = jnp.logical_or(lane == HEAD_DIM + j, lane == HEAD_DIM + n_blocks).astype(F32)
        aux1 = jnp.logical_or(lane == j, lane == n_blocks).astype(F32)
        kp_s[0, rows, :] = jnp.where(low_lanes, kj, aux0).astype(BF16)
        kp_s[1, rows, :] = jnp.where(low_lanes, aux1, kj).astype(BF16)
        for hh in range(2):
            vt_s[hh, :, rows] = jnp.concatenate(
                [vj_t[hh * HEAD_DIM:(hh + 1) * HEAD_DIM], ones_row], axis=0).astype(BF16)
    km_hi = kmean.astype(BF16)
    km_lo = (kmean - km_hi.astype(F32)).astype(BF16)
    kamax_upto = [kamax[0]]
    for j in range(1, n_blocks):
        kamax_upto.append(jnp.maximum(kamax_upto[-1], kamax[j]))
    pad_rows = jnp.zeros((6, LANES), F32)
    bias_max = [jnp.max(wn_ref[0, hh:hh + 1, blk:], axis=1, keepdims=True) * LOG2E for hh in range(2)]
    bias_zero = [wn_ref[0, hh:hh + 1, blk:blk + 1] * LOG2E for hh in range(2)]

    blk_row = lax.broadcasted_iota(jnp.int32, (n_blocks, blk), 0)
    blk_row_f = blk_row.astype(F32)
    gap = jnp.full((1, blk), -jnp.inf, F32)
    for n in range(n_blocks):
        rows = slice(n * blk, (n + 1) * blk)
        q2 = q_ref[0, rows, :]
        q_abs = jnp.abs(q2)
        k_bounds = jnp.concatenate([kamax_upto[n], kamax[n], pad_rows], axis=0).astype(BF16)
        for hh in range(2):
            own = low_lanes if hh == 0 else jnp.logical_not(low_lanes)
            qm = jnp.where(own, q2, jnp.zeros_like(q2))
            gate = (lax.dot_general(km_hi, qm, NT_DIMS, preferred_element_type=F32)
                    + lax.dot_general(km_lo, qm, NT_DIMS, preferred_element_type=F32))
            gate = jnp.where(blk_row < n, gate, -jnp.inf)
            chosen = blk_row == n
            for _ in range(min(MOBA_TOPK, n)):
                best = jnp.max(gate, axis=0, keepdims=True)
                cand = jnp.where(gate == best, blk_row_f, float(n_blocks))
                first = jnp.min(cand, axis=0, keepdims=True)
                pick = blk_row_f == first
                chosen = jnp.logical_or(chosen, pick)
                gate = jnp.where(pick, -jnp.inf, gate)
            pen_t = jnp.where(chosen, 0.0, NEG)
            sums = lax.dot_general(k_bounds, jnp.where(own, q_abs, jnp.zeros_like(q_abs)),
                                   NT_DIMS, preferred_element_type=F32)
            bound = sums[0:1] + bias_max[hh] + 1.0
            gap = jnp.maximum(gap, bound - (bias_zero[hh] - sums[1:2]))
            pad_lo = HEAD_DIM if hh == 0 else 0
            pieces = [jnp.zeros((pad_lo, blk), F32)] if pad_lo else []
            pieces += [pen_t, -bound, jnp.zeros((LANES - pad_lo - n_blocks - 1, blk), F32)]
            pen = jnp.concatenate(pieces, axis=0).T
            qa_s[hh, rows, :] = jnp.where(own, q2, pen.astype(BF16))

    def emit(n, width):
        cols = width * blk
        o_t = jnp.concatenate([acc_s[hh, :HEAD_DIM, :cols] / acc_s[hh, HEAD_DIM:HEAD_DIM + 1, :cols]
                               for hh in range(2)], axis=0)
        o_ref[0, pl.ds(_aligned(n * blk, blk), cols), :] = o_t.T.astype(o_ref.dtype)

    def raw_scores(hh, n, key0, n_keys, width):
        q_aug = qa_s[hh, pl.ds(_aligned(n * blk, blk), width * blk), :]
        s_t = lax.dot_general(kp_s[hh, key0:key0 + n_keys, :], q_aug, NT_DIMS,
                              preferred_element_type=F32)
        bias = [bias_s[hh, pl.ds(_aligned((n_blocks - 1 - n - i) * blk + key0, blk), n_keys), :]
                for i in range(width)]
        return s_t + (bias[0] if width == 1 else jnp.concatenate(bias, axis=1))

    acc_s[...] = jnp.ones(acc_s.shape, F32)
    n_groups = n_blocks // chunk
    bound_is_tight = jnp.max(gap) < EXP_RANGE
    one = jnp.minimum(b + 1, 1)

    @pl.when(bound_is_tight)
    def _single_pass():
        def q_pair(pair):
            n = 2 * pair
            wide = 4 * span
            own_chunk, own_blocks = divmod(n + 2, 4 * chunk)
            emit(jnp.maximum(n - 2, 0), 2)
            ranges = [(c * wide, wide) for c in range(own_chunk)]
            if own_blocks:
                ranges.append((own_chunk * wide, own_blocks * blk))
            for hh in range(2):
                acc = None
                for key0, n_keys in ranges:
                    p_t = jnp.exp2(raw_scores(hh, n, key0, n_keys, 2)).astype(BF16)
                    part = jnp.dot(vt_s[hh, :, key0:key0 + n_keys], p_t,
                                   preferred_element_type=F32)
                    acc = part if acc is None else acc + part
                acc_s[hh] = acc

        for first in range(0, n_blocks // 2, MOBA_PAIRS_PER_REGION):
            def region(t, c, first=first):
                for pair in range(first, first + MOBA_PAIRS_PER_REGION):
                    q_pair(pair)
                return c
            lax.fori_loop(0, one, region, 0)
        emit(n_blocks - 2, 2)

    @pl.when(jnp.logical_not(bound_is_tight))
    def _two_pass():
        def scores(hh, n, n_chunks):
            m = None
            for c in range(n_chunks):
                s_t = raw_scores(hh, n, c * span, span, 1)
                s_s[hh, c * span:(c + 1) * span, :] = s_t
                cm = jnp.max(s_t, axis=0, keepdims=True)
                m = cm if m is None else jnp.maximum(m, cm)
            return m

        def weighted_values(hh, m, n_chunks):
            n_keys = n_chunks * span
            p_t = jnp.exp2(s_s[hh, :n_keys, :] - m).astype(BF16)
            return jnp.dot(vt_s[hh, :, :n_keys], p_t, preferred_element_type=F32)

        def step(n, m0, n_chunks, next_chunks):
            emit(jnp.maximum(n - 1, 0), 1)
            acc0 = weighted_values(0, m0, n_chunks)
            m1 = scores(1, n, n_chunks)
            m0_next = scores(0, n + 1, next_chunks) if next_chunks else m0
            acc1 = weighted_values(1, m1, n_chunks)
            acc_s[0, :, :blk] = acc0
            acc_s[1, :, :blk] = acc1
            return m0_next

        m0 = scores(0, 0, 1)
        for g in range(n_groups):
            m0 = lax.fori_loop(0, chunk - 1,
                               lambda t, m, g=g: step(g * chunk + t, m, g + 1, g + 1), m0)
            m0 = step(g * chunk + chunk - 1, m0, g + 1, g + 2 if g + 1 < n_groups else 0)
        emit(n_blocks - 1, 1)


def _moba_attention(q, k, v, moba_vec, batch, seq, chunk):
    n_rows, width = q.shape
    n_pairs = width // LANES
    n_blocks = seq // MOBA_BLOCK
    assert n_blocks % chunk == 0
    wn = moba_vec.reshape(n_pairs, 2, moba_vec.shape[1])
    spec = pl.BlockSpec((1, seq, LANES), lambda p, b: (b, 0, p))
    shape3 = (batch, seq, width)
    out = pl.pallas_call(
        functools.partial(_moba_kernel, n_blocks=n_blocks, chunk=chunk),
        grid=(n_pairs, batch),
        in_specs=[spec, spec, spec, pl.BlockSpec((1, 2, wn.shape[2]), lambda p, b: (p, 0, 0))],
        out_specs=spec,
        out_shape=jax.ShapeDtypeStruct(shape3, BF16),
        scratch_shapes=[pltpu.VMEM((2, seq + (chunk - 1) * MOBA_BLOCK, MOBA_BLOCK), F32),
                        pltpu.VMEM((2, seq, LANES), BF16),
                        pltpu.VMEM((2, HEAD_DIM + BF16_SUBLANES, seq), BF16),
                        pltpu.VMEM((2, seq, LANES), BF16),
                        pltpu.VMEM((2, seq, MOBA_BLOCK), F32),
                        pltpu.VMEM((2, HEAD_DIM + BF16_SUBLANES, 2 * MOBA_BLOCK), F32)],
        compiler_params=_params(2),
        name="moba_attention",
    )(q.reshape(shape3), k.reshape(shape3), v.reshape(shape3), wn)
    return out.reshape(n_rows, width)


def _gain_row(gain, size, scale):
    return (jnp.tile(gain.astype(F32), size // HEAD_DIM) * scale).reshape(1, size)


def kernel(x, rel_bias, norm_mix, norm_ffn, a_w_qkv, a_q_gain, a_k_gain, a_w_o,
           b_w_qkv, b_q_gain, b_k_gain, b_w_o, ffn_w1, ffn_w2):
    batch, seq, d_model = x.shape
    n_rows = batch * seq
    n_groups = len(DSW_GROUPS)
    dilations = [d for _, d in DSW_GROUPS]
    gw = DSW_HEADS_PER_GROUP * HEAD_DIM
    moba_heads = b_w_o.shape[1] // HEAD_DIM

    dsw_vec, moba_vec = _bias_tables(rel_bias, moba_heads, seq)
    dsw_tiles = _dsw_bias_tiles(dsw_vec)

    qkv = _qkv_project(x, norm_mix[0].reshape(1, d_model), a_w_qkv[0],
                       _gain_row(a_q_gain[0], LANES, SCALE * LOG2E), _gain_row(a_k_gain[0], LANES, 1.0),
                       dilations, tm=ROWS_PER_STEP, name="dsw_qkv")
    os, lses = [], []
    for g, d in enumerate(dilations):
        q_g, k_g, v_g = (a.reshape(n_rows, gw) for a in qkv[g])
        o_g, lse_g = _dsw_attention(q_g, k_g, v_g, dsw_tiles, g, tq=DSW_ROWS_PER_STEP,
                                    blocks_per_seq=seq // d // DSW_BLK)
        os.append(o_g.reshape(batch, d, seq // d, gw))
        lses.append(lse_g.reshape(batch, d, seq // d, LANES))
    h = _merge_wo(os, lses, x, a_w_o[0], dilations, tm=ROWS_PER_STEP)
    h = h.reshape(n_rows, d_model)
    w1, w2 = ffn_w1, ffn_w2
    h = _ffn(h, norm_ffn[0].reshape(1, d_model), w1, w2, 0, tm=ROWS_PER_STEP, name="ffn0")

    width = moba_heads * HEAD_DIM
    (q, k, v), = _qkv_project(
        h.reshape(batch, seq, d_model), norm_mix[1].reshape(1, d_model), b_w_qkv[0],
        _gain_row(b_q_gain[0], LANES, SCALE * LOG2E), _gain_row(b_k_gain[0], LANES, 1.0),
        [1], tm=ROWS_PER_STEP, name="moba_qkv")
    q, k, v = (a.reshape(n_rows, width) for a in (q, k, v))
    attn = _moba_attention(q, k, v, moba_vec, batch, seq, chunk=MOBA_CHUNK_BLOCKS)
    h = _ffn(h, norm_ffn[1].reshape(1, d_model), w1, w2, 1,
             tm=ROWS_PER_STEP, attn=attn, wo=b_w_o[0], name="wo_ffn1")
    return h.reshape(batch, seq, d_model)
```

```python
import functools
import math

import jax
import jax.numpy as jnp
from jax import lax
from jax.experimental import pallas as pl
from jax.experimental.pallas import tpu as pltpu

HEAD_DIM = 64
LANES = 128
DSW_GROUPS = ((128, 1), (512, 4), (2048, 16))
DSW_BLK = 128
DSW_HEADS_PER_GROUP = 8
MOBA_BLOCK = 256
MOBA_TOPK = 3
REL_BUCKETS = 32
REL_MAX_DISTANCE = 2048
EPS = 1e-6
NEG = -1e30
SCALE = HEAD_DIM ** -0.5
LOG2E = 1.4426950408889634
EXP_RANGE = 100.0
VMEM_LIMIT_BYTES = 56 * 1024 * 1024
ROWS_PER_STEP = 512
WIDE_ROWS_PER_STEP = 1024
DSW_ROWS_PER_STEP = 1024
MOBA_CHUNK_BLOCKS = 4
MOBA_PAIRS_PER_REGION = 4
BF16_SUBLANES = 16

F32 = jnp.float32
BF16 = jnp.bfloat16
NT_DIMS = (((1,), (1,)), ((), ()))


def _params(n_axes):
    return pltpu.CompilerParams(dimension_semantics=("arbitrary",) * n_axes,
                                vmem_limit_bytes=VMEM_LIMIT_BYTES)


def _aligned(start, multiple):
    return start if isinstance(start, int) else pl.multiple_of(start, multiple)


def _t5_bucket(dist):
    n = jnp.maximum(dist, 0)
    max_exact = REL_BUCKETS // 2
    nf = jnp.maximum(n, 1).astype(F32)
    large = max_exact + jnp.floor(jnp.log(nf / max_exact) / math.log(REL_MAX_DISTANCE / max_exact)
                                  * (REL_BUCKETS - max_exact)).astype(jnp.int32)
    large = jnp.minimum(large, REL_BUCKETS - 1)
    return jnp.where(n < max_exact, n, large)


def _lookup(bucket, table_t):
    acc = jnp.zeros(bucket.shape, F32)
    for b in range(REL_BUCKETS):
        acc = jnp.where(bucket == b, table_t[:, b:b + 1], acc)
    return acc


def _bias_tables_kernel(tab_ref, dsw_ref, moba_ref, *, n_moba_heads, seq):
    tab = tab_ref[...]
    n_heads = dsw_ref.shape[0]
    m = lax.broadcasted_iota(jnp.int32, (n_heads, 2 * DSW_BLK), 1)
    row = lax.broadcasted_iota(jnp.int32, (n_heads, 2 * DSW_BLK), 0)
    dil = jnp.where(row < DSW_HEADS_PER_GROUP, DSW_GROUPS[0][1],
                    jnp.where(row < 2 * DSW_HEADS_PER_GROUP, DSW_GROUPS[1][1], DSW_GROUPS[2][1]))
    sub = DSW_BLK - m
    vals = _lookup(_t5_bucket(sub * dil), tab)
    dsw_ref[...] = jnp.where(sub >= 0, vals * LOG2E, NEG)
    width = moba_ref.shape[1]
    t = lax.broadcasted_iota(jnp.int32, (n_moba_heads, width), 1)
    dist = t - MOBA_BLOCK
    vals = _lookup(_t5_bucket(dist), tab[:n_moba_heads])
    moba_ref[...] = jnp.where(dist >= 0, vals, NEG)


def _bias_tables(rel_bias, n_moba_heads, seq):
    n_heads = rel_bias.shape[1]
    width = seq + MOBA_BLOCK
    return pl.pallas_call(
        functools.partial(_bias_tables_kernel, n_moba_heads=n_moba_heads, seq=seq),
        out_shape=(jax.ShapeDtypeStruct((n_heads, 2 * DSW_BLK), F32),
                   jax.ShapeDtypeStruct((n_moba_heads, width), F32)),
        name="bias_tables",
    )(rel_bias.T)


def _toeplitz(u_row, rows):
    x = jnp.broadcast_to(u_row, (rows, u_row.shape[1]))
    return pltpu.roll(x, 0, 1, stride=1, stride_axis=0)


def _dsw_bias_kernel(vec_ref, out_ref):
    lane = lax.broadcasted_iota(jnp.int32, (DSW_BLK, 2 * DSW_BLK), 1)
    for h in range(DSW_HEADS_PER_GROUP):
        tile = _toeplitz(vec_ref[0, h:h + 1, :], DSW_BLK)
        out_ref[0, h, 0] = tile
        out_ref[0, h, 1] = jnp.where(lane < DSW_BLK, NEG, tile)


def _dsw_bias_tiles(dsw_vec):
    n_groups = len(DSW_GROUPS)
    hg = DSW_HEADS_PER_GROUP
    vec = dsw_vec.reshape(n_groups, hg, 2 * DSW_BLK)
    return pl.pallas_call(
        _dsw_bias_kernel,
        grid=(n_groups,),
        in_specs=[pl.BlockSpec((1, hg, 2 * DSW_BLK), lambda g: (g, 0, 0))],
        out_specs=pl.BlockSpec((1, hg, 2, DSW_BLK, 2 * DSW_BLK), lambda g: (g, 0, 0, 0, 0)),
        out_shape=jax.ShapeDtypeStruct((n_groups, hg, 2, DSW_BLK, 2 * DSW_BLK), F32),
        compiler_params=_params(1),
        name="dsw_bias_tiles",
    )(vec)


def _rmsnorm_bf16(x, gain):
    ms = jnp.mean(x * x, axis=-1, keepdims=True)
    return (x * lax.rsqrt(ms + EPS) * gain).astype(BF16)


def _qkv_kernel(x_ref, g_ref, w_ref, qg_ref, kg_ref, *rest, width, dilations):
    n_groups = len(dilations)
    out_refs = rest[:3 * n_groups]
    u_s = rest[3 * n_groups]
    x = x_ref[0]
    rows = x.shape[0]
    ms = jnp.mean(x * x, axis=-1, keepdims=True)
    u = x * lax.rsqrt(ms + EPS) * g_ref[...]
    n_tiles = x.shape[1] // LANES
    if any(d > 1 for d in dilations):
        for j in range(n_tiles):
            u_s[j] = u[:, j * LANES:(j + 1) * LANES]
    low_lanes = lax.broadcasted_iota(jnp.int32, (rows, LANES), 1) < HEAD_DIM
    for gi, d in enumerate(dilations):
        per = rows // d
        if d == 1:
            lhs = u.astype(BF16)
        else:
            lhs = jnp.concatenate(
                [jnp.concatenate([u_s[j, pl.ds(c, per, stride=d), :] for j in range(n_tiles)], axis=1)
                 for c in range(d)], axis=0).astype(BF16)
        q_ref, k_ref, v_ref = out_refs[3 * gi:3 * gi + 3]
        for part, (gain_ref, out_ref) in enumerate(((qg_ref, q_ref), (kg_ref, k_ref), (None, v_ref))):
            col = (part * n_groups + gi) * width
            y_part = jnp.dot(lhs, w_ref[:, col:col + width].astype(BF16), preferred_element_type=F32)
            for cc in range(width // LANES):
                y = y_part[:, cc * LANES:(cc + 1) * LANES]
                if gain_ref is not None:
                    sq = y * y
                    both = jnp.sum(sq, axis=-1, keepdims=True)
                    low = jnp.sum(jnp.where(low_lanes, sq, 0.0), axis=-1, keepdims=True)
                    msq = jnp.where(low_lanes, low, both - low) * (1.0 / HEAD_DIM)
                    y = y * lax.rsqrt(msq + EPS) * gain_ref[...]
                y = y.astype(BF16)
                for c in range(d):
                    out_ref[0, c, :, cc * LANES:(cc + 1) * LANES] = y[c * per:(c + 1) * per]


def _qkv_project(x, gain, w, q_gain_row, k_gain_row, dilations, tm, name):
    batch, seq, d_model = x.shape
    width = w.shape[1] // (3 * len(dilations))
    const = lambda b, t: (0, 0)
    out_specs, out_shapes = [], []
    for d in dilations:
        out_specs += [pl.BlockSpec((1, d, tm // d, width), lambda b, t: (b, 0, t, 0))] * 3
        out_shapes += [jax.ShapeDtypeStruct((batch, d, seq // d, width), BF16)] * 3
    outs = pl.pallas_call(
        functools.partial(_qkv_kernel, width=width, dilations=tuple(dilations)),
        grid=(batch, seq // tm),
        in_specs=[pl.BlockSpec((1, tm, d_model), lambda b, t: (b, t, 0)),
                  pl.BlockSpec((1, d_model), const),
                  pl.BlockSpec(w.shape, const, pipeline_mode=pl.Buffered(1)),
                  pl.BlockSpec(q_gain_row.shape, const),
                  pl.BlockSpec(k_gain_row.shape, const)],
        out_specs=tuple(out_specs),
        out_shape=tuple(out_shapes),
        scratch_shapes=[pltpu.VMEM((d_model // LANES, tm, LANES), F32)],
        compiler_params=_params(2),
        name=name,
    )(x, gain, w, q_gain_row, k_gain_row)
    return [tuple(outs[3 * i:3 * i + 3]) for i in range(len(dilations))]


def _dsw_attn_kernel(q_ref, kc_ref, vc_ref, kp_ref, vp_ref, bias_ref, o_ref, stat_ref, *,
                     tq, blocks_per_seq):
    t = pl.program_id(0)
    n_blk = tq // DSW_BLK
    lane = lax.broadcasted_iota(jnp.int32, (DSW_BLK, LANES), 1)
    low_half = lane < HEAD_DIM
    n_pairs = q_ref.shape[1] // LANES
    for qi in range(n_blk):
        rows = slice(qi * DSW_BLK, (qi + 1) * DSW_BLK)
        first = (jnp.bitwise_and(t * n_blk + qi, blocks_per_seq - 1) == 0).astype(jnp.int32)
        stat_tile = jnp.zeros((DSW_BLK, LANES), F32)
        for hp in range(n_pairs):
            cols = slice(hp * LANES, (hp + 1) * LANES)
            q2 = q_ref[rows, cols]
            if qi == 0:
                k_prev, v_prev = kp_ref[:, cols], vp_ref[:, cols]
            else:
                prev = slice((qi - 1) * DSW_BLK, qi * DSW_BLK)
                k_prev, v_prev = kc_ref[prev, cols], vc_ref[prev, cols]
            k_cat = jnp.concatenate([k_prev, kc_ref[rows, cols]], axis=0)
            v_cat = jnp.concatenate([v_prev, vc_ref[rows, cols]], axis=0)
            outs = []
            for hh in range(2):
                head_lanes = low_half if hh == 0 else jnp.logical_not(low_half)
                qm = jnp.where(head_lanes, q2, jnp.zeros_like(q2))
                s = lax.dot_general(qm, k_cat, NT_DIMS, preferred_element_type=F32)
                s = s + bias_ref[0, hp * 2 + hh, first]
                m = jnp.max(s, axis=-1, keepdims=True)
                p = jnp.exp2(s - m)
                den = jnp.sum(p, axis=-1, keepdims=True)
                outs.append(jnp.dot(p.astype(BF16), v_cat, preferred_element_type=F32))
                head = hp * 2 + hh
                stat_tile = jnp.where(lane == head, m,
                                      jnp.where(lane == DSW_HEADS_PER_GROUP + head, den, stat_tile))
            o_ref[rows, cols] = jnp.where(low_half, outs[0], outs[1]).astype(o_ref.dtype)
        stat_ref[rows, :] = stat_tile


def _dsw_attention(q, k, v, bias_tiles, group, tq, blocks_per_seq):
    n_rows, width = q.shape
    assert blocks_per_seq & (blocks_per_seq - 1) == 0
    n_blk = tq // DSW_BLK
    cur = pl.BlockSpec((tq, width), lambda t: (t, 0))
    prev = pl.BlockSpec((DSW_BLK, width), lambda t: (jnp.maximum(t * n_blk - 1, 0), 0))
    hg = DSW_HEADS_PER_GROUP
    return pl.pallas_call(
        functools.partial(_dsw_attn_kernel, tq=tq, blocks_per_seq=blocks_per_seq),
        grid=(n_rows // tq,),
        in_specs=[cur, cur, cur, prev, prev,
                  pl.BlockSpec((1, hg, 2, DSW_BLK, 2 * DSW_BLK), lambda t: (group, 0, 0, 0, 0))],
        out_specs=(pl.BlockSpec((tq, width), lambda t: (t, 0)),
                   pl.BlockSpec((tq, LANES), lambda t: (t, 0))),
        out_shape=(jax.ShapeDtypeStruct((n_rows, width), BF16),
                   jax.ShapeDtypeStruct((n_rows, LANES), F32)),
        compiler_params=_params(1),
        name=f"dsw_attention_g{group}",
    )(q, k, v, k, v, bias_tiles)


def _merge_wo_kernel(*refs, dilations):
    n = len(dilations)
    o_refs, l_refs = refs[:n], refs[n:2 * n]
    x_ref, wo_ref, e_ref, out_ref = refs[2 * n:2 * n + 4]
    scratch = refs[2 * n + 4:]
    rows = x_ref.shape[1]
    outs, lses = [], []
    si = 0
    for d, o_ref, l_ref in zip(dilations, o_refs, l_refs):
        if d == 1:
            outs.append(o_ref[0, 0].astype(F32))
            lses.append(l_ref[0, 0])
            continue
        o_s, l_s = scratch[si], scratch[si + 1]
        si += 2
        n_tiles = o_s.shape[0]
        for c in range(d):
            piece = o_ref[0, c].astype(F32)
            for j in range(n_tiles):
                o_s[j, pl.ds(c, rows // d, stride=d), :] = piece[:, j * LANES:(j + 1) * LANES]
            l_s[pl.ds(c, rows // d, stride=d), :] = l_ref[0, c]
        outs.append(jnp.concatenate([o_s[j] for j in range(n_tiles)], axis=1))
        lses.append(l_s[...])
    lane = lax.broadcasted_iota(jnp.int32, lses[0].shape, 1)
    mx = functools.reduce(jnp.maximum, lses)
    es = [jnp.exp2(l - mx) for l in lses]
    sums = [pltpu.roll(l, LANES - DSW_HEADS_PER_GROUP, 1) for l in lses]
    total = functools.reduce(lambda a, b: a + b, [e * s for e, s in zip(es, sums)])
    merged = jnp.zeros(outs[0].shape, F32)
    for e, o in zip(es, outs):
        wgt = jnp.where(lane < DSW_HEADS_PER_GROUP, e / total, 0.0)
        hi = wgt.astype(BF16)
        lo = (wgt - hi.astype(F32)).astype(BF16)
        spread = (jnp.dot(hi, e_ref[...], preferred_element_type=F32)
                  + jnp.dot(lo, e_ref[...], preferred_element_type=F32))
        merged = merged + spread * o
    out_ref[0] = x_ref[0] + jnp.dot(merged.astype(BF16), wo_ref[...].astype(BF16),
                                    preferred_element_type=F32)


def _merge_wo(os, lses, x, wo, dilations, tm):
    batch, seq, d_model = x.shape
    width = os[0].shape[-1]
    head_of_lane = jnp.arange(width) // HEAD_DIM
    expand = (jnp.arange(LANES)[:, None] == head_of_lane[None, :]).astype(BF16)
    const = lambda b, t: (0, 0)
    blk = lambda d, w: pl.BlockSpec((1, d, tm // d, w), lambda b, t: (b, 0, t, 0))
    x_spec = pl.BlockSpec((1, tm, d_model), lambda b, t: (b, t, 0))
    scratch = []
    for d in dilations:
        if d > 1:
            scratch += [pltpu.VMEM((width // LANES, tm, LANES), F32), pltpu.VMEM((tm, LANES), F32)]
    return pl.pallas_call(
        functools.partial(_merge_wo_kernel, dilations=tuple(dilations)),
        grid=(batch, seq // tm),
        in_specs=([blk(d, width) for d in dilations] + [blk(d, LANES) for d in dilations]
                  + [x_spec, pl.BlockSpec(wo.shape, const), pl.BlockSpec(expand.shape, const)]),
        out_specs=x_spec,
        out_shape=jax.ShapeDtypeStruct(x.shape, F32),
        scratch_shapes=scratch,
        compiler_params=_params(2),
        name="dsw_merge_wo",
    )(*os, *lses, x, wo, expand)


def _ffn_kernel(*refs, has_attn):
    if has_attn:
        h_ref, a_ref, wo_ref, g_ref, w1_ref, w2_ref, out_ref = refs
        h = h_ref[...] + jnp.dot(a_ref[...], wo_ref[...].astype(BF16), preferred_element_type=F32)
    else:
        h_ref, g_ref, w1_ref, w2_ref, out_ref = refs
        h = h_ref[...]
    u = _rmsnorm_bf16(h, g_ref[...])
    a = jnp.dot(u, w1_ref[0].astype(BF16), preferred_element_type=F32)
    a = jnp.square(jnp.maximum(a, 0.0)).astype(BF16)
    out_ref[...] = h + jnp.dot(a, w2_ref[0].astype(BF16), preferred_element_type=F32)


def _ffn(h, gain, w1, w2, layer, tm, attn=None, wo=None, name="ffn"):
    n_rows, d_model = h.shape
    const = lambda i: (0, 0)
    row_spec = pl.BlockSpec((tm, d_model), lambda i: (i, 0))
    in_specs = [row_spec]
    args = [h]
    if attn is not None:
        in_specs += [pl.BlockSpec((tm, attn.shape[1]), lambda i: (i, 0)), pl.BlockSpec(wo.shape, const)]
        args += [attn, wo]
    in_specs += [pl.BlockSpec((1, d_model), const),
                 pl.BlockSpec((1,) + w1.shape[1:], lambda i: (layer, 0, 0), pipeline_mode=pl.Buffered(1)),
                 pl.BlockSpec((1,) + w2.shape[1:], lambda i: (layer, 0, 0), pipeline_mode=pl.Buffered(1))]
    args += [gain, w1, w2]
    return pl.pallas_call(
        functools.partial(_ffn_kernel, has_attn=attn is not None),
        grid=(n_rows // tm,),
        in_specs=in_specs,
        out_specs=row_spec,
        out_shape=jax.ShapeDtypeStruct(h.shape, F32),
        compiler_params=_params(1),
        name=name,
    )(*args)


def _moba_kernel(q_ref, k_ref, v_ref, wn_ref, o_ref,
                 bias_s, kp_s, vt_s, qa_s, s_s, acc_s, *, n_blocks, chunk):
    b = pl.program_id(1)
    blk = MOBA_BLOCK
    seq = n_blocks * blk
    span = chunk * blk
    lane = lax.broadcasted_iota(jnp.int32, (blk, LANES), 1)
    low_lanes = lane < HEAD_DIM
    v_rows = vt_s.shape[1]
    ones_row = (lax.broadcasted_iota(jnp.int32, (v_rows - HEAD_DIM, blk), 0) == 0).astype(F32)

    @pl.when(b == 0)
    def _build_bias():
        for hh in range(2):
            for dlt in range(n_blocks):
                u = jnp.concatenate([wn_ref[0, hh:hh + 1, (dlt + 1) * blk:(dlt + 2) * blk],
                                     wn_ref[0, hh:hh + 1, dlt * blk:(dlt + 1) * blk]], axis=1)
                r = n_blocks - 1 - dlt
                bias_s[hh, r * blk:(r + 1) * blk, :] = _toeplitz(u, blk)[:, :blk] * LOG2E
            bias_s[hh, seq:, :] = jnp.full((span - blk, blk), NEG, F32)

    sub = lax.broadcasted_iota(jnp.int32, (n_blocks, LANES), 0)
    kmean = jnp.zeros((n_blocks, LANES), F32)
    kamax = []
    for j in range(n_blocks):
        rows = slice(j * blk, (j + 1) * blk)
        kj = k_ref[0, rows, :].astype(F32)
        vj_t = v_ref[0, rows, :].astype(F32).T
        kmean = jnp.where(sub == j, jnp.mean(kj, axis=0, keepdims=True), kmean)
        kamax.append(jnp.max(jnp.abs(kj), axis=0, keepdims=True))
        aux0 = jnp.logical_or(lane == HEAD_DIM + j, lane == HEAD_DIM + n_blocks).astype(F32)
        aux1 = jnp.logical_or(lane == j, lane == n_blocks).astype(F32)
        kp_s[0, rows, :] = jnp.where(low_lanes, kj, aux0).astype(BF16)
        kp_s[1, rows, :] = jnp.where(low_lanes, aux1, kj).astype(BF16)
        for hh in range(2):
            vt_s[hh, :, rows] = jnp.concatenate(
                [vj_t[hh * HEAD_DIM:(hh + 1) * HEAD_DIM], ones_row], axis=0).astype(BF16)
    km_hi = kmean.astype(BF16)
    km_lo = (kmean - km_hi.astype(F32)).astype(BF16)
    kamax_upto = [kamax[0]]
    for j in range(1, n_blocks):
        kamax_upto.append(jnp.maximum(kamax_upto[-1], kamax[j]))
    pad_rows = jnp.zeros((6, LANES), F32)
    bias_max = [jnp.max(wn_ref[0, hh:hh + 1, blk:], axis=1, keepdims=True) * LOG2E for hh in range(2)]
    bias_zero = [wn_ref[0, hh:hh + 1, blk:blk + 1] * LOG2E for hh in range(2)]

    blk_row = lax.broadcasted_iota(jnp.int32, (n_blocks, blk), 0)
    blk_row_f = blk_row.astype(F32)
    gap = jnp.full((1, blk), -jnp.inf, F32)
    for n in range(n_blocks):
        rows = slice(n * blk, (n + 1) * blk)
        q2 = q_ref[0, rows, :]
        q_abs = jnp.abs(q2)
        k_bounds = jnp.concatenate([kamax_upto[n], kamax[n], pad_rows], axis=0).astype(BF16)
        for hh in range(2):
            own = low_lanes if hh == 0 else jnp.logical_not(low_lanes)
            qm = jnp.where(own, q2, jnp.zeros_like(q2))
            gate = (lax.dot_general(km_hi, qm, NT_DIMS, preferred_element_type=F32)
                    + lax.dot_general(km_lo, qm, NT_DIMS, preferred_element_type=F32))
            gate = jnp.where(blk_row < n, gate, -jnp.inf)
            chosen = blk_row == n
            for _ in range(min(MOBA_TOPK, n)):
                best = jnp.max(gate, axis=0, keepdims=True)
                cand = jnp.where(gate == best, blk_row_f, float(n_blocks))
                first = jnp.min(cand, axis=0, keepdims=True)
                pick = blk_row_f == first
                chosen = jnp.logical_or(chosen, pick)
                gate = jnp.where(pick, -jnp.inf, gate)
            pen_t = jnp.where(chosen, 0.0, NEG)
            sums = lax.dot_general(k_bounds, jnp.where(own, q_abs, jnp.zeros_like(q_abs)),
                                   NT_DIMS, preferred_element_type=F32)
            bound = sums[0:1] + bias_max[hh] + 1.0
            gap = jnp.maximum(gap, bound - (bias_zero[hh] - sums[1:2]))
            pad_lo = HEAD_DIM if hh == 0 else 0
            pieces = [jnp.zeros((pad_lo, blk), F32)] if pad_lo else []
            pieces += [pen_t, -bound, jnp.zeros((LANES - pad_lo - n_blocks - 1, blk), F32)]
            pen = jnp.concatenate(pieces, axis=0).T
            qa_s[hh, rows, :] = jnp.where(own, q2, pen.astype(BF16))

    def emit(n, width):
        cols = width * blk
        o_t = jnp.concatenate([acc_s[hh, :HEAD_DIM, :cols] / acc_s[hh, HEAD_DIM:HEAD_DIM + 1, :cols]
                               for hh in range(2)], axis=0)
        o_ref[0, pl.ds(_aligned(n * blk, blk), cols), :] = o_t.T.astype(o_ref.dtype)

    def raw_scores(hh, n, key0, n_keys, width):
        q_aug = qa_s[hh, pl.ds(_aligned(n * blk, blk), width * blk), :]
        s_t = lax.dot_general(kp_s[hh, key0:key0 + n_keys, :], q_aug, NT_DIMS,
                              preferred_element_type=F32)
        bias = [bias_s[hh, pl.ds(_aligned((n_blocks - 1 - n - i) * blk + key0, blk), n_keys), :]
                for i in range(width)]
        return s_t + (bias[0] if width == 1 else jnp.concatenate(bias, axis=1))

    acc_s[...] = jnp.ones(acc_s.shape, F32)
    n_groups = n_blocks // chunk
    bound_is_tight = jnp.max(gap) < EXP_RANGE
    one = jnp.minimum(b + 1, 1)

    @pl.when(bound_is_tight)
    def _single_pass():
        def q_pair(pair):
            n = 2 * pair
            wide = 4 * span
            own_chunk, own_blocks = divmod(n + 2, 4 * chunk)
            emit(jnp.maximum(n - 2, 0), 2)
            ranges = [(c * wide, wide) for c in range(own_chunk)]
            if own_blocks:
                ranges.append((own_chunk * wide, own_blocks * blk))
            for hh in range(2):
                acc = None
                for key0, n_keys in ranges:
                    p_t = jnp.exp2(raw_scores(hh, n, key0, n_keys, 2)).astype(BF16)
                    part = jnp.dot(vt_s[hh, :, key0:key0 + n_keys], p_t,
                                   preferred_element_type=F32)
                    acc = part if acc is None else acc + part
                acc_s[hh] = acc

        for first in range(0, n_blocks // 2, MOBA_PAIRS_PER_REGION):
            def region(t, c, first=first):
                for pair in range(first, first + MOBA_PAIRS_PER_REGION):
                    q_pair(pair)
                return c
            lax.fori_loop(0, one, region, 0)
        emit(n_blocks - 2, 2)

    @pl.when(jnp.logical_not(bound_is_tight))
    def _two_pass():
        def scores(hh, n, n_chunks):
            m = None
            for c in range(n_chunks):
                s_t = raw_scores(hh, n, c * span, span, 1)
                s_s[hh, c * span:(c + 1) * span, :] = s_t
                cm = jnp.max(s_t, axis=0, keepdims=True)
                m = cm if m is None else jnp.maximum(m, cm)
            return m

        def weighted_values(hh, m, n_chunks):
            n_keys = n_chunks * span
            p_t = jnp.exp2(s_s[hh, :n_keys, :] - m).astype(BF16)
            return jnp.dot(vt_s[hh, :, :n_keys], p_t, preferred_element_type=F32)

        def step(n, m0, n_chunks, next_chunks):
            emit(jnp.maximum(n - 1, 0), 1)
            acc0 = weighted_values(0, m0, n_chunks)
            m1 = scores(1, n, n_chunks)
            m0_next = scores(0, n + 1, next_chunks) if next_chunks else m0
            acc1 = weighted_values(1, m1, n_chunks)
            acc_s[0, :, :blk] = acc0
            acc_s[1, :, :blk] = acc1
            return m0_next

        m0 = scores(0, 0, 1)
        for g in range(n_groups):
            m0 = lax.fori_loop(0, chunk - 1,
                               lambda t, m, g=g: step(g * chunk + t, m, g + 1, g + 1), m0)
            m0 = step(g * chunk + chunk - 1, m0, g + 1, g + 2 if g + 1 < n_groups else 0)
        emit(n_blocks - 1, 1)


def _moba_attention(q, k, v, moba_vec, batch, seq, chunk):
    n_rows, width = q.shape
    n_pairs = width // LANES
    n_blocks = seq // MOBA_BLOCK
    assert n_blocks % chunk == 0
    wn = moba_vec.reshape(n_pairs, 2, moba_vec.shape[1])
    spec = pl.BlockSpec((1, seq, LANES), lambda p, b: (b, 0, p))
    shape3 = (batch, seq, width)
    out = pl.pallas_call(
        functools.partial(_moba_kernel, n_blocks=n_blocks, chunk=chunk),
        grid=(n_pairs, batch),
        in_specs=[spec, spec, spec, pl.BlockSpec((1, 2, wn.shape[2]), lambda p, b: (p, 0, 0))],
        out_specs=spec,
        out_shape=jax.ShapeDtypeStruct(shape3, BF16),
        scratch_shapes=[pltpu.VMEM((2, seq + (chunk - 1) * MOBA_BLOCK, MOBA_BLOCK), F32),
                        pltpu.VMEM((2, seq, LANES), BF16),
                        pltpu.VMEM((2, HEAD_DIM + BF16_SUBLANES, seq), BF16),
                        pltpu.VMEM((2, seq, LANES), BF16),
                        pltpu.VMEM((2, seq, MOBA_BLOCK), F32),
                        pltpu.VMEM((2, HEAD_DIM + BF16_SUBLANES, 2 * MOBA_BLOCK), F32)],
        compiler_params=_params(2),
        name="moba_attention",
    )(q.reshape(shape3), k.reshape(shape3), v.reshape(shape3), wn)
    return out.reshape(n_rows, width)


def _gain_row(gain, size, scale):
    return (jnp.tile(gain.astype(F32), size // HEAD_DIM) * scale).reshape(1, size)


def kernel(x, rel_bias, norm_mix, norm_ffn, a_w_qkv, a_q_gain, a_k_gain, a_w_o,
           b_w_qkv, b_q_gain, b_k_gain, b_w_o, ffn_w1, ffn_w2):
    batch, seq, d_model = x.shape
    n_rows = batch * seq
    n_groups = len(DSW_GROUPS)
    dilations = [d for _, d in DSW_GROUPS]
    gw = DSW_HEADS_PER_GROUP * HEAD_DIM
    moba_heads = b_w_o.shape[1] // HEAD_DIM

    dsw_vec, moba_vec = _bias_tables(rel_bias, moba_heads, seq)
    dsw_tiles = _dsw_bias_tiles(dsw_vec)

    qkv = _qkv_project(x, norm_mix[0].reshape(1, d_model), a_w_qkv[0],
                       _gain_row(a_q_gain[0], LANES, SCALE * LOG2E), _gain_row(a_k_gain[0], LANES, 1.0),
                       dilations, tm=ROWS_PER_STEP, name="dsw_qkv")
    os, lses = [], []
    for g, d in enumerate(dilations):
        q_g, k_g, v_g = (a.reshape(n_rows, gw) for a in qkv[g])
        o_g, lse_g = _dsw_attention(q_g, k_g, v_g, dsw_tiles, g, tq=DSW_ROWS_PER_STEP,
                                    blocks_per_seq=seq // d // DSW_BLK)
        os.append(o_g.reshape(batch, d, seq // d, gw))
        lses.append(lse_g.reshape(batch, d, seq // d, LANES))
    h = _merge_wo(os, lses, x, a_w_o[0], dilations, tm=WIDE_ROWS_PER_STEP)
    h = h.reshape(n_rows, d_model)
    w1, w2 = ffn_w1, ffn_w2
    h = _ffn(h, norm_ffn[0].reshape(1, d_model), w1, w2, 0, tm=ROWS_PER_STEP, name="ffn0")

    width = moba_heads * HEAD_DIM
    (q, k, v), = _qkv_project(
        h.reshape(batch, seq, d_model), norm_mix[1].reshape(1, d_model), b_w_qkv[0],
        _gain_row(b_q_gain[0], LANES, SCALE * LOG2E), _gain_row(b_k_gain[0], LANES, 1.0),
        [1], tm=WIDE_ROWS_PER_STEP, name="moba_qkv")
    q, k, v = (a.reshape(n_rows, width) for a in (q, k, v))
    attn = _moba_attention(q, k, v, moba_vec, batch, seq, chunk=MOBA_CHUNK_BLOCKS)
    h = _ffn(h, norm_ffn[1].reshape(1, d_model), w1, w2, 1,
             tm=ROWS_PER_STEP, attn=attn, wo=b_w_o[0], name="wo_ffn1")
    return h.reshape(batch, seq, d_model)
```

```python
import functools
import math

import jax
import jax.numpy as jnp
from jax import lax
from jax.experimental import pallas as pl
from jax.experimental.pallas import tpu as pltpu

HEAD_DIM = 64
LANES = 128
DSW_GROUPS = ((128, 1), (512, 4), (2048, 16))
DSW_BLK = 128
DSW_HEADS_PER_GROUP = 8
MOBA_BLOCK = 256
MOBA_TOPK = 3
REL_BUCKETS = 32
REL_MAX_DISTANCE = 2048
EPS = 1e-6
NEG = -1e30
SCALE = HEAD_DIM ** -0.5
LOG2E = 1.4426950408889634
EXP_RANGE = 100.0
VMEM_LIMIT_BYTES = 56 * 1024 * 1024
ROWS_PER_STEP = 512
WIDE_ROWS_PER_STEP = 1024
DSW_ROWS_PER_STEP = 2048
MOBA_CHUNK_BLOCKS = 4
MOBA_PAIRS_PER_REGION = 4
BF16_SUBLANES = 16

F32 = jnp.float32
BF16 = jnp.bfloat16
NT_DIMS = (((1,), (1,)), ((), ()))


def _params(n_axes):
    return pltpu.CompilerParams(dimension_semantics=("arbitrary",) * n_axes,
                                vmem_limit_bytes=VMEM_LIMIT_BYTES)


def _aligned(start, multiple):
    return start if isinstance(start, int) else pl.multiple_of(start, multiple)


def _t5_bucket(dist):
    n = jnp.maximum(dist, 0)
    max_exact = REL_BUCKETS // 2
    nf = jnp.maximum(n, 1).astype(F32)
    large = max_exact + jnp.floor(jnp.log(nf / max_exact) / math.log(REL_MAX_DISTANCE / max_exact)
                                  * (REL_BUCKETS - max_exact)).astype(jnp.int32)
    large = jnp.minimum(large, REL_BUCKETS - 1)
    return jnp.where(n < max_exact, n, large)


def _lookup(bucket, table_t):
    acc = jnp.zeros(bucket.shape, F32)
    for b in range(REL_BUCKETS):
        acc = jnp.where(bucket == b, table_t[:, b:b + 1], acc)
    return acc


def _bias_tables_kernel(tab_ref, dsw_ref, moba_ref, *, n_moba_heads, seq):
    tab = tab_ref[...]
    n_heads = dsw_ref.shape[0]
    m = lax.broadcasted_iota(jnp.int32, (n_heads, 2 * DSW_BLK), 1)
    row = lax.broadcasted_iota(jnp.int32, (n_heads, 2 * DSW_BLK), 0)
    dil = jnp.where(row < DSW_HEADS_PER_GROUP, DSW_GROUPS[0][1],
                    jnp.where(row < 2 * DSW_HEADS_PER_GROUP, DSW_GROUPS[1][1], DSW_GROUPS[2][1]))
    sub = DSW_BLK - m
    vals = _lookup(_t5_bucket(sub * dil), tab)
    dsw_ref[...] = jnp.where(sub >= 0, vals * LOG2E, NEG)
    width = moba_ref.shape[1]
    t = lax.broadcasted_iota(jnp.int32, (n_moba_heads, width), 1)
    dist = t - MOBA_BLOCK
    vals = _lookup(_t5_bucket(dist), tab[:n_moba_heads])
    moba_ref[...] = jnp.where(dist >= 0, vals, NEG)


def _bias_tables(rel_bias, n_moba_heads, seq):
    n_heads = rel_bias.shape[1]
    width = seq + MOBA_BLOCK
    return pl.pallas_call(
        functools.partial(_bias_tables_kernel, n_moba_heads=n_moba_heads, seq=seq),
        out_shape=(jax.ShapeDtypeStruct((n_heads, 2 * DSW_BLK), F32),
                   jax.ShapeDtypeStruct((n_moba_heads, width), F32)),
        name="bias_tables",
    )(rel_bias.T)


def _toeplitz(u_row, rows):
    x = jnp.broadcast_to(u_row, (rows, u_row.shape[1]))
    return pltpu.roll(x, 0, 1, stride=1, stride_axis=0)


def _dsw_bias_kernel(vec_ref, out_ref):
    lane = lax.broadcasted_iota(jnp.int32, (DSW_BLK, 2 * DSW_BLK), 1)
    for h in range(DSW_HEADS_PER_GROUP):
        tile = _toeplitz(vec_ref[0, h:h + 1, :], DSW_BLK)
        out_ref[0, h, 0] = tile
        out_ref[0, h, 1] = jnp.where(lane < DSW_BLK, NEG, tile)


def _dsw_bias_tiles(dsw_vec):
    n_groups = len(DSW_GROUPS)
    hg = DSW_HEADS_PER_GROUP
    vec = dsw_vec.reshape(n_groups, hg, 2 * DSW_BLK)
    return pl.pallas_call(
        _dsw_bias_kernel,
        grid=(n_groups,),
        in_specs=[pl.BlockSpec((1, hg, 2 * DSW_BLK), lambda g: (g, 0, 0))],
        out_specs=pl.BlockSpec((1, hg, 2, DSW_BLK, 2 * DSW_BLK), lambda g: (g, 0, 0, 0, 0)),
        out_shape=jax.ShapeDtypeStruct((n_groups, hg, 2, DSW_BLK, 2 * DSW_BLK), F32),
        compiler_params=_params(1),
        name="dsw_bias_tiles",
    )(vec)


def _rmsnorm_bf16(x, gain):
    ms = jnp.mean(x * x, axis=-1, keepdims=True)
    return (x * lax.rsqrt(ms + EPS) * gain).astype(BF16)


def _qkv_kernel(x_ref, g_ref, w_ref, qg_ref, kg_ref, *rest, width, dilations):
    n_groups = len(dilations)
    out_refs = rest[:3 * n_groups]
    u_s = rest[3 * n_groups]
    x = x_ref[0]
    rows = x.shape[0]
    ms = jnp.mean(x * x, axis=-1, keepdims=True)
    u = x * lax.rsqrt(ms + EPS) * g_ref[...]
    n_tiles = x.shape[1] // LANES
    if any(d > 1 for d in dilations):
        for j in range(n_tiles):
            u_s[j] = u[:, j * LANES:(j + 1) * LANES]
    low_lanes = lax.broadcasted_iota(jnp.int32, (rows, LANES), 1) < HEAD_DIM
    for gi, d in enumerate(dilations):
        per = rows // d
        if d == 1:
            lhs = u.astype(BF16)
        else:
            lhs = jnp.concatenate(
                [jnp.concatenate([u_s[j, pl.ds(c, per, stride=d), :] for j in range(n_tiles)], axis=1)
                 for c in range(d)], axis=0).astype(BF16)
        q_ref, k_ref, v_ref = out_refs[3 * gi:3 * gi + 3]
        for part, (gain_ref, out_ref) in enumerate(((qg_ref, q_ref), (kg_ref, k_ref), (None, v_ref))):
            col = (part * n_groups + gi) * width
            y_part = jnp.dot(lhs, w_ref[:, col:col + width].astype(BF16), preferred_element_type=F32)
            for cc in range(width // LANES):
                y = y_part[:, cc * LANES:(cc + 1) * LANES]
                if gain_ref is not None:
                    sq = y * y
                    both = jnp.sum(sq, axis=-1, keepdims=True)
                    low = jnp.sum(jnp.where(low_lanes, sq, 0.0), axis=-1, keepdims=True)
                    msq = jnp.where(low_lanes, low, both - low) * (1.0 / HEAD_DIM)
                    y = y * lax.rsqrt(msq + EPS) * gain_ref[...]
                y = y.astype(BF16)
                for c in range(d):
                    out_ref[0, c, :, cc * LANES:(cc + 1) * LANES] = y[c * per:(c + 1) * per]


def _qkv_project(x, gain, w, q_gain_row, k_gain_row, dilations, tm, name):
    batch, seq, d_model = x.shape
    width = w.shape[1] // (3 * len(dilations))
    const = lambda b, t: (0, 0)
    out_specs, out_shapes = [], []
    for d in dilations:
        out_specs += [pl.BlockSpec((1, d, tm // d, width), lambda b, t: (b, 0, t, 0))] * 3
        out_shapes += [jax.ShapeDtypeStruct((batch, d, seq // d, width), BF16)] * 3
    outs = pl.pallas_call(
        functools.partial(_qkv_kernel, width=width, dilations=tuple(dilations)),
        grid=(batch, seq // tm),
        in_specs=[pl.BlockSpec((1, tm, d_model), lambda b, t: (b, t, 0)),
                  pl.BlockSpec((1, d_model), const),
                  pl.BlockSpec(w.shape, const, pipeline_mode=pl.Buffered(1)),
                  pl.BlockSpec(q_gain_row.shape, const),
                  pl.BlockSpec(k_gain_row.shape, const)],
        out_specs=tuple(out_specs),
        out_shape=tuple(out_shapes),
        scratch_shapes=[pltpu.VMEM((d_model // LANES, tm, LANES), F32)],
        compiler_params=_params(2),
        name=name,
    )(x, gain, w, q_gain_row, k_gain_row)
    return [tuple(outs[3 * i:3 * i + 3]) for i in range(len(dilations))]


def _dsw_attn_kernel(q_ref, kc_ref, vc_ref, kp_ref, vp_ref, bias_ref, o_ref, stat_ref, *,
                     tq, blocks_per_seq):
    t = pl.program_id(0)
    n_blk = tq // DSW_BLK
    lane = lax.broadcasted_iota(jnp.int32, (DSW_BLK, LANES), 1)
    low_half = lane < HEAD_DIM
    n_pairs = q_ref.shape[1] // LANES
    for qi in range(n_blk):
        rows = slice(qi * DSW_BLK, (qi + 1) * DSW_BLK)
        first = (jnp.bitwise_and(t * n_blk + qi, blocks_per_seq - 1) == 0).astype(jnp.int32)
        stat_tile = jnp.zeros((DSW_BLK, LANES), F32)
        for hp in range(n_pairs):
            cols = slice(hp * LANES, (hp + 1) * LANES)
            q2 = q_ref[rows, cols]
            if qi == 0:
                k_prev, v_prev = kp_ref[:, cols], vp_ref[:, cols]
            else:
                prev = slice((qi - 1) * DSW_BLK, qi * DSW_BLK)
                k_prev, v_prev = kc_ref[prev, cols], vc_ref[prev, cols]
            k_cat = jnp.concatenate([k_prev, kc_ref[rows, cols]], axis=0)
            v_cat = jnp.concatenate([v_prev, vc_ref[rows, cols]], axis=0)
            outs = []
            for hh in range(2):
                head_lanes = low_half if hh == 0 else jnp.logical_not(low_half)
                qm = jnp.where(head_lanes, q2, jnp.zeros_like(q2))
                s = lax.dot_general(qm, k_cat, NT_DIMS, preferred_element_type=F32)
                s = s + bias_ref[0, hp * 2 + hh, first]
                m = jnp.max(s, axis=-1, keepdims=True)
                p = jnp.exp2(s - m)
                den = jnp.sum(p, axis=-1, keepdims=True)
                outs.append(jnp.dot(p.astype(BF16), v_cat, preferred_element_type=F32))
                head = hp * 2 + hh
                stat_tile = jnp.where(lane == head, m,
                                      jnp.where(lane == DSW_HEADS_PER_GROUP + head, den, stat_tile))
            o_ref[rows, cols] = jnp.where(low_half, outs[0], outs[1]).astype(o_ref.dtype)
        stat_ref[rows, :] = stat_tile


def _dsw_attention(q, k, v, bias_tiles, group, tq, blocks_per_seq):
    n_rows, width = q.shape
    assert blocks_per_seq & (blocks_per_seq - 1) == 0
    n_blk = tq // DSW_BLK
    cur = pl.BlockSpec((tq, width), lambda t: (t, 0))
    prev = pl.BlockSpec((DSW_BLK, width), lambda t: (jnp.maximum(t * n_blk - 1, 0), 0))
    hg = DSW_HEADS_PER_GROUP
    return pl.pallas_call(
        functools.partial(_dsw_attn_kernel, tq=tq, blocks_per_seq=blocks_per_seq),
        grid=(n_rows // tq,),
        in_specs=[cur, cur, cur, prev, prev,
                  pl.BlockSpec((1, hg, 2, DSW_BLK, 2 * DSW_BLK), lambda t: (group, 0, 0, 0, 0))],
        out_specs=(pl.BlockSpec((tq, width), lambda t: (t, 0)),
                   pl.BlockSpec((tq, LANES), lambda t: (t, 0))),
        out_shape=(jax.ShapeDtypeStruct((n_rows, width), BF16),
                   jax.ShapeDtypeStruct((n_rows, LANES), F32)),
        compiler_params=_params(1),
        name=f"dsw_attention_g{group}",
    )(q, k, v, k, v, bias_tiles)


def _merge_wo_kernel(*refs, dilations):
    n = len(dilations)
    o_refs, l_refs = refs[:n], refs[n:2 * n]
    x_ref, wo_ref, e_ref, out_ref = refs[2 * n:2 * n + 4]
    scratch = refs[2 * n + 4:]
    rows = x_ref.shape[1]
    outs, lses = [], []
    si = 0
    for d, o_ref, l_ref in zip(dilations, o_refs, l_refs):
        if d == 1:
            outs.append(o_ref[0, 0].astype(F32))
            lses.append(l_ref[0, 0])
            continue
        o_s, l_s = scratch[si], scratch[si + 1]
        si += 2
        n_tiles = o_s.shape[0]
        for c in range(d):
            piece = o_ref[0, c].astype(F32)
            for j in range(n_tiles):
                o_s[j, pl.ds(c, rows // d, stride=d), :] = piece[:, j * LANES:(j + 1) * LANES]
            l_s[pl.ds(c, rows // d, stride=d), :] = l_ref[0, c]
        outs.append(jnp.concatenate([o_s[j] for j in range(n_tiles)], axis=1))
        lses.append(l_s[...])
    lane = lax.broadcasted_iota(jnp.int32, lses[0].shape, 1)
    mx = functools.reduce(jnp.maximum, lses)
    es = [jnp.exp2(l - mx) for l in lses]
    sums = [pltpu.roll(l, LANES - DSW_HEADS_PER_GROUP, 1) for l in lses]
    total = functools.reduce(lambda a, b: a + b, [e * s for e, s in zip(es, sums)])
    merged = jnp.zeros(outs[0].shape, F32)
    for e, o in zip(es, outs):
        wgt = jnp.where(lane < DSW_HEADS_PER_GROUP, e / total, 0.0)
        hi = wgt.astype(BF16)
        lo = (wgt - hi.astype(F32)).astype(BF16)
        spread = (jnp.dot(hi, e_ref[...], preferred_element_type=F32)
                  + jnp.dot(lo, e_ref[...], preferred_element_type=F32))
        merged = merged + spread * o
    out_ref[0] = x_ref[0] + jnp.dot(merged.astype(BF16), wo_ref[...].astype(BF16),
                                    preferred_element_type=F32)


def _merge_wo(os, lses, x, wo, dilations, tm):
    batch, seq, d_model = x.shape
    width = os[0].shape[-1]
    head_of_lane = jnp.arange(width) // HEAD_DIM
    expand = (jnp.arange(LANES)[:, None] == head_of_lane[None, :]).astype(BF16)
    const = lambda b, t: (0, 0)
    blk = lambda d, w: pl.BlockSpec((1, d, tm // d, w), lambda b, t: (b, 0, t, 0))
    x_spec = pl.BlockSpec((1, tm, d_model), lambda b, t: (b, t, 0))
    scratch = []
    for d in dilations:
        if d > 1:
            scratch += [pltpu.VMEM((width // LANES, tm, LANES), F32), pltpu.VMEM((tm, LANES), F32)]
    return pl.pallas_call(
        functools.partial(_merge_wo_kernel, dilations=tuple(dilations)),
        grid=(batch, seq // tm),
        in_specs=([blk(d, width) for d in dilations] + [blk(d, LANES) for d in dilations]
                  + [x_spec, pl.BlockSpec(wo.shape, const), pl.BlockSpec(expand.shape, const)]),
        out_specs=x_spec,
        out_shape=jax.ShapeDtypeStruct(x.shape, F32),
        scratch_shapes=scratch,
        compiler_params=_params(2),
        name="dsw_merge_wo",
    )(*os, *lses, x, wo, expand)


def _ffn_kernel(*refs, has_attn):
    if has_attn:
        h_ref, a_ref, wo_ref, g_ref, w1_ref, w2_ref, out_ref = refs
        h = h_ref[...] + jnp.dot(a_ref[...], wo_ref[...].astype(BF16), preferred_element_type=F32)
    else:
        h_ref, g_ref, w1_ref, w2_ref, out_ref = refs
        h = h_ref[...]
    u = _rmsnorm_bf16(h, g_ref[...])
    a = jnp.dot(u, w1_ref[0].astype(BF16), preferred_element_type=F32)
    a = jnp.square(jnp.maximum(a, 0.0)).astype(BF16)
    out_ref[...] = h + jnp.dot(a, w2_ref[0].astype(BF16), preferred_element_type=F32)


def _ffn(h, gain, w1, w2, layer, tm, attn=None, wo=None, name="ffn"):
    n_rows, d_model = h.shape
    const = lambda i: (0, 0)
    row_spec = pl.BlockSpec((tm, d_model), lambda i: (i, 0))
    in_specs = [row_spec]
    args = [h]
    if attn is not None:
        in_specs += [pl.BlockSpec((tm, attn.shape[1]), lambda i: (i, 0)), pl.BlockSpec(wo.shape, const)]
        args += [attn, wo]
    in_specs += [pl.BlockSpec((1, d_model), const),
                 pl.BlockSpec((1,) + w1.shape[1:], lambda i: (layer, 0, 0), pipeline_mode=pl.Buffered(1)),
                 pl.BlockSpec((1,) + w2.shape[1:], lambda i: (layer, 0, 0), pipeline_mode=pl.Buffered(1))]
    args += [gain, w1, w2]
    return pl.pallas_call(
        functools.partial(_ffn_kernel, has_attn=attn is not None),
        grid=(n_rows // tm,),
        in_specs=in_specs,
        out_specs=row_spec,
        out_shape=jax.ShapeDtypeStruct(h.shape, F32),
        compiler_params=_params(1),
        name=name,
    )(*args)


def _moba_kernel(q_ref, k_ref, v_ref, wn_ref, o_ref,
                 bias_s, kp_s, vt_s, qa_s, s_s, acc_s, *, n_blocks, chunk):
    b = pl.program_id(1)
    blk = MOBA_BLOCK
    seq = n_blocks * blk
    span = chunk * blk
    lane = lax.broadcasted_iota(jnp.int32, (blk, LANES), 1)
    low_lanes = lane < HEAD_DIM
    v_rows = vt_s.shape[1]
    ones_row = (lax.broadcasted_iota(jnp.int32, (v_rows - HEAD_DIM, blk), 0) == 0).astype(F32)

    @pl.when(b == 0)
    def _build_bias():
        for hh in range(2):
            for dlt in range(n_blocks):
                u = jnp.concatenate([wn_ref[0, hh:hh + 1, (dlt + 1) * blk:(dlt + 2) * blk],
                                     wn_ref[0, hh:hh + 1, dlt * blk:(dlt + 1) * blk]], axis=1)
                r = n_blocks - 1 - dlt
                bias_s[hh, r * blk:(r + 1) * blk, :] = _toeplitz(u, blk)[:, :blk] * LOG2E
            bias_s[hh, seq:, :] = jnp.full((span - blk, blk), NEG, F32)

    sub = lax.broadcasted_iota(jnp.int32, (n_blocks, LANES), 0)
    kmean = jnp.zeros((n_blocks, LANES), F32)
    kamax = []
    for j in range(n_blocks):
        rows = slice(j * blk, (j + 1) * blk)
        kj = k_ref[0, rows, :].astype(F32)
        vj_t = v_ref[0, rows, :].astype(F32).T
        kmean = jnp.where(sub == j, jnp.mean(kj, axis=0, keepdims=True), kmean)
        kamax.append(jnp.max(jnp.abs(kj), axis=0, keepdims=True))
        aux0 = jnp.logical_or(lane == HEAD_DIM + j, lane == HEAD_DIM + n_blocks).astype(F32)
        aux1 = jnp.logical_or(lane == j, lane == n_blocks).astype(F32)
        kp_s[0, rows, :] = jnp.where(low_lanes, kj, aux0).astype(BF16)
        kp_s[1, rows, :] = jnp.where(low_lanes, aux1, kj).astype(BF16)
        for hh in range(2):
            vt_s[hh, :, rows] = jnp.concatenate(
                [vj_t[hh * HEAD_DIM:(hh + 1) * HEAD_DIM], ones_row], axis=0).astype(BF16)
    km_hi = kmean.astype(BF16)
    km_lo = (kmean - km_hi.astype(F32)).astype(BF16)
    kamax_upto = [kamax[0]]
    for j in range(1, n_blocks):
        kamax_upto.append(jnp.maximum(kamax_upto[-1], kamax[j]))
    pad_rows = jnp.zeros((6, LANES), F32)
    bias_max = [jnp.max(wn_ref[0, hh:hh + 1, blk:], axis=1, keepdims=True) * LOG2E for hh in range(2)]
    bias_zero = [wn_ref[0, hh:hh + 1, blk:blk + 1] * LOG2E for hh in range(2)]

    blk_row = lax.broadcasted_iota(jnp.int32, (n_blocks, blk), 0)
    blk_row_f = blk_row.astype(F32)
    gap = jnp.full((1, blk), -jnp.inf, F32)
    for n in range(n_blocks):
        rows = slice(n * blk, (n + 1) * blk)
        q2 = q_ref[0, rows, :]
        q_abs = jnp.abs(q2)
        k_bounds = jnp.concatenate([kamax_upto[n], kamax[n], pad_rows], axis=0).astype(BF16)
        for hh in range(2):
            own = low_lanes if hh == 0 else jnp.logical_not(low_lanes)
            qm = jnp.where(own, q2, jnp.zeros_like(q2))
            gate = (lax.dot_general(km_hi, qm, NT_DIMS, preferred_element_type=F32)
                    + lax.dot_general(km_lo, qm, NT_DIMS, preferred_element_type=F32))
            gate = jnp.where(blk_row < n, gate, -jnp.inf)
            chosen = blk_row == n
            for _ in range(min(MOBA_TOPK, n)):
                best = jnp.max(gate, axis=0, keepdims=True)
                cand = jnp.where(gate == best, blk_row_f, float(n_blocks))
                first = jnp.min(cand, axis=0, keepdims=True)
                pick = blk_row_f == first
                chosen = jnp.logical_or(chosen, pick)
                gate = jnp.where(pick, -jnp.inf, gate)
            pen_t = jnp.where(chosen, 0.0, NEG)
            sums = lax.dot_general(k_bounds, jnp.where(own, q_abs, jnp.zeros_like(q_abs)),
                                   NT_DIMS, preferred_element_type=F32)
            bound = sums[0:1] + bias_max[hh] + 1.0
            gap = jnp.maximum(gap, bound - (bias_zero[hh] - sums[1:2]))
            pad_lo = HEAD_DIM if hh == 0 else 0
            pieces = [jnp.zeros((pad_lo, blk), F32)] if pad_lo else []
            pieces += [pen_t, -bound, jnp.zeros((LANES - pad_lo - n_blocks - 1, blk), F32)]
            pen = jnp.concatenate(pieces, axis=0).T
            qa_s[hh, rows, :] = jnp.where(own, q2, pen.astype(BF16))

    def emit(n, width):
        cols = width * blk
        o_t = jnp.concatenate([acc_s[hh, :HEAD_DIM, :cols] / acc_s[hh, HEAD_DIM:HEAD_DIM + 1, :cols]
                               for hh in range(2)], axis=0)
        o_ref[0, pl.ds(_aligned(n * blk, blk), cols), :] = o_t.T.astype(o_ref.dtype)

    def raw_scores(hh, n, key0, n_keys, width):
        q_aug = qa_s[hh, pl.ds(_aligned(n * blk, blk), width * blk), :]
        s_t = lax.dot_general(kp_s[hh, key0:key0 + n_keys, :], q_aug, NT_DIMS,
                              preferred_element_type=F32)
        bias = [bias_s[hh, pl.ds(_aligned((n_blocks - 1 - n - i) * blk + key0, blk), n_keys), :]
                for i in range(width)]
        return s_t + (bias[0] if width == 1 else jnp.concatenate(bias, axis=1))

    acc_s[...] = jnp.ones(acc_s.shape, F32)
    n_groups = n_blocks // chunk
    bound_is_tight = jnp.max(gap) < EXP_RANGE
    one = jnp.minimum(b + 1, 1)

    @pl.when(bound_is_tight)
    def _single_pass():
        def q_pair(pair):
            n = 2 * pair
            wide = 4 * span
            own_chunk, own_blocks = divmod(n + 2, 4 * chunk)
            emit(jnp.maximum(n - 2, 0), 2)
            ranges = [(c * wide, wide) for c in range(own_chunk)]
            if own_blocks:
                ranges.append((own_chunk * wide, own_blocks * blk))
            for hh in range(2):
                acc = None
                for key0, n_keys in ranges:
                    p_t = jnp.exp2(raw_scores(hh, n, key0, n_keys, 2)).astype(BF16)
                    part = jnp.dot(vt_s[hh, :, key0:key0 + n_keys], p_t,
                                   preferred_element_type=F32)
                    acc = part if acc is None else acc + part
                acc_s[hh] = acc

        for first in range(0, n_blocks // 2, MOBA_PAIRS_PER_REGION):
            def region(t, c, first=first):
                for pair in range(first, first + MOBA_PAIRS_PER_REGION):
                    q_pair(pair)
                return c
            lax.fori_loop(0, one, region, 0)
        emit(n_blocks - 2, 2)

    @pl.when(jnp.logical_not(bound_is_tight))
    def _two_pass():
        def scores(hh, n, n_chunks):
            m = None
            for c in range(n_chunks):
                s_t = raw_scores(hh, n, c * span, span, 1)
                s_s[hh, c * span:(c + 1) * span, :] = s_t
                cm = jnp.max(s_t, axis=0, keepdims=True)
                m = cm if m is None else jnp.maximum(m, cm)
            return m

        def weighted_values(hh, m, n_chunks):
            n_keys = n_chunks * span
            p_t = jnp.exp2(s_s[hh, :n_keys, :] - m).astype(BF16)
            return jnp.dot(vt_s[hh, :, :n_keys], p_t, preferred_element_type=F32)

        def step(n, m0, n_chunks, next_chunks):
            emit(jnp.maximum(n - 1, 0), 1)
            acc0 = weighted_values(0, m0, n_chunks)
            m1 = scores(1, n, n_chunks)
            m0_next = scores(0, n + 1, next_chunks) if next_chunks else m0
            acc1 = weighted_values(1, m1, n_chunks)
            acc_s[0, :, :blk] = acc0
            acc_s[1, :, :blk] = acc1
            return m0_next

        m0 = scores(0, 0, 1)
        for g in range(n_groups):
            m0 = lax.fori_loop(0, chunk - 1,
                               lambda t, m, g=g: step(g * chunk + t, m, g + 1, g + 1), m0)
            m0 = step(g * chunk + chunk - 1, m0, g + 1, g + 2 if g + 1 < n_groups else 0)
        emit(n_blocks - 1, 1)


def _moba_attention(q, k, v, moba_vec, batch, seq, chunk):
    n_rows, width = q.shape
    n_pairs = width // LANES
    n_blocks = seq // MOBA_BLOCK
    assert n_blocks % chunk == 0
    wn = moba_vec.reshape(n_pairs, 2, moba_vec.shape[1])
    spec = pl.BlockSpec((1, seq, LANES), lambda p, b: (b, 0, p))
    shape3 = (batch, seq, width)
    out = pl.pallas_call(
        functools.partial(_moba_kernel, n_blocks=n_blocks, chunk=chunk),
        grid=(n_pairs, batch),
        in_specs=[spec, spec, spec, pl.BlockSpec((1, 2, wn.shape[2]), lambda p, b: (p, 0, 0))],
        out_specs=spec,
        out_shape=jax.ShapeDtypeStruct(shape3, BF16),
        scratch_shapes=[pltpu.VMEM((2, seq + (chunk - 1) * MOBA_BLOCK, MOBA_BLOCK), F32),
                        pltpu.VMEM((2, seq, LANES), BF16),
                        pltpu.VMEM((2, HEAD_DIM + BF16_SUBLANES, seq), BF16),
                        pltpu.VMEM((2, seq, LANES), BF16),
                        pltpu.VMEM((2, seq, MOBA_BLOCK), F32),
                        pltpu.VMEM((2, HEAD_DIM + BF16_SUBLANES, 2 * MOBA_BLOCK), F32)],
        compiler_params=_params(2),
        name="moba_attention",
    )(q.reshape(shape3), k.reshape(shape3), v.reshape(shape3), wn)
    return out.reshape(n_rows, width)


def _gain_row(gain, size, scale):
    return (jnp.tile(gain.astype(F32), size // HEAD_DIM) * scale).reshape(1, size)


def kernel(x, rel_bias, norm_mix, norm_ffn, a_w_qkv, a_q_gain, a_k_gain, a_w_o,
           b_w_qkv, b_q_gain, b_k_gain, b_w_o, ffn_w1, ffn_w2):
    batch, seq, d_model = x.shape
    n_rows = batch * seq
    n_groups = len(DSW_GROUPS)
    dilations = [d for _, d in DSW_GROUPS]
    gw = DSW_HEADS_PER_GROUP * HEAD_DIM
    moba_heads = b_w_o.shape[1] // HEAD_DIM

    dsw_vec, moba_vec = _bias_tables(rel_bias, moba_heads, seq)
    dsw_tiles = _dsw_bias_tiles(dsw_vec)

    qkv = _qkv_project(x, norm_mix[0].reshape(1, d_model), a_w_qkv[0],
                       _gain_row(a_q_gain[0], LANES, SCALE * LOG2E), _gain_row(a_k_gain[0], LANES, 1.0),
                       dilations, tm=ROWS_PER_STEP, name="dsw_qkv")
    os, lses = [], []
    for g, d in enumerate(dilations):
        q_g, k_g, v_g = (a.reshape(n_rows, gw) for a in qkv[g])
        o_g, lse_g = _dsw_attention(q_g, k_g, v_g, dsw_tiles, g, tq=DSW_ROWS_PER_STEP,
                                    blocks_per_seq=seq // d // DSW_BLK)
        os.append(o_g.reshape(batch, d, seq // d, gw))
        lses.append(lse_g.reshape(batch, d, seq // d, LANES))
    h = _merge_wo(os, lses, x, a_w_o[0], dilations, tm=WIDE_ROWS_PER_STEP)
    h = h.reshape(n_rows, d_model)
    w1, w2 = ffn_w1, ffn_w2
    h = _ffn(h, norm_ffn[0].reshape(1, d_model), w1, w2, 0, tm=ROWS_PER_STEP, name="ffn0")

    width = moba_heads * HEAD_DIM
    (q, k, v), = _qkv_project(
        h.reshape(batch, seq, d_model), norm_mix[1].reshape(1, d_model), b_w_qkv[0],
        _gain_row(b_q_gain[0], LANES, SCALE * LOG2E), _gain_row(b_k_gain[0], LANES, 1.0),
        [1], tm=WIDE_ROWS_PER_STEP, name="moba_qkv")
    q, k, v = (a.reshape(n_rows, width) for a in (q, k, v))
    attn = _moba_attention(q, k, v, moba_vec, batch, seq, chunk=MOBA_CHUNK_BLOCKS)
    h = _ffn(h, norm_ffn[1].reshape(1, d_model), w1, w2, 1,
             tm=ROWS_PER_STEP, attn=attn, wo=b_w_o[0], name="wo_ffn1")
    return h.reshape(batch, seq, d_model)
```

```python
import functools
import math

import jax
import jax.numpy as jnp
from jax import lax
from jax.experimental import pallas as pl
from jax.experimental.pallas import tpu as pltpu

HEAD_DIM = 64
LANES = 128
DSW_GROUPS = ((128, 1), (512, 4), (2048, 16))
DSW_BLK = 128
DSW_HEADS_PER_GROUP = 8
MOBA_BLOCK = 256
MOBA_TOPK = 3
REL_BUCKETS = 32
REL_MAX_DISTANCE = 2048
EPS = 1e-6
NEG = -1e30
SCALE = HEAD_DIM ** -0.5
LOG2E = 1.4426950408889634
EXP_RANGE = 100.0
VMEM_LIMIT_BYTES = 56 * 1024 * 1024
ROWS_PER_STEP = 512
WIDE_ROWS_PER_STEP = 1024
DSW_ROWS_PER_STEP = 2048
MOBA_CHUNK_BLOCKS = 4
MOBA_PAIRS_PER_REGION = 4
BF16_SUBLANES = 16

F32 = jnp.float32
BF16 = jnp.bfloat16
NT_DIMS = (((1,), (1,)), ((), ()))


def _params(n_axes):
    return pltpu.CompilerParams(dimension_semantics=("arbitrary",) * n_axes,
                                vmem_limit_bytes=VMEM_LIMIT_BYTES)


def _aligned(start, multiple):
    return start if isinstance(start, int) else pl.multiple_of(start, multiple)


def _t5_bucket(dist):
    n = jnp.maximum(dist, 0)
    max_exact = REL_BUCKETS // 2
    nf = jnp.maximum(n, 1).astype(F32)
    large = max_exact + jnp.floor(jnp.log(nf / max_exact) / math.log(REL_MAX_DISTANCE / max_exact)
                                  * (REL_BUCKETS - max_exact)).astype(jnp.int32)
    large = jnp.minimum(large, REL_BUCKETS - 1)
    return jnp.where(n < max_exact, n, large)


def _lookup(bucket, table_t):
    acc = jnp.zeros(bucket.shape, F32)
    for b in range(REL_BUCKETS):
        acc = jnp.where(bucket == b, table_t[:, b:b + 1], acc)
    return acc


def _bias_tables_kernel(tab_ref, dsw_ref, moba_ref, *, n_moba_heads, seq):
    tab = tab_ref[...]
    n_heads = dsw_ref.shape[0]
    m = lax.broadcasted_iota(jnp.int32, (n_heads, 2 * DSW_BLK), 1)
    row = lax.broadcasted_iota(jnp.int32, (n_heads, 2 * DSW_BLK), 0)
    dil = jnp.where(row < DSW_HEADS_PER_GROUP, DSW_GROUPS[0][1],
                    jnp.where(row < 2 * DSW_HEADS_PER_GROUP, DSW_GROUPS[1][1], DSW_GROUPS[2][1]))
    sub = DSW_BLK - m
    vals = _lookup(_t5_bucket(sub * dil), tab)
    dsw_ref[...] = jnp.where(sub >= 0, vals * LOG2E, NEG)
    width = moba_ref.shape[1]
    t = lax.broadcasted_iota(jnp.int32, (n_moba_heads, width), 1)
    dist = t - MOBA_BLOCK
    vals = _lookup(_t5_bucket(dist), tab[:n_moba_heads])
    moba_ref[...] = jnp.where(dist >= 0, vals, NEG)


def _bias_tables(rel_bias, n_moba_heads, seq):
    n_heads = rel_bias.shape[1]
    width = seq + MOBA_BLOCK
    return pl.pallas_call(
        functools.partial(_bias_tables_kernel, n_moba_heads=n_moba_heads, seq=seq),
        out_shape=(jax.ShapeDtypeStruct((n_heads, 2 * DSW_BLK), F32),
                   jax.ShapeDtypeStruct((n_moba_heads, width), F32)),
        name="bias_tables",
    )(rel_bias.T)


def _toeplitz(u_row, rows):
    x = jnp.broadcast_to(u_row, (rows, u_row.shape[1]))
    return pltpu.roll(x, 0, 1, stride=1, stride_axis=0)


def _dsw_bias_kernel(vec_ref, out_ref):
    lane = lax.broadcasted_iota(jnp.int32, (DSW_BLK, 2 * DSW_BLK), 1)
    for h in range(DSW_HEADS_PER_GROUP):
        tile = _toeplitz(vec_ref[0, h:h + 1, :], DSW_BLK)
        out_ref[0, h, 0] = tile
        out_ref[0, h, 1] = jnp.where(lane < DSW_BLK, NEG, tile)


def _dsw_bias_tiles(dsw_vec):
    n_groups = len(DSW_GROUPS)
    hg = DSW_HEADS_PER_GROUP
    vec = dsw_vec.reshape(n_groups, hg, 2 * DSW_BLK)
    return pl.pallas_call(
        _dsw_bias_kernel,
        grid=(n_groups,),
        in_specs=[pl.BlockSpec((1, hg, 2 * DSW_BLK), lambda g: (g, 0, 0))],
        out_specs=pl.BlockSpec((1, hg, 2, DSW_BLK, 2 * DSW_BLK), lambda g: (g, 0, 0, 0, 0)),
        out_shape=jax.ShapeDtypeStruct((n_groups, hg, 2, DSW_BLK, 2 * DSW_BLK), F32),
        compiler_params=_params(1),
        name="dsw_bias_tiles",
    )(vec)


def _rmsnorm_bf16(x, gain):
    ms = jnp.mean(x * x, axis=-1, keepdims=True)
    return (x * lax.rsqrt(ms + EPS) * gain).astype(BF16)


def _qkv_kernel(x_ref, g_ref, w_ref, qg_ref, kg_ref, *rest, width, dilations):
    n_groups = len(dilations)
    out_refs = rest[:3 * n_groups]
    u_s = rest[3 * n_groups]
    x = x_ref[0]
    rows = x.shape[0]
    ms = jnp.mean(x * x, axis=-1, keepdims=True)
    u = x * lax.rsqrt(ms + EPS) * g_ref[...]
    n_tiles = x.shape[1] // LANES
    if any(d > 1 for d in dilations):
        for j in range(n_tiles):
            u_s[j] = u[:, j * LANES:(j + 1) * LANES]
    low_lanes = lax.broadcasted_iota(jnp.int32, (rows, LANES), 1) < HEAD_DIM
    for gi, d in enumerate(dilations):
        per = rows // d
        if d == 1:
            lhs = u.astype(BF16)
        else:
            lhs = jnp.concatenate(
                [jnp.concatenate([u_s[j, pl.ds(c, per, stride=d), :] for j in range(n_tiles)], axis=1)
                 for c in range(d)], axis=0).astype(BF16)
        q_ref, k_ref, v_ref = out_refs[3 * gi:3 * gi + 3]
        for part, (gain_ref, out_ref) in enumerate(((qg_ref, q_ref), (kg_ref, k_ref), (None, v_ref))):
            col = (part * n_groups + gi) * width
            y_part = jnp.dot(lhs, w_ref[:, col:col + width].astype(BF16), preferred_element_type=F32)
            for cc in range(width // LANES):
                y = y_part[:, cc * LANES:(cc + 1) * LANES]
                if gain_ref is not None:
                    sq = y * y
                    both = jnp.sum(sq, axis=-1, keepdims=True)
                    low = jnp.sum(jnp.where(low_lanes, sq, 0.0), axis=-1, keepdims=True)
                    msq = jnp.where(low_lanes, low, both - low) * (1.0 / HEAD_DIM)
                    y = y * lax.rsqrt(msq + EPS) * gain_ref[...]
                y = y.astype(BF16)
                for c in range(d):
                    out_ref[0, c, :, cc * LANES:(cc + 1) * LANES] = y[c * per:(c + 1) * per]


def _qkv_project(x, gain, w, q_gain_row, k_gain_row, dilations, tm, name):
    batch, seq, d_model = x.shape
    width = w.shape[1] // (3 * len(dilations))
    const = lambda b, t: (0, 0)
    out_specs, out_shapes = [], []
    for d in dilations:
        out_specs += [pl.BlockSpec((1, d, tm // d, width), lambda b, t: (b, 0, t, 0))] * 3
        out_shapes += [jax.ShapeDtypeStruct((batch, d, seq // d, width), BF16)] * 3
    outs = pl.pallas_call(
        functools.partial(_qkv_kernel, width=width, dilations=tuple(dilations)),
        grid=(batch, seq // tm),
        in_specs=[pl.BlockSpec((1, tm, d_model), lambda b, t: (b, t, 0)),
                  pl.BlockSpec((1, d_model), const),
                  pl.BlockSpec(w.shape, const, pipeline_mode=pl.Buffered(1)),
                  pl.BlockSpec(q_gain_row.shape, const),
                  pl.BlockSpec(k_gain_row.shape, const)],
        out_specs=tuple(out_specs),
        out_shape=tuple(out_shapes),
        scratch_shapes=[pltpu.VMEM((d_model // LANES, tm, LANES), F32)],
        compiler_params=_params(2),
        name=name,
    )(x, gain, w, q_gain_row, k_gain_row)
    return [tuple(outs[3 * i:3 * i + 3]) for i in range(len(dilations))]


def _dsw_attn_kernel(q_ref, kc_ref, vc_ref, kp_ref, vp_ref, bias_ref, o_ref, stat_ref, *,
                     tq, blocks_per_seq):
    t = pl.program_id(0)
    n_blk = tq // DSW_BLK
    lane = lax.broadcasted_iota(jnp.int32, (DSW_BLK, LANES), 1)
    low_half = lane < HEAD_DIM
    n_pairs = q_ref.shape[1] // LANES
    for qi in range(n_blk):
        rows = slice(qi * DSW_BLK, (qi + 1) * DSW_BLK)
        first = (jnp.bitwise_and(t * n_blk + qi, blocks_per_seq - 1) == 0).astype(jnp.int32)
        stat_tile = jnp.zeros((DSW_BLK, LANES), F32)
        for hp in range(n_pairs):
            cols = slice(hp * LANES, (hp + 1) * LANES)
            q2 = q_ref[rows, cols]
            if qi == 0:
                k_prev, v_prev = kp_ref[:, cols], vp_ref[:, cols]
            else:
                prev = slice((qi - 1) * DSW_BLK, qi * DSW_BLK)
                k_prev, v_prev = kc_ref[prev, cols], vc_ref[prev, cols]
            k_cat = jnp.concatenate([k_prev, kc_ref[rows, cols]], axis=0)
            v_cat = jnp.concatenate([v_prev, vc_ref[rows, cols]], axis=0)
            outs = []
            for hh in range(2):
                head_lanes = low_half if hh == 0 else jnp.logical_not(low_half)
                qm = jnp.where(head_lanes, q2, jnp.zeros_like(q2))
                s = lax.dot_general(qm, k_cat, NT_DIMS, preferred_element_type=F32)
                s = s + bias_ref[0, hp * 2 + hh, first]
                m = jnp.max(s, axis=-1, keepdims=True)
                p = jnp.exp2(s - m)
                den = jnp.sum(p, axis=-1, keepdims=True)
                outs.append(jnp.dot(p.astype(BF16), v_cat, preferred_element_type=F32))
                head = hp * 2 + hh
                stat_tile = jnp.where(lane == head, m,
                                      jnp.where(lane == DSW_HEADS_PER_GROUP + head, den, stat_tile))
            o_ref[rows, cols] = jnp.where(low_half, outs[0], outs[1]).astype(o_ref.dtype)
        stat_ref[rows, :] = stat_tile


def _dsw_attention(q, k, v, bias_tiles, group, tq, blocks_per_seq):
    n_rows, width = q.shape
    assert blocks_per_seq & (blocks_per_seq - 1) == 0
    n_blk = tq // DSW_BLK
    cur = pl.BlockSpec((tq, width), lambda t: (t, 0))
    prev = pl.BlockSpec((DSW_BLK, width), lambda t: (jnp.maximum(t * n_blk - 1, 0), 0))
    hg = DSW_HEADS_PER_GROUP
    return pl.pallas_call(
        functools.partial(_dsw_attn_kernel, tq=tq, blocks_per_seq=blocks_per_seq),
        grid=(n_rows // tq,),
        in_specs=[cur, cur, cur, prev, prev,
                  pl.BlockSpec((1, hg, 2, DSW_BLK, 2 * DSW_BLK), lambda t: (group, 0, 0, 0, 0))],
        out_specs=(pl.BlockSpec((tq, width), lambda t: (t, 0)),
                   pl.BlockSpec((tq, LANES), lambda t: (t, 0))),
        out_shape=(jax.ShapeDtypeStruct((n_rows, width), BF16),
                   jax.ShapeDtypeStruct((n_rows, LANES), F32)),
        compiler_params=_params(1),
        name=f"dsw_attention_g{group}",
    )(q, k, v, k, v, bias_tiles)


def _merge_wo_kernel(*refs, dilations):
    n = len(dilations)
    o_refs, l_refs = refs[:n], refs[n:2 * n]
    x_ref, wo_ref, e_ref, out_ref = refs[2 * n:2 * n + 4]
    scratch = refs[2 * n + 4:]
    rows = x_ref.shape[1]
    outs, lses = [], []
    si = 0
    for d, o_ref, l_ref in zip(dilations, o_refs, l_refs):
        if d == 1:
            outs.append(o_ref[0, 0].astype(F32))
            lses.append(l_ref[0, 0])
            continue
        o_s, l_s = scratch[si], scratch[si + 1]
        si += 2
        n_tiles = o_s.shape[0]
        for c in range(d):
            piece = o_ref[0, c].astype(F32)
            for j in range(n_tiles):
                o_s[j, pl.ds(c, rows // d, stride=d), :] = piece[:, j * LANES:(j + 1) * LANES]
            l_s[pl.ds(c, rows // d, stride=d), :] = l_ref[0, c]
        outs.append(jnp.concatenate([o_s[j] for j in range(n_tiles)], axis=1))
        lses.append(l_s[...])
    lane = lax.broadcasted_iota(jnp.int32, lses[0].shape, 1)
    mx = functools.reduce(jnp.maximum, lses)
    es = [jnp.exp2(l - mx) for l in lses]
    sums = [pltpu.roll(l, LANES - DSW_HEADS_PER_GROUP, 1) for l in lses]
    total = functools.reduce(lambda a, b: a + b, [e * s for e, s in zip(es, sums)])
    merged = jnp.zeros(outs[0].shape, F32)
    for e, o in zip(es, outs):
        wgt = jnp.where(lane < DSW_HEADS_PER_GROUP, e / total, 0.0)
        hi = wgt.astype(BF16)
        lo = (wgt - hi.astype(F32)).astype(BF16)
        spread = (jnp.dot(hi, e_ref[...], preferred_element_type=F32)
                  + jnp.dot(lo, e_ref[...], preferred_element_type=F32))
        merged = merged + spread * o
    out_ref[0] = x_ref[0] + jnp.dot(merged.astype(BF16), wo_ref[...].astype(BF16),
                                    preferred_element_type=F32)


def _merge_wo(os, lses, x, wo, dilations, tm):
    batch, seq, d_model = x.shape
    width = os[0].shape[-1]
    head_of_lane = jnp.arange(width) // HEAD_DIM
    expand = (jnp.arange(LANES)[:, None] == head_of_lane[None, :]).astype(BF16)
    const = lambda b, t: (0, 0)
    blk = lambda d, w: pl.BlockSpec((1, d, tm // d, w), lambda b, t: (b, 0, t, 0))
    x_spec = pl.BlockSpec((1, tm, d_model), lambda b, t: (b, t, 0))
    scratch = []
    for d in dilations:
        if d > 1:
            scratch += [pltpu.VMEM((width // LANES, tm, LANES), F32), pltpu.VMEM((tm, LANES), F32)]
    return pl.pallas_call(
        functools.partial(_merge_wo_kernel, dilations=tuple(dilations)),
        grid=(batch, seq // tm),
        in_specs=([blk(d, width) for d in dilations] + [blk(d, LANES) for d in dilations]
                  + [x_spec, pl.BlockSpec(wo.shape, const), pl.BlockSpec(expand.shape, const)]),
        out_specs=x_spec,
        out_shape=jax.ShapeDtypeStruct(x.shape, F32),
        scratch_shapes=scratch,
        compiler_params=_params(2),
        name="dsw_merge_wo",
    )(*os, *lses, x, wo, expand)


def _ffn_kernel(*refs, has_attn):
    if has_attn:
        h_ref, a_ref, wo_ref, g_ref, w1_ref, w2_ref, out_ref = refs
        h = h_ref[...] + jnp.dot(a_ref[...], wo_ref[...].astype(BF16), preferred_element_type=F32)
    else:
        h_ref, g_ref, w1_ref, w2_ref, out_ref = refs
        h = h_ref[...]
    u = _rmsnorm_bf16(h, g_ref[...])
    a = jnp.dot(u, w1_ref[0].astype(BF16), preferred_element_type=F32)
    a = jnp.square(jnp.maximum(a, 0.0)).astype(BF16)
    out_ref[...] = h + jnp.dot(a, w2_ref[0].astype(BF16), preferred_element_type=F32)


def _ffn(h, gain, w1, w2, layer, tm, attn=None, wo=None, name="ffn"):
    n_rows, d_model = h.shape
    const = lambda i: (0, 0)
    row_spec = pl.BlockSpec((tm, d_model), lambda i: (i, 0))
    in_specs = [row_spec]
    args = [h]
    if attn is not None:
        in_specs += [pl.BlockSpec((tm, attn.shape[1]), lambda i: (i, 0)), pl.BlockSpec(wo.shape, const)]
        args += [attn, wo]
    in_specs += [pl.BlockSpec((1, d_model), const),
                 pl.BlockSpec((1,) + w1.shape[1:], lambda i: (layer, 0, 0), pipeline_mode=pl.Buffered(1)),
                 pl.BlockSpec((1,) + w2.shape[1:], lambda i: (layer, 0, 0), pipeline_mode=pl.Buffered(1))]
    args += [gain, w1, w2]
    return pl.pallas_call(
        functools.partial(_ffn_kernel, has_attn=attn is not None),
        grid=(n_rows // tm,),
        in_specs=in_specs,
        out_specs=row_spec,
        out_shape=jax.ShapeDtypeStruct(h.shape, F32),
        compiler_params=_params(1),
        name=name,
    )(*args)


def _moba_kernel(q_ref, k_ref, v_ref, wn_ref, o_ref,
                 bias_s, kp_s, vt_s, qa_s, s_s, acc_s, *, n_blocks, chunk):
    b = pl.program_id(1)
    blk = MOBA_BLOCK
    seq = n_blocks * blk
    span = chunk * blk
    lane = lax.broadcasted_iota(jnp.int32, (blk, LANES), 1)
    low_lanes = lane < HEAD_DIM
    v_rows = vt_s.shape[1]
    ones_row = (lax.broadcasted_iota(jnp.int32, (v_rows - HEAD_DIM, blk), 0) == 0).astype(F32)

    @pl.when(b == 0)
    def _build_bias():
        for hh in range(2):
            for dlt in range(n_blocks):
                u = jnp.concatenate([wn_ref[0, hh:hh + 1, (dlt + 1) * blk:(dlt + 2) * blk],
                                     wn_ref[0, hh:hh + 1, dlt * blk:(dlt + 1) * blk]], axis=1)
                r = n_blocks - 1 - dlt
                bias_s[hh, r * blk:(r + 1) * blk, :] = _toeplitz(u, blk)[:, :blk] * LOG2E
            bias_s[hh, seq:, :] = jnp.full((span - blk, blk), NEG, F32)

    sub = lax.broadcasted_iota(jnp.int32, (n_blocks, LANES), 0)
    kmean = jnp.zeros((n_blocks, LANES), F32)
    kamax = []
    for j in range(n_blocks):
        rows = slice(j * blk, (j + 1) * blk)
        kj = k_ref[0, rows, :].astype(F32)
        vj_t = v_ref[0, rows, :].astype(F32).T
        kmean = jnp.where(sub == j, jnp.mean(kj, axis=0, keepdims=True), kmean)
        kamax.append(jnp.max(jnp.abs(kj), axis=0, keepdims=True))
        aux0 = jnp.logical_or(lane == HEAD_DIM + j, lane == HEAD_DIM + n_blocks).astype(F32)
        aux1 = jnp.logical_or(lane == j, lane == n_blocks).astype(F32)
        kp_s[0, rows, :] = jnp.where(low_lanes, kj, aux0).astype(BF16)
        kp_s[1, rows, :] = jnp.where(low_lanes, aux1, kj).astype(BF16)
        for hh in range(2):
            vt_s[hh, :, rows] = jnp.concatenate(
                [vj_t[hh * HEAD_DIM:(hh + 1) * HEAD_DIM], ones_row], axis=0).astype(BF16)
    km_hi = kmean.astype(BF16)
    km_lo = (kmean - km_hi.astype(F32)).astype(BF16)
    kamax_upto = [kamax[0]]
    for j in range(1, n_blocks):
        kamax_upto.append(jnp.maximum(kamax_upto[-1], kamax[j]))
    pad_rows = jnp.zeros((6, LANES), F32)
    bias_max = [jnp.max(wn_ref[0, hh:hh + 1, blk:], axis=1, keepdims=True) * LOG2E for hh in range(2)]
    bias_zero = [wn_ref[0, hh:hh + 1, blk:blk + 1] * LOG2E for hh in range(2)]

    blk_row = lax.broadcasted_iota(jnp.int32, (n_blocks, blk), 0)
    blk_row_f = blk_row.astype(F32)
    gap = jnp.full((1, blk), -jnp.inf, F32)
    for n in range(n_blocks):
        rows = slice(n * blk, (n + 1) * blk)
        q2 = q_ref[0, rows, :]
        q_abs = jnp.abs(q2)
        k_bounds = jnp.concatenate([kamax_upto[n], kamax[n], pad_rows], axis=0).astype(BF16)
        for hh in range(2):
            own = low_lanes if hh == 0 else jnp.logical_not(low_lanes)
            qm = jnp.where(own, q2, jnp.zeros_like(q2))
            gate = (lax.dot_general(km_hi, qm, NT_DIMS, preferred_element_type=F32)
                    + lax.dot_general(km_lo, qm, NT_DIMS, preferred_element_type=F32))
            gate = jnp.where(blk_row < n, gate, -jnp.inf)
            chosen = blk_row == n
            for _ in range(min(MOBA_TOPK, n)):
                best = jnp.max(gate, axis=0, keepdims=True)
                cand = jnp.where(gate == best, blk_row_f, float(n_blocks))
                first = jnp.min(cand, axis=0, keepdims=True)
                pick = blk_row_f == first
                chosen = jnp.logical_or(chosen, pick)
                gate = jnp.where(pick, -jnp.inf, gate)
            pen_t = jnp.where(chosen, 0.0, NEG)
            sums = lax.dot_general(k_bounds, jnp.where(own, q_abs, jnp.zeros_like(q_abs)),
                                   NT_DIMS, preferred_element_type=F32)
            bound = sums[0:1] + bias_max[hh] + 1.0
            gap = jnp.maximum(gap, bound - (bias_zero[hh] - sums[1:2]))
            pad_lo = HEAD_DIM if hh == 0 else 0
            pieces = [jnp.zeros((pad_lo, blk), F32)] if pad_lo else []
            pieces += [pen_t, -bound, jnp.zeros((LANES - pad_lo - n_blocks - 1, blk), F32)]
            pen = jnp.concatenate(pieces, axis=0).T
            qa_s[hh, rows, :] = jnp.where(own, q2, pen.astype(BF16))

    def emit(n, width):
        cols = width * blk
        o_t = jnp.concatenate([acc_s[hh, :HEAD_DIM, :cols] / acc_s[hh, HEAD_DIM:HEAD_DIM + 1, :cols]
                               for hh in range(2)], axis=0)
        o_ref[0, pl.ds(_aligned(n * blk, blk), cols), :] = o_t.T.astype(o_ref.dtype)

    def raw_scores(hh, n, key0, n_keys, width):
        q_aug = qa_s[hh, pl.ds(_aligned(n * blk, blk), width * blk), :]
        s_t = lax.dot_general(kp_s[hh, key0:key0 + n_keys, :], q_aug, NT_DIMS,
                              preferred_element_type=F32)
        bias = [bias_s[hh, pl.ds(_aligned((n_blocks - 1 - n - i) * blk + key0, blk), n_keys), :]
                for i in range(width)]
        return s_t + (bias[0] if width == 1 else jnp.concatenate(bias, axis=1))

    acc_s[...] = jnp.ones(acc_s.shape, F32)
    n_groups = n_blocks // chunk
    bound_is_tight = jnp.max(gap) < EXP_RANGE
    one = jnp.minimum(b + 1, 1)

    @pl.when(bound_is_tight)
    def _single_pass():
        def q_pair(pair):
            n = 2 * pair
            wide = 4 * span
            own_chunk, own_blocks = divmod(n + 2, 4 * chunk)
            emit(jnp.maximum(n - 2, 0), 2)
            ranges = [(c * wide, wide) for c in range(own_chunk)]
            if own_blocks:
                ranges.append((own_chunk * wide, own_blocks * blk))
            for hh in range(2):
                acc = None
                for key0, n_keys in ranges:
                    p_t = jnp.exp2(raw_scores(hh, n, key0, n_keys, 2)).astype(BF16)
                    part = jnp.dot(vt_s[hh, :, key0:key0 + n_keys], p_t,
                                   preferred_element_type=F32)
                    acc = part if acc is None else acc + part
                acc_s[hh] = acc

        for first in range(0, n_blocks // 2, MOBA_PAIRS_PER_REGION):
            def region(t, c, first=first):
                for pair in range(first, first + MOBA_PAIRS_PER_REGION):
                    q_pair(pair)
                return c
            lax.fori_loop(0, one, region, 0)
        emit(n_blocks - 2, 2)

    @pl.when(jnp.logical_not(bound_is_tight))
    def _two_pass():
        def scores(hh, n, n_chunks):
            m = None
            for c in range(n_chunks):
                s_t = raw_scores(hh, n, c * span, span, 1)
                s_s[hh, c * span:(c + 1) * span, :] = s_t
                cm = jnp.max(s_t, axis=0, keepdims=True)
                m = cm if m is None else jnp.maximum(m, cm)
            return m

        def weighted_values(hh, m, n_chunks):
            n_keys = n_chunks * span
            p_t = jnp.exp2(s_s[hh, :n_keys, :] - m).astype(BF16)
            return jnp.dot(vt_s[hh, :, :n_keys], p_t, preferred_element_type=F32)

        def step(n, m0, n_chunks, next_chunks):
            emit(jnp.maximum(n - 1, 0), 1)
            acc0 = weighted_values(0, m0, n_chunks)
            m1 = scores(1, n, n_chunks)
            m0_next = scores(0, n + 1, next_chunks) if next_chunks else m0
            acc1 = weighted_values(1, m1, n_chunks)
            acc_s[0, :, :blk] = acc0
            acc_s[1, :, :blk] = acc1
            return m0_next

        m0 = scores(0, 0, 1)
        for g in range(n_groups):
            m0 = lax.fori_loop(0, chunk - 1,
                               lambda t, m, g=g: step(g * chunk + t, m, g + 1, g + 1), m0)
            m0 = step(g * chunk + chunk - 1, m0, g + 1, g + 2 if g + 1 < n_groups else 0)
        emit(n_blocks - 1, 1)


def _moba_attention(q, k, v, moba_vec, batch, seq, chunk):
    n_rows, width = q.shape
    n_pairs = width // LANES
    n_blocks = seq // MOBA_BLOCK
    assert n_blocks % chunk == 0
    wn = moba_vec.reshape(n_pairs, 2, moba_vec.shape[1])
    spec = pl.BlockSpec((1, seq, LANES), lambda p, b: (b, 0, p))
    shape3 = (batch, seq, width)
    out = pl.pallas_call(
        functools.partial(_moba_kernel, n_blocks=n_blocks, chunk=chunk),
        grid=(n_pairs, batch),
        in_specs=[spec, spec, spec, pl.BlockSpec((1, 2, wn.shape[2]), lambda p, b: (p, 0, 0))],
        out_specs=spec,
        out_shape=jax.ShapeDtypeStruct(shape3, BF16),
        scratch_shapes=[pltpu.VMEM((2, seq + (chunk - 1) * MOBA_BLOCK, MOBA_BLOCK), F32),
                        pltpu.VMEM((2, seq, LANES), BF16),
                        pltpu.VMEM((2, HEAD_DIM + BF16_SUBLANES, seq), BF16),
                        pltpu.VMEM((2, seq, LANES), BF16),
                        pltpu.VMEM((2, seq, MOBA_BLOCK), F32),
                        pltpu.VMEM((2, HEAD_DIM + BF16_SUBLANES, 2 * MOBA_BLOCK), F32)],
        compiler_params=_params(2),
        name="moba_attention",
    )(q.reshape(shape3), k.reshape(shape3), v.reshape(shape3), wn)
    return out.reshape(n_rows, width)


def _gain_row(gain, size, scale):
    return (jnp.tile(gain.astype(F32), size // HEAD_DIM) * scale).reshape(1, size)


def kernel(x, rel_bias, norm_mix, norm_ffn, a_w_qkv, a_q_gain, a_k_gain, a_w_o,
           b_w_qkv, b_q_gain, b_k_gain, b_w_o, ffn_w1, ffn_w2):
    batch, seq, d_model = x.shape
    n_rows = batch * seq
    n_groups = len(DSW_GROUPS)
    dilations = [d for _, d in DSW_GROUPS]
    gw = DSW_HEADS_PER_GROUP * HEAD_DIM
    moba_heads = b_w_o.shape[1] // HEAD_DIM

    dsw_vec, moba_vec = _bias_tables(rel_bias, moba_heads, seq)
    dsw_tiles = _dsw_bias_tiles(dsw_vec)

    qkv = _qkv_project(x, norm_mix[0].reshape(1, d_model), a_w_qkv[0],
                       _gain_row(a_q_gain[0], LANES, SCALE * LOG2E), _gain_row(a_k_gain[0], LANES, 1.0),
                       dilations, tm=WIDE_ROWS_PER_STEP, name="dsw_qkv")
    os, lses = [], []
    for g, d in enumerate(dilations):
        q_g, k_g, v_g = (a.reshape(n_rows, gw) for a in qkv[g])
        o_g, lse_g = _dsw_attention(q_g, k_g, v_g, dsw_tiles, g, tq=DSW_ROWS_PER_STEP,
                                    blocks_per_seq=seq // d // DSW_BLK)
        os.append(o_g.reshape(batch, d, seq // d, gw))
        lses.append(lse_g.reshape(batch, d, seq // d, LANES))
    h = _merge_wo(os, lses, x, a_w_o[0], dilations, tm=WIDE_ROWS_PER_STEP)
    h = h.reshape(n_rows, d_model)
    w1, w2 = ffn_w1, ffn_w2
    h = _ffn(h, norm_ffn[0].reshape(1, d_model), w1, w2, 0, tm=ROWS_PER_STEP, name="ffn0")

    width = moba_heads * HEAD_DIM
    (q, k, v), = _qkv_project(
        h.reshape(batch, seq, d_model), norm_mix[1].reshape(1, d_model), b_w_qkv[0],
        _gain_row(b_q_gain[0], LANES, SCALE * LOG2E), _gain_row(b_k_gain[0], LANES, 1.0),
        [1], tm=WIDE_ROWS_PER_STEP, name="moba_qkv")
    q, k, v = (a.reshape(n_rows, width) for a in (q, k, v))
    attn = _moba_attention(q, k, v, moba_vec, batch, seq, chunk=MOBA_CHUNK_BLOCKS)
    h = _ffn(h, norm_ffn[1].reshape(1, d_model), w1, w2, 1,
             tm=ROWS_PER_STEP, attn=attn, wo=b_w_o[0], name="wo_ffn1")
    return h.reshape(batch, seq, d_model)
```
